```python
import math
import jax, jax.numpy as jnp
from jax import lax
import numpy as np

D_MODEL = 1024
BATCH = 16
SEQ = 2048
DEPTH = 2
DEC_BATCH = 16
DEC_SEQ = 64
PAST_LEN = 1024

CHUNK = 64
EPS = 1e-6
HALF = 0.5
D_FF = 2816
PLE_DIM = 256
SSM_HEAD_DIM = 64
D_INNER = D_MODEL
SSM_HEADS = D_INNER // SSM_HEAD_DIM
N_GROUPS = 4
HEADS_PER_GROUP = SSM_HEADS // N_GROUPS
D_STATE = 128
CONV_W = 4
CONV_CH = D_INNER + 2 * N_GROUPS * D_STATE
DSA_HEAD_DIM = 128
DSA_HEADS = D_MODEL // DSA_HEAD_DIM
KV_HEADS = 2
KV_REP = DSA_HEADS // KV_HEADS
IDX_HEADS = 4
IDX_DIM = 64
TOPK_MAX = 256
Q_BLOCK = 128
BAND_HEAD_DIM = 64
BAND_HEADS = D_MODEL // BAND_HEAD_DIM
LEFT_CHUNKS = 8
BAND = LEFT_CHUNKS * CHUNK
REL_CLIP = 256
N_BRANCH = 3
IN_WIDTHS = (D_INNER, CONV_CH, SSM_HEADS,
             DSA_HEADS * DSA_HEAD_DIM, KV_HEADS * DSA_HEAD_DIM, KV_HEADS * DSA_HEAD_DIM,
             IDX_HEADS * IDX_DIM, IDX_DIM, IDX_HEADS,
             BAND_HEADS * BAND_HEAD_DIM, BAND_HEADS * BAND_HEAD_DIM, BAND_HEADS * BAND_HEAD_DIM)
IN_WIDTH = sum(IN_WIDTHS)
IN_SPLITS = tuple(int(s) for s in np.cumsum(IN_WIDTHS)[:-1])

kernel_name = 'hybrid_streaming_encoder_step'


def rmsnorm(x, g):
    xf = x.astype(jnp.float32)
    xf = xf * lax.rsqrt(jnp.mean(xf * xf, axis=-1, keepdims=True) + EPS)
    return (xf * g.astype(jnp.float32)).astype(x.dtype)


def swiglu(u, w13, w2):
    a, b = jnp.split(u @ w13, 2, axis=-1)
    return (jax.nn.silu(a) * b) @ w2


def ssd_scan(xh, dt, a, bm, cm, h0):
    b, t = xh.shape[:2]
    q = min(CHUNK, t)
    nc = t // q
    g, hg = N_GROUPS, HEADS_PER_GROUP
    xdt = (xh.astype(jnp.float32) * dt[..., None]).reshape(b, nc, q, g, hg, SSM_HEAD_DIM)
    cum = jnp.cumsum((dt * a).reshape(b, nc, q, g, hg), axis=2)
    bmc = bm.astype(jnp.float32).reshape(b, nc, q, g, D_STATE)
    cmc = cm.astype(jnp.float32).reshape(b, nc, q, g, D_STATE)
    causal = jnp.tril(jnp.ones((q, q), dtype=bool))[:, :, None, None]
    seg = cum[:, :, :, None] - cum[:, :, None, :]
    decay_ls = jnp.exp(jnp.where(causal, seg, -jnp.inf))
    cb = jnp.einsum('bclgn,bcsgn->bclsg', cmc, bmc)
    y_diag = jnp.einsum('bclsgh,bcsghp->bclghp', cb[..., None] * decay_ls, xdt)
    decay_s = jnp.exp(cum[:, :, -1:] - cum)
    states = jnp.einsum('bcsgn,bcsghp->bcghpn', bmc, xdt * decay_s[..., None])
    chunk_decay = jnp.exp(cum[:, :, -1])

    def step(h, inp):
        st, dec = inp
        return dec[..., None, None] * h + st, h

    h_init = h0.astype(jnp.float32).reshape(b, g, hg, SSM_HEAD_DIM, D_STATE)
    h_fin, h_prev = lax.scan(step, h_init, (jnp.swapaxes(states, 0, 1), jnp.swapaxes(chunk_decay, 0, 1)))
    h_prev = jnp.swapaxes(h_prev, 0, 1)
    y_off = jnp.einsum('bclgn,bcghpn->bclghp', cmc, h_prev) * jnp.exp(cum)[..., None]
    y = (y_diag + y_off).reshape(b, t, SSM_HEADS, SSM_HEAD_DIM)
    return y, h_fin.reshape(b, SSM_HEADS, SSM_HEAD_DIM, D_STATE)


def mamba_branch(z, xbc, dt_raw, conv_buf, h0, conv_w, conv_b, dt_bias, a_log, d_skip, norm_g):
    b, t, _ = xbc.shape
    xp = jnp.concatenate([conv_buf.astype(xbc.dtype), xbc], axis=1)
    acc = conv_b
    for k in range(CONV_W):
        acc = acc + xp[:, k:k + t] * conv_w[k]
    xbc_c = jax.nn.silu(acc)
    xs, bm, cm = jnp.split(xbc_c, [D_INNER, D_INNER + N_GROUPS * D_STATE], axis=-1)
    dt = jax.nn.softplus(dt_raw.astype(jnp.float32) + dt_bias.astype(jnp.float32))
    a = -jnp.exp(a_log.astype(jnp.float32))
    xh = xs.reshape(b, t, SSM_HEADS, SSM_HEAD_DIM)
    y, h = ssd_scan(xh, dt, a, bm.reshape(b, t, N_GROUPS, D_STATE), cm.reshape(b, t, N_GROUPS, D_STATE), h0)
    y = y + d_skip.astype(jnp.float32)[:, None] * xh.astype(jnp.float32)
    gsz = D_INNER // N_GROUPS
    y = y.reshape(b, t, N_GROUPS, gsz) * jax.nn.silu(z.astype(jnp.float32)).reshape(b, t, N_GROUPS, gsz)
    y = y * lax.rsqrt(jnp.mean(y * y, axis=-1, keepdims=True) + EPS)
    y = (y.reshape(b, t, D_INNER) * norm_g.astype(jnp.float32)).astype(z.dtype)
    return y, xp[:, -(CONV_W - 1):], h.astype(h0.dtype)


def dsa_core(q, qi, wi, qpos, k, v, ki, kpos, topk):
    b, t = q.shape[:2]
    logits = jnp.einsum('bthd,bsd->bths', qi, ki) * (IDX_DIM ** -0.5)
    score = jnp.einsum('bths,bth->bts', jax.nn.relu(logits), wi).astype(jnp.float32)
    allowed = (kpos[None, :] // CHUNK) <= (qpos[:, None] // CHUNK)
    score = jnp.where(allowed[None], score, -jnp.inf)
    top_s, top_i = lax.top_k(score, topk)
    valid = jnp.isfinite(top_s)
    kg = jax.vmap(lambda kk, ii: kk[ii])(k, top_i)
    vg = jax.vmap(lambda vv, ii: vv[ii])(v, top_i)
    qg = q.reshape(b, t, KV_HEADS, KV_REP, DSA_HEAD_DIM)
    s = jnp.einsum('btgrd,btkgd->btgrk', qg, kg).astype(jnp.float32) * (DSA_HEAD_DIM ** -0.5)
    s = jnp.where(valid[:, :, None, None, :], s, -jnp.inf)
    pr = jax.nn.softmax(s, axis=-1).astype(v.dtype)
    o = jnp.einsum('btgrk,btkgd->btgrd', pr, vg)
    return o.reshape(b, t, DSA_HEADS * DSA_HEAD_DIM)


def dsa_prompt(q, qi, wi, k, v, ki):
    b, t = q.shape[:2]
    nb = t // Q_BLOCK
    topk = min(TOPK_MAX, t // 4)
    kpos = jnp.arange(t)

    def blk(args):
        j, qj, qij, wij = args
        qpos = j * Q_BLOCK + jnp.arange(Q_BLOCK)
        return dsa_core(qj, qij, wij, qpos, k, v, ki, kpos, topk)

    def to_blocks(a):
        return jnp.swapaxes(a.reshape((b, nb, Q_BLOCK) + a.shape[2:]), 0, 1)

    out = lax.map(blk, (jnp.arange(nb), to_blocks(q), to_blocks(qi), to_blocks(wi)))
    return jnp.swapaxes(out, 0, 1).reshape(b, t, DSA_HEADS * DSA_HEAD_DIM)


def band_core(q, k, v, qpos, kpos, rel_bias):
    rel = jnp.clip(qpos[:, None] - kpos[None, :], -REL_CLIP, REL_CLIP) + REL_CLIP
    bias = jnp.transpose(rel_bias[rel], (2, 0, 1)).astype(jnp.float32)
    dchunk = qpos[:, None] // CHUNK - kpos[None, :] // CHUNK
    allowed = (kpos[None, :] >= 0) & (dchunk >= 0) & (dchunk <= LEFT_CHUNKS)
    s = jnp.einsum('bthd,bshd->bhts', q, k).astype(jnp.float32) * (BAND_HEAD_DIM ** -0.5) + bias[None]
    s = jnp.where(allowed[None, None], s, -jnp.inf)
    pr = jax.nn.softmax(s, axis=-1).astype(v.dtype)
    o = jnp.einsum('bhts,bshd->bthd', pr, v)
    return o.reshape(q.shape[0], q.shape[1], BAND_HEADS * BAND_HEAD_DIM)


def band_prompt(q, k, v, rel_bias):
    b, t = q.shape[:2]
    nc = t // CHUNK
    pad = ((0, 0), (BAND, 0), (0, 0), (0, 0))
    kp, vp = jnp.pad(k, pad), jnp.pad(v, pad)
    band_off = jnp.arange(BAND + CHUNK) - BAND

    def blk(c):
        s0 = c * CHUNK
        qj = lax.dynamic_slice_in_dim(q, s0, CHUNK, axis=1)
        kj = lax.dynamic_slice_in_dim(kp, s0, BAND + CHUNK, axis=1)
        vj = lax.dynamic_slice_in_dim(vp, s0, BAND + CHUNK, axis=1)
        return band_core(qj, kj, vj, s0 + jnp.arange(CHUNK), s0 + band_off, rel_bias)

    out = lax.map(blk, jnp.arange(nc))
    return jnp.swapaxes(out, 0, 1).reshape(b, t, BAND_HEADS * BAND_HEAD_DIM)


def trunk_layer(x, p, lw, cache):
    b, t, _ = x.shape
    g = lw['norm_g']
    x = x + HALF * rmsnorm(swiglu(rmsnorm(x, g[0]), lw['ffn_w13'][0], lw['ffn_w2'][0]), g[1])
    u = rmsnorm(x, g[2])
    z, xbc, dt_raw, qb, kb, vb, qi, ki, wi, qc, kc, vc = jnp.split(u @ lw['w_in'], IN_SPLITS, axis=-1)
    qb = qb.reshape(b, t, DSA_HEADS, DSA_HEAD_DIM)
    kb = kb.reshape(b, t, KV_HEADS, DSA_HEAD_DIM)
    vb = vb.reshape(b, t, KV_HEADS, DSA_HEAD_DIM)
    qi = qi.reshape(b, t, IDX_HEADS, IDX_DIM)
    wi = wi * (IDX_HEADS ** -0.5)
    qc = qc.reshape(b, t, BAND_HEADS, BAND_HEAD_DIM)
    kc = kc.reshape(b, t, BAND_HEADS, BAND_HEAD_DIM)
    vc = vc.reshape(b, t, BAND_HEADS, BAND_HEAD_DIM)

    if cache is None:
        conv_buf = jnp.zeros((b, CONV_W - 1, CONV_CH), x.dtype)
        h0 = jnp.zeros((b, SSM_HEADS, SSM_HEAD_DIM, D_STATE), jnp.float32)
    else:
        conv_buf, h0 = cache['conv'], cache['ssm']
    ya, conv_new, h_new = mamba_branch(z, xbc, dt_raw, conv_buf, h0, lw['conv_w'], lw['conv_b'],
                                       lw['dt_bias'], lw['a_log'], lw['d_skip'], lw['ssm_norm_g'])

    if cache is None:
        yb = dsa_prompt(qb, qi, wi, kb, vb, ki)
        yc = band_prompt(qc, kc, vc, lw['rel_bias'])
        band_rows = min(BAND, t)
        band_k_out, band_v_out = kc[:, t - band_rows:], vc[:, t - band_rows:]
    else:
        past = cache['dsa_k'].shape[1]
        qpos = past + jnp.arange(t)
        k_all = jnp.concatenate([cache['dsa_k'].astype(kb.dtype), kb], axis=1)
        v_all = jnp.concatenate([cache['dsa_v'].astype(vb.dtype), vb], axis=1)
        ki_all = jnp.concatenate([cache['idx_k'].astype(ki.dtype), ki], axis=1)
        topk = min(TOPK_MAX, (past + t) // 4)
        yb = dsa_core(qb, qi, wi, qpos, k_all, v_all, ki_all, jnp.arange(past + t), topk)
        nrows = cache['band_k'].shape[1]
        kc_all = jnp.concatenate([cache['band_k'].astype(kc.dtype), kc], axis=1)
        vc_all = jnp.concatenate([cache['band_v'].astype(vc.dtype), vc], axis=1)
        kpos_c = past - nrows + jnp.arange(nrows + t)
        yc = band_core(qc, kc_all, vc_all, qpos, kpos_c, lw['rel_bias'])
        band_k_out, band_v_out = kc, vc

    gates = jax.nn.sigmoid((u @ lw['w_gate'] + lw['b_gate']).astype(jnp.float32)).astype(x.dtype)
    gates = gates.reshape(b, t, N_BRANCH, D_MODEL)
    branches = jnp.stack([ya, yb, yc], axis=2)
    proj = jnp.einsum('btkc,kcd->btkd', branches, lw['w_branch'])
    mix = jnp.sum(gates * proj, axis=2) @ lw['w_out']
    x = x + rmsnorm(mix, g[3])
    x = x + HALF * rmsnorm(swiglu(rmsnorm(x, g[4]), lw['ffn_w13'][1], lw['ffn_w2'][1]), g[5])
    e = p.astype(x.dtype) @ lw['w_ple']
    pg = jax.nn.sigmoid(rmsnorm(x, g[6]) @ lw['w_ple_gate'])
    x = x + rmsnorm(pg * e, g[7])
    return x, (kb, vb, ki, band_k_out, band_v_out, h_new, conv_new)


def setup_inputs(seed: int = 0) -> dict:
    key = jax.random.key(seed)
    ks = list(jax.random.split(key, 32))

    def nrm(shape, scale):
        return jax.random.normal(ks.pop(), shape, jnp.float32) * scale

    band_rows = min(BAND, PAST_LEN)
    dt0 = jnp.exp(jax.random.uniform(ks.pop(), (DEPTH, SSM_HEADS), jnp.float32,
                                     math.log(1e-3), math.log(1e-1)))
    a_log = jnp.log(jax.random.uniform(ks.pop(), (DEPTH, SSM_HEADS), jnp.float32, 1.0, 16.0))
    return {
        'x_prompt': nrm((BATCH, SEQ, D_MODEL), 1.0),
        'x_sample': nrm((DEC_BATCH, DEC_SEQ, D_MODEL), 1.0),
        'p_prompt': nrm((DEPTH, BATCH, SEQ, PLE_DIM), 1.0),
        'p_sample': nrm((DEPTH, DEC_BATCH, DEC_SEQ, PLE_DIM), 1.0),
        'cache_dsa_k': nrm((DEPTH, DEC_BATCH, PAST_LEN, KV_HEADS, DSA_HEAD_DIM), 1.0),
        'cache_dsa_v': nrm((DEPTH, DEC_BATCH, PAST_LEN, KV_HEADS, DSA_HEAD_DIM), 1.0),
        'cache_idx_k': nrm((DEPTH, DEC_BATCH, PAST_LEN, IDX_DIM), 1.0),
        'cache_band_k': nrm((DEPTH, DEC_BATCH, band_rows, BAND_HEADS, BAND_HEAD_DIM), 1.0),
        'cache_band_v': nrm((DEPTH, DEC_BATCH, band_rows, BAND_HEADS, BAND_HEAD_DIM), 1.0),
        'state_ssm': nrm((DEPTH, DEC_BATCH, SSM_HEADS, SSM_HEAD_DIM, D_STATE), 0.1),
        'state_conv': nrm((DEPTH, DEC_BATCH, CONV_W - 1, CONV_CH), 1.0),
        'norm_g': 1.0 + nrm((DEPTH, 8, D_MODEL), 0.05),
        'ffn_w13': nrm((DEPTH, 2, D_MODEL, 2 * D_FF), D_MODEL ** -0.5),
        'ffn_w2': nrm((DEPTH, 2, D_FF, D_MODEL), D_FF ** -0.5),
        'w_in': nrm((DEPTH, D_MODEL, IN_WIDTH), D_MODEL ** -0.5),
        'conv_w': nrm((DEPTH, CONV_W, CONV_CH), CONV_W ** -0.5),
        'conv_b': nrm((DEPTH, CONV_CH), 0.02),
        'dt_bias': dt0 + jnp.log(-jnp.expm1(-dt0)),
        'a_log': a_log,
        'd_skip': 1.0 + nrm((DEPTH, SSM_HEADS), 0.1),
        'ssm_norm_g': 1.0 + nrm((DEPTH, D_INNER), 0.05),
        'rel_bias': nrm((DEPTH, 2 * REL_CLIP + 1, BAND_HEADS), 0.5),
        'w_gate': nrm((DEPTH, D_MODEL, N_BRANCH * D_MODEL), D_MODEL ** -0.5),
        'b_gate': nrm((DEPTH, N_BRANCH * D_MODEL), 0.01),
        'w_branch': nrm((DEPTH, N_BRANCH, D_MODEL, D_MODEL), D_MODEL ** -0.5),
        'w_out': nrm((DEPTH, D_MODEL, D_MODEL), D_MODEL ** -0.5),
        'w_ple': nrm((DEPTH, PLE_DIM, D_MODEL), PLE_DIM ** -0.5),
        'w_ple_gate': nrm((DEPTH, D_MODEL, D_MODEL), D_MODEL ** -0.5),
    }


def reference(x_prompt, x_sample, p_prompt, p_sample, cache_dsa_k, cache_dsa_v, cache_idx_k,
              cache_band_k, cache_band_v, state_ssm, state_conv, norm_g, ffn_w13, ffn_w2, w_in,
              conv_w, conv_b, dt_bias, a_log, d_skip, ssm_norm_g, rel_bias, w_gate, b_gate,
              w_branch, w_out, w_ple, w_ple_gate):
    yp, ys = x_prompt, x_sample
    st_p, st_s = [], []
    for i in range(DEPTH):
        lw = {'norm_g': norm_g[i], 'ffn_w13': ffn_w13[i], 'ffn_w2': ffn_w2[i], 'w_in': w_in[i],
              'conv_w': conv_w[i], 'conv_b': conv_b[i], 'dt_bias': dt_bias[i], 'a_log': a_log[i],
              'd_skip': d_skip[i], 'ssm_norm_g': ssm_norm_g[i], 'rel_bias': rel_bias[i],
              'w_gate': w_gate[i], 'b_gate': b_gate[i], 'w_branch': w_branch[i], 'w_out': w_out[i],
              'w_ple': w_ple[i], 'w_ple_gate': w_ple_gate[i]}
        yp, sp = trunk_layer(yp, p_prompt[i], lw, None)
        st_p.append(sp)
        cache = {'dsa_k': cache_dsa_k[i], 'dsa_v': cache_dsa_v[i], 'idx_k': cache_idx_k[i],
                 'band_k': cache_band_k[i], 'band_v': cache_band_v[i],
                 'ssm': state_ssm[i], 'conv': state_conv[i]}
        ys, ss = trunk_layer(ys, p_sample[i], lw, cache)
        st_s.append(ss)
    dsa_k_p, dsa_v_p, idx_k_p, band_k_p, band_v_p, ssm_p, conv_p = [jnp.stack(c) for c in zip(*st_p)]
    dsa_k_s, dsa_v_s, idx_k_s, band_k_s, band_v_s, ssm_s, conv_s = [jnp.stack(c) for c in zip(*st_s)]
    return (yp, ys, dsa_k_p, dsa_v_p, idx_k_p, band_k_p, band_v_p, ssm_p, conv_p,
            dsa_k_s, dsa_v_s, idx_k_s, band_k_s, band_v_s, ssm_s, conv_s)
```

```python
import functools

import numpy as np
import jax
import jax.numpy as jnp
from jax import lax
from jax.experimental import pallas as pl
from jax.experimental.pallas import tpu as pltpu

F32 = jnp.float32
BF16 = jnp.bfloat16

D_MODEL = 1024
DEPTH = 2
CHUNK = 64
EPS = 1e-6
HALF = 0.5
D_FF = 2816
PLE_DIM = 256
SSM_HEAD_DIM = 64
D_INNER = D_MODEL
SSM_HEADS = D_INNER // SSM_HEAD_DIM
N_GROUPS = 4
HEADS_PER_GROUP = SSM_HEADS // N_GROUPS
D_STATE = 128
CONV_W = 4
CONV_CH = D_INNER + 2 * N_GROUPS * D_STATE
DSA_HEAD_DIM = 128
DSA_HEADS = D_MODEL // DSA_HEAD_DIM
KV_HEADS = 2
KV_REP = DSA_HEADS // KV_HEADS
IDX_HEADS = 4
IDX_DIM = 64
TOPK_MAX = 256
BAND_HEAD_DIM = 64
BAND_HEADS = D_MODEL // BAND_HEAD_DIM
LEFT_CHUNKS = 8
BAND = LEFT_CHUNKS * CHUNK
REL_CLIP = 256
N_BRANCH = 3
IN_WIDTHS = (D_INNER, CONV_CH, SSM_HEADS,
             DSA_HEADS * DSA_HEAD_DIM, KV_HEADS * DSA_HEAD_DIM, KV_HEADS * DSA_HEAD_DIM,
             IDX_HEADS * IDX_DIM, IDX_DIM, IDX_HEADS,
             BAND_HEADS * BAND_HEAD_DIM, BAND_HEADS * BAND_HEAD_DIM, BAND_HEADS * BAND_HEAD_DIM)
IN_SPLITS = tuple(int(s) for s in np.cumsum(IN_WIDTHS)[:-1])

LANES = 128
KV_DIM = KV_HEADS * DSA_HEAD_DIM
GROUP_CH = D_INNER // N_GROUPS
NEG_BIG = -1e30
INT_MIN = -2 ** 31
VMEM_LIMIT = 56 * 1024 * 1024


def _mm(a, b):
    return jnp.dot(a, b, preferred_element_type=F32)


def _mm_nt(a, b):
    return lax.dot_general(a, b, (((1,), (1,)), ((), ())), preferred_element_type=F32)


def _mm_tn(a, b):
    return lax.dot_general(a, b, (((0,), (0,)), ((), ())), preferred_element_type=F32)


def _rms(x, g):
    return x * lax.rsqrt(jnp.mean(x * x, axis=-1, keepdims=True) + EPS) * g


def _sigmoid(x):
    return 1.0 / (1.0 + jnp.exp(-x))


def _silu(x):
    return x * _sigmoid(x)


def _resident(shape):
    return pl.BlockSpec(shape, lambda *_: (0,) * len(shape), pipeline_mode=pl.Buffered(1))


def _params(n_grid_dims):
    return pltpu.CompilerParams(dimension_semantics=("arbitrary",) * n_grid_dims,
                                vmem_limit_bytes=VMEM_LIMIT)


def _row_tile(n_rows, want):
    t = min(want, n_rows)
    assert n_rows % t == 0
    return t


FF_CHUNK = 256


def _ffn_body(x_ref, g_ref, wa_ref, wb_ref, w2_ref, o_ref, *, g_pre, g_post):
    x = x_ref[...]
    u = _rms(x, g_ref[g_pre:g_pre + 1, :]).astype(BF16)
    acc = jnp.zeros(x.shape, F32)
    for c in range(D_FF // FF_CHUNK):
        sl = slice(c * FF_CHUNK, (c + 1) * FF_CHUNK)
        a = _mm(u, wa_ref[:, sl])
        b = _mm(u, wb_ref[:, sl])
        acc = acc + _mm((_silu(a) * b).astype(BF16), w2_ref[sl, :])
    o_ref[...] = x + HALF * _rms(acc, g_ref[g_post:g_post + 1, :])


def _ffn(x, g, wa, wb, w2, g_pre, g_post):
    n = x.shape[0]
    tm = _row_tile(n, 512)
    row = lambda i: (i, 0)
    return pl.pallas_call(
        functools.partial(_ffn_body, g_pre=g_pre, g_post=g_post),
        grid=(n // tm,),
        in_specs=[pl.BlockSpec((tm, D_MODEL), row), _resident(g.shape), _resident(wa.shape),
                  _resident(wb.shape), _resident(w2.shape)],
        out_specs=pl.BlockSpec((tm, D_MODEL), row),
        out_shape=jax.ShapeDtypeStruct((n, D_MODEL), F32),
        compiler_params=_params(1),
        name="ffn",
    )(x, g, wa, wb, w2)


_INPROJ_OUT = (
    ("z", D_INNER, D_INNER, F32, None),
    ("xbc", CONV_CH, CONV_CH, F32, None),
    ("qb", D_MODEL, D_MODEL, BF16, DSA_HEAD_DIM ** -0.5),
    ("kb", KV_DIM, KV_DIM, F32, None),
    ("vb", KV_DIM, KV_DIM, F32, None),
    ("qi", IDX_HEADS * IDX_DIM, IDX_HEADS * IDX_DIM, F32, None),
    ("qc", D_MODEL, D_MODEL, BF16, BAND_HEAD_DIM ** -0.5),
    ("kc", D_MODEL, D_MODEL, F32, None),
    ("vc", D_MODEL, D_MODEL, F32, None),
    ("ki", LANES, IDX_DIM, F32, None),
    ("dtwi", LANES, LANES, F32, None),
)
WI_LANE = SSM_HEADS


def _pack_w_in(w_in):
    z, xbc, dt, qb, kb, vb, qi, ki, wi, qc, kc, vc = jnp.split(w_in, IN_SPLITS, axis=-1)
    pad = lambda w: jnp.pad(w, ((0, 0), (0, LANES - w.shape[1])))
    cols = [z, xbc, qb, kb, vb, qi, qc, kc, vc, pad(ki), pad(jnp.concatenate([dt, wi], axis=1))]
    return jnp.concatenate(cols, axis=1).astype(BF16)


def _inproj_body(x_ref, g_ref, w_ref, *out_refs):
    u = _rms(x_ref[...], g_ref[2:3, :]).astype(BF16)
    c0 = 0
    for (_, width, stored, dtype, scale), o_ref in zip(_INPROJ_OUT, out_refs):
        r = _mm(u, w_ref[:, c0:c0 + width])
        if scale is not None:
            r = r * scale
        if stored != width:
            r = r[:, :stored]
        o_ref[...] = r.astype(dtype)
        c0 += width


def _inproj(x, g, w_packed):
    n = x.shape[0]
    tm = _row_tile(n, 256)
    row = lambda i: (i, 0)
    outs = pl.pallas_call(
        _inproj_body,
        grid=(n // tm,),
        in_specs=[pl.BlockSpec((tm, D_MODEL), row), _resident(g.shape), _resident(w_packed.shape)],
        out_specs=[pl.BlockSpec((tm, o[2]), row) for o in _INPROJ_OUT],
        out_shape=[jax.ShapeDtypeStruct((n, o[2]), o[3]) for o in _INPROJ_OUT],
        compiler_params=_params(1),
        name="inproj",
    )(x, g, w_packed)
    return {o[0]: a for o, a in zip(_INPROJ_OUT, outs)}


CONV_PAD = 8


def _split3(x):
    hi = x.astype(BF16)
    r = x - hi.astype(F32)
    mid = r.astype(BF16)
    lo = (r - mid.astype(F32)).astype(BF16)
    return hi, mid, lo


def _expand_heads(x, e):
    hi, mid, lo = _split3(x)
    return _mm(hi, e) + _mm(mid, e) + _mm(lo, e)


def _cumsum_rows(x):
    n = x.shape[0]
    row = lax.broadcasted_iota(jnp.int32, x.shape, 0)
    d = 1
    while d < n:
        x = x + jnp.where(row >= d, pltpu.roll(x, d, 0), 0.0)
        d *= 2
    return x


def _mamba_body(*refs, has_state, n_chunks):
    if has_state:
        (z_ref, xbc_ref, dtwi_ref, conv0_ref, h0_ref, cw_ref, cb_ref, dtb_ref, alog_ref, dskip_ref,
         ng_ref, e_ref, y_ref, hout_ref, xp_scr, ht_scr) = refs
    else:
        (z_ref, xbc_ref, dtwi_ref, cw_ref, cb_ref, dtb_ref, alog_ref, dskip_ref,
         ng_ref, e_ref, y_ref, hout_ref, xp_scr, ht_scr) = refs
    c = pl.program_id(1)
    q = CHUNK

    @pl.when(c == 0)
    def _():
        if has_state:
            xp_scr[0:CONV_PAD, :] = conv0_ref[...]
            ht_scr[...] = h0_ref[...].T
        else:
            xp_scr[0:CONV_PAD, :] = jnp.zeros((CONV_PAD, CONV_CH), F32)
            ht_scr[...] = jnp.zeros(ht_scr.shape, F32)

    @pl.when(c > 0)
    def _():
        xp_scr[0:CONV_PAD, :] = xp_scr[q:q + CONV_PAD, :]

    xp_scr[CONV_PAD:CONV_PAD + q, :] = xbc_ref[...]
    acc = cb_ref[...]
    for k in range(CONV_W):
        r0 = CONV_PAD - (CONV_W - 1) + k
        acc = acc + xp_scr[r0:r0 + q, :] * cw_ref[k:k + 1, :]
    xc = _silu(acc)
    xs = xc[:, :D_INNER]
    bm = xc[:, D_INNER:D_INNER + N_GROUPS * D_STATE]
    cm = xc[:, D_INNER + N_GROUPS * D_STATE:]

    lane = lax.broadcasted_iota(jnp.int32, (q, LANES), 1)
    pre = dtwi_ref[...] + dtb_ref[...]
    dt = jnp.maximum(pre, 0.0) + jnp.log1p(jnp.exp(-jnp.abs(pre)))
    dt = jnp.where(lane < SSM_HEADS, dt, 0.0)
    dta = dt * (-jnp.exp(alog_ref[...]))
    cum = _cumsum_rows(dta)

    e = e_ref[...]
    ecol = _expand_heads(cum, e)
    dtx = _expand_heads(dt, e)
    li = lax.broadcasted_iota(jnp.int32, (q, D_INNER), 0)
    si = lax.broadcasted_iota(jnp.int32, (q, D_INNER), 1) & (q - 1)
    erow = jnp.sum(jnp.where(li == si, ecol, 0.0), axis=0, keepdims=True)
    elast = ecol[q - 1:q, :]
    causal = (li >= si)[:, :GROUP_CH]
    xdt = xs * dtx
    xdec = (xdt * jnp.exp(elast - ecol)).astype(BF16)
    exp_e = jnp.exp(ecol)
    chunk_decay = jnp.exp(elast)
    bdr = lax.broadcasted_iota(jnp.int32, (GROUP_CH, GROUP_CH), 0) // SSM_HEAD_DIM
    bdc = lax.broadcasted_iota(jnp.int32, (GROUP_CH, GROUP_CH), 1) // SSM_HEAD_DIM
    block_diag = bdr == bdc

    ys = []
    for g in range(N_GROUPS):
        sl = slice(g * GROUP_CH, (g + 1) * GROUP_CH)
        nl = slice(g * D_STATE, (g + 1) * D_STATE)
        bg = bm[:, nl].astype(BF16)
        cg = cm[:, nl].astype(BF16)
        cb = _mm_nt(cg, bg)
        cbt = jnp.concatenate([cb] * HEADS_PER_GROUP, axis=1)
        decay = jnp.exp(jnp.where(causal, ecol[:, sl] - erow[:, sl], NEG_BIG))
        m = (cbt * decay).astype(BF16)
        xg = xdt[:, sl]
        bd = jnp.where(block_diag, jnp.concatenate([xg] * HEADS_PER_GROUP, axis=0), 0.0).astype(BF16)
        y_diag = _mm(m, bd)
        ht_g = ht_scr[:, sl]
        y_off = _mm(cg, ht_g.astype(BF16)) * exp_e[:, sl]
        ys.append(y_diag + y_off)
        ht_scr[:, sl] = ht_g * chunk_decay[:, sl] + _mm_tn(bg, xdec[:, sl])
    y = jnp.concatenate(ys, axis=1) + dskip_ref[...] * xs
    y = y * _silu(z_ref[...])
    outs = []
    for g in range(N_GROUPS):
        yg = y[:, g * GROUP_CH:(g + 1) * GROUP_CH]
        outs.append(yg * lax.rsqrt(jnp.mean(yg * yg, axis=-1, keepdims=True) + EPS))
    y_ref[...] = (jnp.concatenate(outs, axis=1) * ng_ref[...]).astype(y_ref.dtype)

    @pl.when(c == n_chunks - 1)
    def _():
        hout_ref[...] = ht_scr[...].T


def _head_expand_matrix():
    e = np.zeros((LANES, D_INNER), np.float32)
    for h in range(SSM_HEADS):
        e[h, h * SSM_HEAD_DIM:(h + 1) * SSM_HEAD_DIM] = 1.0
    return jnp.asarray(e, BF16)


def _mamba(z, xbc, dtwi, conv0, h0, lw, batch, t):
    nc = t // CHUNK
    has_state = h0 is not None
    row = lambda b, c: (b * nc + c, 0)
    per_b = lambda b, c: (b, 0, 0)
    pad16 = lambda v: jnp.pad(v.reshape(1, SSM_HEADS), ((0, 0), (0, LANES - SSM_HEADS)))
    small = [lw["conv_w"], lw["conv_b"].reshape(1, CONV_CH), pad16(lw["dt_bias"]), pad16(lw["a_log"]),
             jnp.repeat(lw["d_skip"], SSM_HEAD_DIM).reshape(1, D_INNER),
             lw["ssm_norm_g"].reshape(1, D_INNER), _head_expand_matrix()]
    ins = [z, xbc, dtwi]
    in_specs = [pl.BlockSpec((CHUNK, D_INNER), row), pl.BlockSpec((CHUNK, CONV_CH), row),
                pl.BlockSpec((CHUNK, LANES), row)]
    if has_state:
        ins += [conv0, h0]
        in_specs += [pl.BlockSpec((None, CONV_PAD, CONV_CH), per_b),
                     pl.BlockSpec((None, D_INNER, D_STATE), per_b)]
    ins += small
    in_specs += [_resident(a.shape) for a in small]
    y, h_out = pl.pallas_call(
        functools.partial(_mamba_body, has_state=has_state, n_chunks=nc),
        grid=(batch, nc),
        in_specs=in_specs,
        out_specs=[pl.BlockSpec((CHUNK, D_INNER), row), pl.BlockSpec((None, D_INNER, D_STATE), per_b)],
        out_shape=[jax.ShapeDtypeStruct((batch * t, D_INNER), BF16),
                   jax.ShapeDtypeStruct((batch, D_INNER, D_STATE), F32)],
        scratch_shapes=[pltpu.VMEM((CONV_PAD + CHUNK, CONV_CH), F32), pltpu.VMEM((D_STATE, D_INNER), F32)],
        compiler_params=_params(2),
        name="mamba",
    )(*ins)
    return y, h_out


IDX_CAT = 4 * IDX_DIM


def _hi_lo(x):
    hi = x.astype(BF16).astype(F32)
    return hi, (x - hi).astype(BF16).astype(F32)


def _dsa_body(qb_ref, qi_ref, dtwi_ref, ki_ref, k_ref, v_ref, o_ref, kcat_scr, kbf_scr, vbf_scr, key_scr,
              *, tq, s_keys, n_valid, q_pos0, topk):
    j = pl.program_id(1)

    @pl.when(j == 0)
    def _():
        hi, lo = _hi_lo(ki_ref[...])
        kcat_scr[...] = jnp.concatenate([hi, lo, hi, jnp.zeros_like(hi)], axis=1).astype(BF16)
        kbf_scr[...] = k_ref[...].astype(BF16)
        vbf_scr[...] = v_ref[...].astype(BF16)

    qi = qi_ref[...]
    wi = dtwi_ref[...] * (IDX_DIM ** -0.5 * IDX_HEADS ** -0.5)
    kcat = kcat_scr[...]
    score = jnp.zeros((tq, s_keys), F32)
    for h in range(IDX_HEADS):
        hi, lo = _hi_lo(qi[:, h * IDX_DIM:(h + 1) * IDX_DIM])
        qcat = jnp.concatenate([hi, hi, lo, jnp.zeros_like(hi)], axis=1).astype(BF16)
        logit = _mm_nt(qcat, kcat)
        score = score + jnp.maximum(logit, 0.0) * wi[:, WI_LANE + h:WI_LANE + h + 1]

    qpos = q_pos0 + j * tq + lax.broadcasted_iota(jnp.int32, (tq, 1), 0)
    kpos = lax.broadcasted_iota(jnp.int32, (1, s_keys), 1)
    q_chunk_end = ((qpos >> 6) + 1) << 6
    allowed = kpos < jnp.minimum(q_chunk_end, n_valid)
    score = jnp.where(allowed, score, -jnp.inf)
    k_eff = jnp.minimum(jnp.minimum(q_chunk_end, n_valid), topk).astype(F32)

    bits = lax.bitcast_convert_type(score, jnp.int32)
    key_scr[...] = jnp.where(bits < 0, bits ^ jnp.int32(0x7FFFFFFF), bits)

    def count_ge(thr):
        return jnp.sum(jnp.where(key_scr[...] >= thr, 1.0, 0.0), axis=1, keepdims=True)

    def radix_step(i, tu):
        cand = tu | lax.shift_left(jnp.int32(1), 31 - i)
        return jnp.where(count_ge(cand ^ jnp.int32(INT_MIN)) >= k_eff, cand, tu)

    tu = lax.fori_loop(0, 32, radix_step, jnp.zeros((tq, 1), jnp.int32))
    thr = tu ^ jnp.int32(INT_MIN)

    key = key_scr[...]
    n_gt = jnp.sum(jnp.where(key > thr, 1.0, 0.0), axis=1, keepdims=True)
    need = k_eff - n_gt
    ur = lax.broadcasted_iota(jnp.int32, (LANES, LANES), 0)
    uc = lax.broadcasted_iota(jnp.int32, (LANES, LANES), 1)
    strict_upper = jnp.where(ur < uc, 1.0, 0.0).astype(BF16)
    carry = jnp.zeros((tq, 1), F32)
    mask_cols = []
    for cb in range(s_keys // LANES):
        kb = key[:, cb * LANES:(cb + 1) * LANES]
        eq = jnp.where(kb == thr, 1.0, 0.0)
        rank = _mm(eq.astype(BF16), strict_upper) + carry
        take = jnp.where(kb > thr, 1.0, jnp.where(rank < need, eq, 0.0))
        mask_cols.append(jnp.where(take > 0.0, 0.0, NEG_BIG))
        carry = carry + jnp.sum(eq, axis=1, keepdims=True)
    mask_b = jnp.concatenate(mask_cols, axis=1)

    qb = qb_ref[...]
    outs = []
    for g in range(KV_HEADS):
        q4 = jnp.concatenate([qb[:, (g * KV_REP + r) * DSA_HEAD_DIM:(g * KV_REP + r + 1) * DSA_HEAD_DIM]
                              for r in range(KV_REP)], axis=0)
        kg = kbf_scr[:, g * DSA_HEAD_DIM:(g + 1) * DSA_HEAD_DIM]
        vg = vbf_scr[:, g * DSA_HEAD_DIM:(g + 1) * DSA_HEAD_DIM]
        s = _mm_nt(q4, kg) + jnp.concatenate([mask_b] * KV_REP, axis=0)
        p = jnp.exp(s - jnp.max(s, axis=1, keepdims=True))
        o = _mm(p.astype(BF16), vg) / jnp.sum(p, axis=1, keepdims=True)
        outs += [o[r * tq:(r + 1) * tq, :] for r in range(KV_REP)]
    o_ref[...] = jnp.concatenate(outs, axis=1).astype(o_ref.dtype)


def _dsa(qb, qi, dtwi, ki_all, k_all, v_all, batch, t, tq, s_keys, n_valid, q_pos0):
    nq = t // tq
    topk = min(TOPK_MAX, n_valid // 4)
    qrow = lambda b, j: (b * nq + j, 0)
    krow = lambda b, j: (b, 0)
    return pl.pallas_call(
        functools.partial(_dsa_body, tq=tq, s_keys=s_keys, n_valid=n_valid, q_pos0=q_pos0, topk=topk),
        grid=(batch, nq),
        in_specs=[pl.BlockSpec((tq, D_MODEL), qrow), pl.BlockSpec((tq, IDX_HEADS * IDX_DIM), qrow),
                  pl.BlockSpec((tq, LANES), qrow), pl.BlockSpec((s_keys, IDX_DIM), krow),
                  pl.BlockSpec((s_keys, KV_DIM), krow), pl.BlockSpec((s_keys, KV_DIM), krow)],
        out_specs=pl.BlockSpec((tq, D_MODEL), qrow),
        out_shape=jax.ShapeDtypeStruct((batch * t, D_MODEL), BF16),
        scratch_shapes=[pltpu.VMEM((s_keys, IDX_CAT), BF16), pltpu.VMEM((s_keys, KV_DIM), BF16),
                        pltpu.VMEM((s_keys, KV_DIM), BF16), pltpu.VMEM((tq, s_keys), jnp.int32)],
        compiler_params=_params(2),
        name="dsa",
    )(qb, qi, dtwi, ki_all, k_all, v_all)


def _band_body(*refs, tq, n_kblk, q_pos0, k_min):
    q_ref = refs[0]
    k_refs = refs[1:1 + n_kblk]
    v_refs = refs[1 + n_kblk:1 + 2 * n_kblk]
    vec_ref, o_ref, bias_scr = refs[1 + 2 * n_kblk:]
    w = sum(r.shape[0] for r in k_refs)
    first = (pl.program_id(0) == 0) & (pl.program_id(1) == 0)

    @pl.when(first)
    def _():
        for h in range(BAND_HEADS):
            rows = jnp.broadcast_to(vec_ref[h:h + 1, :], (tq, vec_ref.shape[1]))
            bias_scr[h] = pltpu.roll(rows, 0, 1, stride=1, stride_axis=0)[:, :w]

    qpos = q_pos0 + pl.program_id(1) * tq + lax.broadcasted_iota(jnp.int32, (tq, 1), 0)
    kpos = q_pos0 + pl.program_id(1) * tq + (tq - w) + lax.broadcasted_iota(jnp.int32, (1, w), 1)
    dchunk = (qpos >> 6) - (kpos >> 6)
    mask_b = jnp.where((kpos >= k_min) & (dchunk >= 0) & (dchunk <= LEFT_CHUNKS), 0.0, NEG_BIG)

    outs = []
    for h in range(BAND_HEADS):
        hs = slice(h * BAND_HEAD_DIM, (h + 1) * BAND_HEAD_DIM)
        qh = q_ref[:, hs]
        kh = jnp.concatenate([r[:, hs] for r in k_refs], axis=0).astype(BF16)
        vh = jnp.concatenate([r[:, hs] for r in v_refs], axis=0).astype(BF16)
        s = _mm_nt(qh, kh) + bias_scr[h] + mask_b
        p = jnp.exp(s - jnp.max(s, axis=1, keepdims=True))
        outs.append(_mm(p.astype(BF16), vh) / jnp.sum(p, axis=1, keepdims=True))
    o_ref[...] = jnp.concatenate(outs, axis=1).astype(o_ref.dtype)


def _band_bias_vec(rel_bias, tq, w):
    l = -(-(w + tq) // LANES) * LANES
    m = np.arange(l)
    d = np.where(m < w, m, m - l)
    rel = np.clip(w - tq - d, -REL_CLIP, REL_CLIP) + REL_CLIP
    return rel_bias[jnp.asarray(rel)].T


def _band(q, k, v, rel_bias, batch, t, tq, n_kblk, k_rows, q_pos0, k_min):
    nq = t // tq
    w = n_kblk * tq
    kb_per_batch = k_rows // tq
    own_history = k_rows != t
    qrow = lambda b, i: (b * nq + i, 0)

    def krow(off):
        if own_history:
            return lambda b, i: (b * kb_per_batch + i + off, 0)
        return lambda b, i: (b * kb_per_batch + jnp.maximum(i - (n_kblk - 1) + off, 0), 0)

    vec = _band_bias_vec(rel_bias, tq, w)
    kspecs = [pl.BlockSpec((tq, D_MODEL), krow(o)) for o in range(n_kblk)]
    return pl.pallas_call(
        functools.partial(_band_body, tq=tq, n_kblk=n_kblk, q_pos0=q_pos0, k_min=k_min),
        grid=(batch, nq),
        in_specs=[pl.BlockSpec((tq, D_MODEL), qrow)] + kspecs + kspecs + [_resident(vec.shape)],
        out_specs=pl.BlockSpec((tq, D_MODEL), qrow),
        out_shape=jax.ShapeDtypeStruct((batch * t, D_MODEL), BF16),
        scratch_shapes=[pltpu.VMEM((BAND_HEADS, tq, w), F32)],
        compiler_params=_params(2),
        name="band",
    )(q, *([k] * n_kblk), *([v] * n_kblk), vec)


def _merge_body(x_ref, ya_ref, yb_ref, yc_ref, g_ref, wg_ref, bg_ref, wbr_ref, wo_ref, o_ref):
    x = x_ref[...]
    u = _rms(x, g_ref[2:3, :]).astype(BF16)
    mix = jnp.zeros(x.shape, F32)
    for k, y_ref in enumerate((ya_ref, yb_ref, yc_ref)):
        sl = slice(k * D_MODEL, (k + 1) * D_MODEL)
        gate = _sigmoid(_mm(u, wg_ref[:, sl]) + bg_ref[:, sl])
        mix = mix + gate * _mm(y_ref[...], wbr_ref[k])
    o_ref[...] = x + _rms(_mm(mix.astype(BF16), wo_ref[...]), g_ref[3:4, :])


def _merge(x, ya, yb, yc, g, wg, bg, wbr, wo):
    n = x.shape[0]
    tm = _row_tile(n, 256)
    row = lambda i: (i, 0)
    tile = pl.BlockSpec((tm, D_MODEL), row)
    return pl.pallas_call(
        _merge_body,
        grid=(n // tm,),
        in_specs=[tile, tile, tile, tile] + [_resident(a.shape) for a in (g, wg, bg, wbr, wo)],
        out_specs=tile,
        out_shape=jax.ShapeDtypeStruct((n, D_MODEL), F32),
        compiler_params=_params(1),
        name="merge",
    )(x, ya, yb, yc, g, wg, bg, wbr, wo)


def _ple_body(x_ref, p_ref, g_ref, wp_ref, wpg_ref, o_ref):
    x = x_ref[...]
    e = _mm(p_ref[...].astype(BF16), wp_ref[...])
    pg = _sigmoid(_mm(_rms(x, g_ref[6:7, :]).astype(BF16), wpg_ref[...]))
    o_ref[...] = x + _rms(pg * e, g_ref[7:8, :])


def _ple(x, p, g, wp, wpg):
    n = x.shape[0]
    tm = _row_tile(n, 512)
    row = lambda i: (i, 0)
    return pl.pallas_call(
        _ple_body,
        grid=(n // tm,),
        in_specs=[pl.BlockSpec((tm, D_MODEL), row), pl.BlockSpec((tm, PLE_DIM), row)]
        + [_resident(a.shape) for a in (g, wp, wpg)],
        out_specs=pl.BlockSpec((tm, D_MODEL), row),
        out_shape=jax.ShapeDtypeStruct((n, D_MODEL), F32),
        compiler_params=_params(1),
        name="ple",
    )(x, p, g, wp, wpg)


DSA_TQ = 128
BAND_TQ = 256
BAND_KBLK = 1 + -(-BAND // BAND_TQ)


def _prep_layer(w):
    bf = lambda a: a.astype(BF16)
    return {
        "g": w["norm_g"],
        "ffn": [(bf(w["ffn_w13"][j][:, :D_FF]), bf(w["ffn_w13"][j][:, D_FF:]), bf(w["ffn_w2"][j])) for j in range(2)],
        "w_in": _pack_w_in(w["w_in"]),
        "wg": bf(w["w_gate"]), "bg": w["b_gate"].reshape(1, N_BRANCH * D_MODEL),
        "wbr": bf(w["w_branch"]), "wo": bf(w["w_out"]),
        "wp": bf(w["w_ple"]), "wpg": bf(w["w_ple_gate"]),
    }


def _trunk_layer(x, p, w, pw, cache, batch, t):
    g = pw["g"]
    x = _ffn(x, g, *pw["ffn"][0], 0, 1)
    pr = _inproj(x, g, pw["w_in"])
    if cache is None:
        ya, h_new = _mamba(pr["z"], pr["xbc"], pr["dtwi"], None, None, w, batch, t)
        yb = _dsa(pr["qb"], pr["qi"], pr["dtwi"], pr["ki"], pr["kb"], pr["vb"],
                  batch, t, DSA_TQ, t, t, 0)
        yc = _band(pr["qc"], pr["kc"], pr["vc"], w["rel_bias"], batch, t, BAND_TQ, BAND_KBLK, t, 0, 0)
        conv_src = pr["xbc"].reshape(batch, t, CONV_CH)
    else:
        past = cache["dsa_k"].shape[1]
        conv0 = jnp.pad(cache["conv"], ((0, 0), (CONV_PAD - (CONV_W - 1), 0), (0, 0)))
        h0 = cache["ssm"].reshape(batch, D_INNER, D_STATE)
        ya, h_new = _mamba(pr["z"], pr["xbc"], pr["dtwi"], conv0, h0, w, batch, t)

        n_valid = past + t
        s_keys = -(-n_valid // LANES) * LANES

        def with_cache(c, new):
            width = new.shape[-1]
            a = jnp.concatenate([c.reshape(batch, past, width), new.reshape(batch, t, width)], axis=1)
            return jnp.pad(a, ((0, 0), (0, s_keys - n_valid), (0, 0))).reshape(batch * s_keys, width)

        yb = _dsa(pr["qb"], pr["qi"], pr["dtwi"], with_cache(cache["idx_k"], pr["ki"]),
                  with_cache(cache["dsa_k"], pr["kb"]), with_cache(cache["dsa_v"], pr["vb"]),
                  batch, t, t, s_keys, n_valid, past)

        nrows = cache["band_k"].shape[1]
        n_kblk = 1 + -(-nrows // t)
        lead = (n_kblk - 1) * t - nrows

        def with_band(c, new):
            a = jnp.concatenate([c.reshape(batch, nrows, D_MODEL), new.reshape(batch, t, D_MODEL)], axis=1)
            return jnp.pad(a, ((0, 0), (lead, 0), (0, 0))).reshape(batch * n_kblk * t, D_MODEL)

        yc = _band(pr["qc"], with_band(cache["band_k"], pr["kc"]), with_band(cache["band_v"], pr["vc"]),
                   w["rel_bias"], batch, t, t, n_kblk, n_kblk * t, past, past - nrows)
        conv_src = jnp.concatenate([cache["conv"], pr["xbc"].reshape(batch, t, CONV_CH)], axis=1)

    x = _merge(x, ya, yb, yc, g, pw["wg"], pw["bg"], pw["wbr"], pw["wo"])
    x = _ffn(x, g, *pw["ffn"][1], 4, 5)
    x = _ple(x, p.reshape(batch * t, PLE_DIM), g, pw["wp"], pw["wpg"])

    band_rows = min(BAND, t)
    kc3 = pr["kc"].reshape(batch, t, BAND_HEADS, BAND_HEAD_DIM)
    vc3 = pr["vc"].reshape(batch, t, BAND_HEADS, BAND_HEAD_DIM)
    state = (pr["kb"].reshape(batch, t, KV_HEADS, DSA_HEAD_DIM), pr["vb"].reshape(batch, t, KV_HEADS, DSA_HEAD_DIM),
             pr["ki"].reshape(batch, t, IDX_DIM), kc3[:, t - band_rows:], vc3[:, t - band_rows:],
             h_new.reshape(batch, SSM_HEADS, SSM_HEAD_DIM, D_STATE), conv_src[:, -(CONV_W - 1):])
    return x, state


def kernel(x_prompt, x_sample, p_prompt, p_sample, cache_dsa_k, cache_dsa_v, cache_idx_k, cache_band_k,
           cache_band_v, state_ssm, state_conv, norm_g, ffn_w13, ffn_w2, w_in, conv_w, conv_b, dt_bias,
           a_log, d_skip, ssm_norm_g, rel_bias, w_gate, b_gate, w_branch, w_out, w_ple, w_ple_gate):
    bp, tp, _ = x_prompt.shape
    bs, ts, _ = x_sample.shape
    yp = x_prompt.reshape(bp * tp, D_MODEL)
    ys = x_sample.reshape(bs * ts, D_MODEL)
    st_p, st_s = [], []
    for i in range(DEPTH):
        w = {"norm_g": norm_g[i], "ffn_w13": ffn_w13[i], "ffn_w2": ffn_w2[i], "w_in": w_in[i],
             "conv_w": conv_w[i], "conv_b": conv_b[i], "dt_bias": dt_bias[i], "a_log": a_log[i],
             "d_skip": d_skip[i], "ssm_norm_g": ssm_norm_g[i], "rel_bias": rel_bias[i],
             "w_gate": w_gate[i], "b_gate": b_gate[i], "w_branch": w_branch[i], "w_out": w_out[i],
             "w_ple": w_ple[i], "w_ple_gate": w_ple_gate[i]}
        pw = _prep_layer(w)
        yp, sp = _trunk_layer(yp, p_prompt[i], w, pw, None, bp, tp)
        st_p.append(sp)
        cache = {"dsa_k": cache_dsa_k[i], "dsa_v": cache_dsa_v[i], "idx_k": cache_idx_k[i],
                 "band_k": cache_band_k[i], "band_v": cache_band_v[i], "ssm": state_ssm[i], "conv": state_conv[i]}
        ys, ss = _trunk_layer(ys, p_sample[i], w, pw, cache, bs, ts)
        st_s.append(ss)
    outs_p = [jnp.stack(c) for c in zip(*st_p)]
    outs_s = [jnp.stack(c) for c in zip(*st_s)]
    return (yp.reshape(bp, tp, D_MODEL), ys.reshape(bs, ts, D_MODEL), *outs_p, *outs_s)
```

```python
import functools
import math

import numpy as np
import jax
import jax.numpy as jnp
from jax import lax
from jax.experimental import pallas as pl
from jax.experimental.pallas import tpu as pltpu

F32 = jnp.float32
BF16 = jnp.bfloat16

D_MODEL = 1024
DEPTH = 2
CHUNK = 64
EPS = 1e-6
HALF = 0.5
D_FF = 2816
PLE_DIM = 256
SSM_HEAD_DIM = 64
D_INNER = D_MODEL
SSM_HEADS = D_INNER // SSM_HEAD_DIM
N_GROUPS = 4
HEADS_PER_GROUP = SSM_HEADS // N_GROUPS
D_STATE = 128
CONV_W = 4
CONV_CH = D_INNER + 2 * N_GROUPS * D_STATE
DSA_HEAD_DIM = 128
DSA_HEADS = D_MODEL // DSA_HEAD_DIM
KV_HEADS = 2
KV_REP = DSA_HEADS // KV_HEADS
IDX_HEADS = 4
IDX_DIM = 64
TOPK_MAX = 256
BAND_HEAD_DIM = 64
BAND_HEADS = D_MODEL // BAND_HEAD_DIM
LEFT_CHUNKS = 8
BAND = LEFT_CHUNKS * CHUNK
REL_CLIP = 256
N_BRANCH = 3
IN_WIDTHS = (D_INNER, CONV_CH, SSM_HEADS,
             DSA_HEADS * DSA_HEAD_DIM, KV_HEADS * DSA_HEAD_DIM, KV_HEADS * DSA_HEAD_DIM,
             IDX_HEADS * IDX_DIM, IDX_DIM, IDX_HEADS,
             BAND_HEADS * BAND_HEAD_DIM, BAND_HEADS * BAND_HEAD_DIM, BAND_HEADS * BAND_HEAD_DIM)
IN_SPLITS = tuple(int(s) for s in np.cumsum(IN_WIDTHS)[:-1])

LANES = 128
SUBLANES = 8
KV_DIM = KV_HEADS * DSA_HEAD_DIM
GROUP_CH = D_INNER // N_GROUPS
NEG_BIG = -1e30
INT_MIN = -2 ** 31
LOG2E = math.log2(math.e)
VMEM_LIMIT = 56 * 1024 * 1024


def _mm(a, b):
    return jnp.dot(a, b, preferred_element_type=F32)


def _mm_nt(a, b):
    return lax.dot_general(a, b, (((1,), (1,)), ((), ())), preferred_element_type=F32)


def _mm_tn(a, b):
    return lax.dot_general(a, b, (((0,), (0,)), ((), ())), preferred_element_type=F32)


def _rms(x, g):
    return x * lax.rsqrt(jnp.mean(x * x, axis=-1, keepdims=True) + EPS) * g


def _sigmoid(x):
    return 1.0 / (1.0 + jnp.exp(-x))


def _silu(x):
    return x * _sigmoid(x)


def _resident(shape):
    return pl.BlockSpec(shape, lambda *_: (0,) * len(shape), pipeline_mode=pl.Buffered(1))


def _params(n_grid_dims):
    return pltpu.CompilerParams(dimension_semantics=("arbitrary",) * n_grid_dims,
                                vmem_limit_bytes=VMEM_LIMIT)


def _row_tile(n_rows, want):
    t = min(want, n_rows)
    assert n_rows % t == 0
    return t


def _fold_rows(x, op):
    parts = [x[i * SUBLANES:(i + 1) * SUBLANES] for i in range(x.shape[0] // SUBLANES)]
    while len(parts) > 1:
        parts = [op(parts[i], parts[i + 1]) for i in range(0, len(parts) - 1, 2)] + parts[len(parts) & ~1:]
    return parts[0]


FF_CHUNK = 256


def _ffn_body(x_ref, g_ref, wa_ref, wb_ref, w2_ref, o_ref, *, g_pre, g_post):
    x = x_ref[...]
    u = _rms(x, g_ref[g_pre:g_pre + 1, :]).astype(BF16)
    acc = jnp.zeros(x.shape, F32)
    for c in range(D_FF // FF_CHUNK):
        sl = slice(c * FF_CHUNK, (c + 1) * FF_CHUNK)
        a = _mm(u, wa_ref[:, sl])
        b = _mm(u, wb_ref[:, sl])
        acc = acc + _mm((_silu(a) * b).astype(BF16), w2_ref[sl, :])
    o_ref[...] = x + HALF * _rms(acc, g_ref[g_post:g_post + 1, :])


def _ffn(x, g, wa, wb, w2, g_pre, g_post):
    n = x.shape[0]
    tm = _row_tile(n, 512)
    row = lambda i: (i, 0)
    return pl.pallas_call(
        functools.partial(_ffn_body, g_pre=g_pre, g_post=g_post),
        grid=(n // tm,),
        in_specs=[pl.BlockSpec((tm, D_MODEL), row), _resident(g.shape), _resident(wa.shape),
                  _resident(wb.shape), _resident(w2.shape)],
        out_specs=pl.BlockSpec((tm, D_MODEL), row),
        out_shape=jax.ShapeDtypeStruct((n, D_MODEL), F32),
        compiler_params=_params(1),
        name="ffn",
    )(x, g, wa, wb, w2)


_INPROJ_OUT = (
    ("z", D_INNER, D_INNER, F32, None),
    ("xbc", CONV_CH, CONV_CH, F32, None),
    ("qb", D_MODEL, D_MODEL, BF16, DSA_HEAD_DIM ** -0.5 * LOG2E),
    ("kb", KV_DIM, KV_DIM, F32, None),
    ("vb", KV_DIM, KV_DIM, F32, None),
    ("qi", IDX_HEADS * IDX_DIM, IDX_HEADS * IDX_DIM, F32, None),
    ("qc", D_MODEL, D_MODEL, BF16, BAND_HEAD_DIM ** -0.5 * LOG2E),
    ("kc", D_MODEL, D_MODEL, F32, None),
    ("vc", D_MODEL, D_MODEL, F32, None),
    ("ki", LANES, IDX_DIM, F32, None),
    ("dtwi", LANES, LANES, F32, None),
)
WI_LANE = SSM_HEADS


def _pack_w_in(w_in):
    z, xbc, dt, qb, kb, vb, qi, ki, wi, qc, kc, vc = jnp.split(w_in, IN_SPLITS, axis=-1)
    pad = lambda w: jnp.pad(w, ((0, 0), (0, LANES - w.shape[1])))
    cols = [z, xbc, qb, kb, vb, qi, qc, kc, vc, pad(ki), pad(jnp.concatenate([dt, wi], axis=1))]
    return jnp.concatenate(cols, axis=1).astype(BF16)


def _inproj_body(x_ref, g_ref, w_ref, *out_refs):
    u = _rms(x_ref[...], g_ref[2:3, :]).astype(BF16)
    c0 = 0
    for (_, width, stored, dtype, scale), o_ref in zip(_INPROJ_OUT, out_refs):
        r = _mm(u, w_ref[:, c0:c0 + width])
        if scale is not None:
            r = r * scale
        if stored != width:
            r = r[:, :stored]
        o_ref[...] = r.astype(dtype)
        c0 += width


def _inproj(x, g, w_packed):
    n = x.shape[0]
    tm = _row_tile(n, 256)
    row = lambda i: (i, 0)
    outs = pl.pallas_call(
        _inproj_body,
        grid=(n // tm,),
        in_specs=[pl.BlockSpec((tm, D_MODEL), row), _resident(g.shape), _resident(w_packed.shape)],
        out_specs=[pl.BlockSpec((tm, o[2]), row) for o in _INPROJ_OUT],
        out_shape=[jax.ShapeDtypeStruct((n, o[2]), o[3]) for o in _INPROJ_OUT],
        compiler_params=_params(1),
        name="inproj",
    )(x, g, w_packed)
    return {o[0]: a for o, a in zip(_INPROJ_OUT, outs)}


CONV_PAD = 8


def _split3(x):
    hi = x.astype(BF16)
    r = x - hi.astype(F32)
    mid = r.astype(BF16)
    lo = (r - mid.astype(F32)).astype(BF16)
    return hi, mid, lo


def _expand_heads(x, e):
    hi, mid, lo = _split3(x)
    return _mm(hi, e) + _mm(mid, e) + _mm(lo, e)


def _cumsum_rows(x):
    n = x.shape[0]
    row = lax.broadcasted_iota(jnp.int32, x.shape, 0)
    d = 1
    while d < n:
        x = x + jnp.where(row >= d, pltpu.roll(x, d, 0), 0.0)
        d *= 2
    return x


def _mamba_body(*refs, has_state, n_chunks):
    if has_state:
        (z_ref, xbc_ref, dtwi_ref, conv0_ref, h0_ref, cw_ref, cb_ref, dtb_ref, alog_ref, dskip_ref,
         ng_ref, e_ref, y_ref, hout_ref, xp_scr, ht_scr) = refs
    else:
        (z_ref, xbc_ref, dtwi_ref, cw_ref, cb_ref, dtb_ref, alog_ref, dskip_ref,
         ng_ref, e_ref, y_ref, hout_ref, xp_scr, ht_scr) = refs
    c = pl.program_id(1)
    q = CHUNK

    @pl.when(c == 0)
    def _():
        if has_state:
            xp_scr[0:CONV_PAD, :] = conv0_ref[...]
            ht_scr[...] = h0_ref[...].T
        else:
            xp_scr[0:CONV_PAD, :] = jnp.zeros((CONV_PAD, CONV_CH), F32)
            ht_scr[...] = jnp.zeros(ht_scr.shape, F32)

    @pl.when(c > 0)
    def _():
        xp_scr[0:CONV_PAD, :] = xp_scr[q:q + CONV_PAD, :]

    xp_scr[CONV_PAD:CONV_PAD + q, :] = xbc_ref[...]
    acc = cb_ref[...]
    for k in range(CONV_W):
        r0 = CONV_PAD - (CONV_W - 1) + k
        acc = acc + xp_scr[r0:r0 + q, :] * cw_ref[k:k + 1, :]
    xc = _silu(acc)
    xs = xc[:, :D_INNER]
    bm = xc[:, D_INNER:D_INNER + N_GROUPS * D_STATE]
    cm = xc[:, D_INNER + N_GROUPS * D_STATE:]

    lane = lax.broadcasted_iota(jnp.int32, (q, LANES), 1)
    pre = dtwi_ref[...] + dtb_ref[...]
    dt = jnp.maximum(pre, 0.0) + jnp.log1p(jnp.exp(-jnp.abs(pre)))
    dt = jnp.where(lane < SSM_HEADS, dt, 0.0)
    dta = dt * (-jnp.exp(alog_ref[...]))
    cum = _cumsum_rows(dta)

    e = e_ref[...]
    ecol = _expand_heads(cum, e)
    dtx = _expand_heads(dt, e)
    li = lax.broadcasted_iota(jnp.int32, (q, D_INNER), 0)
    si = lax.broadcasted_iota(jnp.int32, (q, D_INNER), 1) & (q - 1)
    erow = jnp.sum(jnp.where(li == si, ecol, 0.0), axis=0, keepdims=True)
    elast = ecol[q - 1:q, :]
    causal = (li >= si)[:, :GROUP_CH]
    xdt = xs * dtx
    xdec = (xdt * jnp.exp(elast - ecol)).astype(BF16)
    exp_e = jnp.exp(ecol)
    chunk_decay = jnp.exp(elast)
    bdr = lax.broadcasted_iota(jnp.int32, (GROUP_CH, GROUP_CH), 0) // SSM_HEAD_DIM
    bdc = lax.broadcasted_iota(jnp.int32, (GROUP_CH, GROUP_CH), 1) // SSM_HEAD_DIM
    block_diag = bdr == bdc

    ys = []
    for g in range(N_GROUPS):
        sl = slice(g * GROUP_CH, (g + 1) * GROUP_CH)
        nl = slice(g * D_STATE, (g + 1) * D_STATE)
        bg = bm[:, nl].astype(BF16)
        cg = cm[:, nl].astype(BF16)
        cb = _mm_nt(cg, bg)
        cbt = jnp.concatenate([cb] * HEADS_PER_GROUP, axis=1)
        decay = jnp.exp(jnp.where(causal, ecol[:, sl] - erow[:, sl], NEG_BIG))
        m = (cbt * decay).astype(BF16)
        xg = xdt[:, sl]
        bd = jnp.where(block_diag, jnp.concatenate([xg] * HEADS_PER_GROUP, axis=0), 0.0).astype(BF16)
        y_diag = _mm(m, bd)
        ht_g = ht_scr[:, sl]
        y_off = _mm(cg, ht_g.astype(BF16)) * exp_e[:, sl]
        ys.append(y_diag + y_off)
        ht_scr[:, sl] = ht_g * chunk_decay[:, sl] + _mm_tn(bg, xdec[:, sl])
    y = jnp.concatenate(ys, axis=1) + dskip_ref[...] * xs
    y = y * _silu(z_ref[...])
    outs = []
    for g in range(N_GROUPS):
        yg = y[:, g * GROUP_CH:(g + 1) * GROUP_CH]
        outs.append(yg * lax.rsqrt(jnp.mean(yg * yg, axis=-1, keepdims=True) + EPS))
    y_ref[...] = (jnp.concatenate(outs, axis=1) * ng_ref[...]).astype(y_ref.dtype)

    @pl.when(c == n_chunks - 1)
    def _():
        hout_ref[...] = ht_scr[...].T


def _head_expand_matrix():
    e = np.zeros((LANES, D_INNER), np.float32)
    for h in range(SSM_HEADS):
        e[h, h * SSM_HEAD_DIM:(h + 1) * SSM_HEAD_DIM] = 1.0
    return jnp.asarray(e, BF16)


def _mamba(z, xbc, dtwi, conv0, h0, lw, batch, t):
    nc = t // CHUNK
    has_state = h0 is not None
    row = lambda b, c: (b * nc + c, 0)
    per_b = lambda b, c: (b, 0, 0)
    pad16 = lambda v: jnp.pad(v.reshape(1, SSM_HEADS), ((0, 0), (0, LANES - SSM_HEADS)))
    small = [lw["conv_w"], lw["conv_b"].reshape(1, CONV_CH), pad16(lw["dt_bias"]), pad16(lw["a_log"]),
             jnp.repeat(lw["d_skip"], SSM_HEAD_DIM).reshape(1, D_INNER),
             lw["ssm_norm_g"].reshape(1, D_INNER), _head_expand_matrix()]
    ins = [z, xbc, dtwi]
    in_specs = [pl.BlockSpec((CHUNK, D_INNER), row), pl.BlockSpec((CHUNK, CONV_CH), row),
                pl.BlockSpec((CHUNK, LANES), row)]
    if has_state:
        ins += [conv0, h0]
        in_specs += [pl.BlockSpec((None, CONV_PAD, CONV_CH), per_b),
                     pl.BlockSpec((None, D_INNER, D_STATE), per_b)]
    ins += small
    in_specs += [_resident(a.shape) for a in small]
    y, h_out = pl.pallas_call(
        functools.partial(_mamba_body, has_state=has_state, n_chunks=nc),
        grid=(batch, nc),
        in_specs=in_specs,
        out_specs=[pl.BlockSpec((CHUNK, D_INNER), row), pl.BlockSpec((None, D_INNER, D_STATE), per_b)],
        out_shape=[jax.ShapeDtypeStruct((batch * t, D_INNER), BF16),
                   jax.ShapeDtypeStruct((batch, D_INNER, D_STATE), F32)],
        scratch_shapes=[pltpu.VMEM((CONV_PAD + CHUNK, CONV_CH), F32), pltpu.VMEM((D_STATE, D_INNER), F32)],
        compiler_params=_params(2),
        name="mamba",
    )(*ins)
    return y, h_out


DSA_TQ = LANES
DSA_KB = 512
IDX_CAT = 4 * IDX_DIM
DSA_OT_ROWS = 2 * DSA_HEAD_DIM


def _hi_lo(x):
    hi = x.astype(BF16).astype(F32)
    return hi, (x - hi).astype(BF16).astype(F32)


def _dsa_body(qb_ref, qi_ref, dtwi_ref, ki_ref, k_ref, v_ref, o_ref,
              kcat_scr, kbf_scr, vt_scr, tri_scr, key_scr, mask_scr, s_scr, acc_scr,
              *, n_kb, n_valid, q_pos0, topk):
    tq, kb_rows = DSA_TQ, DSA_KB
    j = pl.program_id(1)

    @pl.when(j == 0)
    def _():
        hi, lo = _hi_lo(ki_ref[...])
        kcat_scr[...] = jnp.concatenate([hi, lo, hi, jnp.zeros_like(hi)], axis=1).astype(BF16)
        kbf_scr[...] = k_ref[...].astype(BF16)
        ones = jnp.ones((DSA_HEAD_DIM, kb_rows), F32)
        for kb in range(n_kb):
            vt = v_ref[kb * kb_rows:(kb + 1) * kb_rows, :].T
            for g in range(KV_HEADS):
                vt_g = vt[g * DSA_HEAD_DIM:(g + 1) * DSA_HEAD_DIM, :]
                vt_scr[g, kb] = jnp.concatenate([vt_g, ones], axis=0).astype(BF16)
        tr = lax.broadcasted_iota(jnp.int32, (kb_rows, kb_rows), 0)
        tc = lax.broadcasted_iota(jnp.int32, (kb_rows, kb_rows), 1)
        tri_scr[...] = jnp.where(tc < tr, 1.0, 0.0).astype(BF16)

    qpos = q_pos0 + j * tq + lax.broadcasted_iota(jnp.int32, (1, tq), 1)
    q_end = jnp.minimum(((qpos >> 6) + 1) << 6, n_valid)
    k_eff = jnp.minimum(q_end, topk).astype(F32)
    last_end = jnp.minimum((((q_pos0 + (j + 1) * tq - 1) >> 6) + 1) << 6, n_valid)
    nkb = (last_end + (kb_rows - 1)) // kb_rows

    wit = (dtwi_ref[...] * (IDX_DIM ** -0.5 * IDX_HEADS ** -0.5)).T
    qi = qi_ref[...]
    qparts = []
    for h in range(IDX_HEADS):
        hi, lo = _hi_lo(qi[:, h * IDX_DIM:(h + 1) * IDX_DIM])
        qparts.append(jnp.concatenate([hi, hi, lo, jnp.zeros_like(hi)], axis=1))
    qcat = jnp.concatenate(qparts, axis=0).astype(BF16)
    krow = lax.broadcasted_iota(jnp.int32, (kb_rows, 1), 0)

    def score_block(kb, _):
        r0 = pl.multiple_of(kb * kb_rows, kb_rows)
        logit = _mm_nt(kcat_scr[pl.ds(r0, kb_rows), :], qcat)
        sc = jnp.zeros((kb_rows, tq), F32)
        for h in range(IDX_HEADS):
            sc = sc + jnp.maximum(logit[:, h * tq:(h + 1) * tq], 0.0) * wit[WI_LANE + h:WI_LANE + h + 1, :]
        sc = jnp.where(r0 + krow < q_end, sc, -jnp.inf)
        bits = lax.bitcast_convert_type(sc, jnp.int32)
        key_scr[kb] = jnp.where(bits < 0, bits ^ jnp.int32(0x7FFFFFFF), bits)
        return 0

    lax.fori_loop(0, nkb, score_block, 0)

    def count(pred):
        def blk(kb, acc):
            return acc + _fold_rows(jnp.where(pred(key_scr[kb]), 1.0, 0.0), jnp.add)
        acc = lax.fori_loop(0, nkb, blk, jnp.zeros((SUBLANES, tq), F32))
        return jnp.sum(acc, axis=0, keepdims=True)

    def radix_step(i, tu):
        cand = tu | lax.shift_left(jnp.int32(1), 31 - i)
        thr_c = cand ^ jnp.int32(INT_MIN)
        return jnp.where(count(lambda k: k >= thr_c) >= k_eff, cand, tu)

    tu = lax.fori_loop(0, 32, radix_step, jnp.zeros((1, tq), jnp.int32))
    thr = tu ^ jnp.int32(INT_MIN)

    need = k_eff - count(lambda k: k > thr)

    def mask_block(kb, before):
        key = key_scr[kb]
        eq = jnp.where(key == thr, 1.0, 0.0)
        rank = _mm(tri_scr[...], eq.astype(BF16)) + before
        take = jnp.where(key > thr, 1.0, jnp.where(rank < need, eq, 0.0))
        mask_scr[kb] = jnp.where(take > 0.0, 0.0, NEG_BIG)
        return before + jnp.sum(_fold_rows(eq, jnp.add), axis=0, keepdims=True)

    lax.fori_loop(0, nkb, mask_block, jnp.zeros((1, tq), F32))

    qb = qb_ref[...]
    q4 = [jnp.concatenate([qb[:, (g * KV_REP + r) * DSA_HEAD_DIM:(g * KV_REP + r + 1) * DSA_HEAD_DIM]
                           for r in range(KV_REP)], axis=0) for g in range(KV_HEADS)]

    def logits_block(kb, ms):
        r0 = pl.multiple_of(kb * kb_rows, kb_rows)
        mask4 = jnp.concatenate([mask_scr[kb]] * KV_REP, axis=1)
        new = []
        for g in range(KV_HEADS):
            st = _mm_nt(kbf_scr[pl.ds(r0, kb_rows), g * DSA_HEAD_DIM:(g + 1) * DSA_HEAD_DIM], q4[g]) + mask4
            s_scr[kb, g] = st
            new.append(jnp.maximum(ms[g], _fold_rows(st, jnp.maximum)))
        return tuple(new)

    m8 = lax.fori_loop(0, nkb, logits_block,
                       tuple(jnp.full((SUBLANES, KV_REP * tq), NEG_BIG, F32) for _ in range(KV_HEADS)))
    m = [jnp.max(m8[g], axis=0, keepdims=True) for g in range(KV_HEADS)]
    acc_scr[...] = jnp.zeros(acc_scr.shape, F32)

    def pv_block(kb, _):
        for g in range(KV_HEADS):
            p = jnp.exp2(s_scr[kb, g] - m[g]).astype(BF16)
            acc_scr[g] += _mm(vt_scr[g, kb], p)
        return 0

    lax.fori_loop(0, nkb, pv_block, 0)
    outs = []
    for g in range(KV_HEADS):
        ot = acc_scr[g, 0:DSA_HEAD_DIM, :] / acc_scr[g, DSA_HEAD_DIM:DSA_HEAD_DIM + 1, :]
        outs += [ot[:, r * tq:(r + 1) * tq].T for r in range(KV_REP)]
    o_ref[...] = jnp.concatenate(outs, axis=1).astype(o_ref.dtype)


def _dsa(qb, qi, dtwi, ki_all, k_all, v_all, batch, t, s_keys, n_valid, q_pos0):
    tq, kb_rows = DSA_TQ, DSA_KB
    assert t % tq == 0 and s_keys % kb_rows == 0
    nq = t // tq
    n_kb = s_keys // kb_rows
    topk = min(TOPK_MAX, n_valid // 4)
    qrow = lambda b, j: (b * nq + j, 0)
    krow = lambda b, j: (b, 0)
    return pl.pallas_call(
        functools.partial(_dsa_body, n_kb=n_kb, n_valid=n_valid, q_pos0=q_pos0, topk=topk),
        grid=(batch, nq),
        in_specs=[pl.BlockSpec((tq, D_MODEL), qrow), pl.BlockSpec((tq, IDX_HEADS * IDX_DIM), qrow),
                  pl.BlockSpec((tq, LANES), qrow), pl.BlockSpec((s_keys, IDX_DIM), krow),
                  pl.BlockSpec((s_keys, KV_DIM), krow), pl.BlockSpec((s_keys, KV_DIM), krow)],
        out_specs=pl.BlockSpec((tq, D_MODEL), qrow),
        out_shape=jax.ShapeDtypeStruct((batch * t, D_MODEL), BF16),
        scratch_shapes=[pltpu.VMEM((s_keys, IDX_CAT), BF16), pltpu.VMEM((s_keys, KV_DIM), BF16),
                        pltpu.VMEM((KV_HEADS, n_kb, DSA_OT_ROWS, kb_rows), BF16),
                        pltpu.VMEM((kb_rows, kb_rows), BF16),
                        pltpu.VMEM((n_kb, kb_rows, tq), jnp.int32), pltpu.VMEM((n_kb, kb_rows, tq), F32),
                        pltpu.VMEM((n_kb, KV_HEADS, kb_rows, KV_REP * tq), F32),
                        pltpu.VMEM((KV_HEADS, DSA_OT_ROWS, KV_REP * tq), F32)],
        compiler_params=_params(2),
        name="dsa",
    )(qb, qi, dtwi, ki_all, k_all, v_all)


def _band_body(*refs, tq, sub, n_kblk, q_pos0, k_min, clamped):
    q_ref = refs[0]
    k_refs = refs[1:1 + n_kblk]
    v_refs = refs[1 + n_kblk:1 + 2 * n_kblk]
    vec_ref, o_ref, bias_scr = refs[1 + 2 * n_kblk:]
    w = sum(r.shape[0] for r in k_refs)
    wsub = w - tq + sub
    i = pl.program_id(1)

    @pl.when((pl.program_id(0) == 0) & (i == 0))
    def _():
        r = lax.broadcasted_iota(jnp.int32, (sub, wsub), 0)
        c = lax.broadcasted_iota(jnp.int32, (sub, wsub), 1)
        dchunk = (r >> 6) + (wsub - sub) // CHUNK - (c >> 6)
        band_mask = jnp.where((dchunk >= 0) & (dchunk <= LEFT_CHUNKS), 0.0, NEG_BIG)
        for h in range(BAND_HEADS):
            rows = jnp.broadcast_to(vec_ref[h:h + 1, :], (sub, vec_ref.shape[1]))
            toeplitz = pltpu.roll(rows, 0, 1, stride=1, stride_axis=0)[:, :wsub]
            bias_scr[h] = toeplitz * LOG2E + band_mask

    def heads(mask_missing_keys):
        ones = jnp.ones((w, BAND_HEAD_DIM), F32)
        outs = []
        for h in range(BAND_HEADS):
            hs = slice(h * BAND_HEAD_DIM, (h + 1) * BAND_HEAD_DIM)
            kh = jnp.concatenate([r[:, hs] for r in k_refs], axis=0).astype(BF16)
            vh = jnp.concatenate([jnp.concatenate([r[:, hs] for r in v_refs], axis=0), ones],
                                 axis=1).astype(BF16)
            subs = []
            for c2 in range(tq // sub):
                rows = slice(c2 * sub, (c2 + 1) * sub)
                win = slice(c2 * sub, c2 * sub + wsub)
                s = _mm_nt(q_ref[rows, hs], kh[win]) + bias_scr[h]
                if mask_missing_keys:
                    kpos = q_pos0 + i * tq + c2 * sub + (sub - wsub) + lax.broadcasted_iota(jnp.int32, (1, wsub), 1)
                    s = s + jnp.where(kpos >= k_min, 0.0, NEG_BIG)
                p = jnp.exp2(s - jnp.max(s, axis=1, keepdims=True)).astype(BF16)
                o = _mm(p, vh[win])
                subs.append(o[:, :BAND_HEAD_DIM] / o[:, BAND_HEAD_DIM:BAND_HEAD_DIM + 1])
            outs.append(jnp.concatenate(subs, axis=0))
        o_ref[...] = jnp.concatenate(outs, axis=1).astype(o_ref.dtype)

    if clamped:
        first_full = -(-(w - tq) // tq)
        pl.when(i < first_full)(functools.partial(heads, True))
        pl.when(i >= first_full)(functools.partial(heads, False))
    else:
        heads(False)


def _band_bias_vec(rel_bias, sub, wsub):
    l = -(-(wsub + sub) // LANES) * LANES
    m = np.arange(l)
    d = np.where(m < wsub, m, m - l)
    rel = np.clip(wsub - sub - d, -REL_CLIP, REL_CLIP) + REL_CLIP
    return rel_bias[jnp.asarray(rel)].T


def _band(q, k, v, rel_bias, batch, t, tq, sub, k_block_rows, n_kblk, q_pos0, k_min, clamped):
    nq = t // tq
    w = n_kblk * k_block_rows
    wsub = w - tq + sub
    assert wsub % LANES == 0 and (wsub - sub) % CHUNK == 0
    qrow = lambda b, i: (b * nq + i, 0)

    def krow(off):
        if clamped:
            return lambda b, i: (b * nq + jnp.maximum(i - (n_kblk - 1) + off, 0), 0)
        return lambda b, i: (b * n_kblk + off, 0)

    vec = _band_bias_vec(rel_bias, sub, wsub)
    kspecs = [pl.BlockSpec((k_block_rows, D_MODEL), krow(o)) for o in range(n_kblk)]
    return pl.pallas_call(
        functools.partial(_band_body, tq=tq, sub=sub, n_kblk=n_kblk, q_pos0=q_pos0, k_min=k_min, clamped=clamped),
        grid=(batch, nq),
        in_specs=[pl.BlockSpec((tq, D_MODEL), qrow)] + kspecs + kspecs + [_resident(vec.shape)],
        out_specs=pl.BlockSpec((tq, D_MODEL), qrow),
        out_shape=jax.ShapeDtypeStruct((batch * t, D_MODEL), BF16),
        scratch_shapes=[pltpu.VMEM((BAND_HEADS, sub, wsub), F32)],
        compiler_params=_params(2),
        name="band",
    )(q, *([k] * n_kblk), *([v] * n_kblk), vec)


def _merge_body(x_ref, ya_ref, yb_ref, yc_ref, g_ref, wg_ref, bg_ref, wbr_ref, wo_ref, o_ref):
    x = x_ref[...]
    u = _rms(x, g_ref[2:3, :]).astype(BF16)
    mix = jnp.zeros(x.shape, F32)
    for k, y_ref in enumerate((ya_ref, yb_ref, yc_ref)):
        sl = slice(k * D_MODEL, (k + 1) * D_MODEL)
        gate = _sigmoid(_mm(u, wg_ref[:, sl]) + bg_ref[:, sl])
        mix = mix + gate * _mm(y_ref[...], wbr_ref[k])
    o_ref[...] = x + _rms(_mm(mix.astype(BF16), wo_ref[...]), g_ref[3:4, :])


def _merge(x, ya, yb, yc, g, wg, bg, wbr, wo):
    n = x.shape[0]
    tm = _row_tile(n, 256)
    row = lambda i: (i, 0)
    tile = pl.BlockSpec((tm, D_MODEL), row)
    return pl.pallas_call(
        _merge_body,
        grid=(n // tm,),
        in_specs=[tile, tile, tile, tile] + [_resident(a.shape) for a in (g, wg, bg, wbr, wo)],
        out_specs=tile,
        out_shape=jax.ShapeDtypeStruct((n, D_MODEL), F32),
        compiler_params=_params(1),
        name="merge",
    )(x, ya, yb, yc, g, wg, bg, wbr, wo)


def _ple_body(x_ref, p_ref, g_ref, wp_ref, wpg_ref, o_ref):
    x = x_ref[...]
    e = _mm(p_ref[...].astype(BF16), wp_ref[...])
    pg = _sigmoid(_mm(_rms(x, g_ref[6:7, :]).astype(BF16), wpg_ref[...]))
    o_ref[...] = x + _rms(pg * e, g_ref[7:8, :])


def _ple(x, p, g, wp, wpg):
    n = x.shape[0]
    tm = _row_tile(n, 512)
    row = lambda i: (i, 0)
    return pl.pallas_call(
        _ple_body,
        grid=(n // tm,),
        in_specs=[pl.BlockSpec((tm, D_MODEL), row), pl.BlockSpec((tm, PLE_DIM), row)]
        + [_resident(a.shape) for a in (g, wp, wpg)],
        out_specs=pl.BlockSpec((tm, D_MODEL), row),
        out_shape=jax.ShapeDtypeStruct((n, D_MODEL), F32),
        compiler_params=_params(1),
        name="ple",
    )(x, p, g, wp, wpg)


BAND_TQ = 256
BAND_SUB = 128
BAND_KBLK = 1 + -(-BAND // BAND_TQ)


def _prep_layer(w):
    bf = lambda a: a.astype(BF16)
    return {
        "g": w["norm_g"],
        "ffn": [(bf(w["ffn_w13"][j][:, :D_FF]), bf(w["ffn_w13"][j][:, D_FF:]), bf(w["ffn_w2"][j])) for j in range(2)],
        "w_in": _pack_w_in(w["w_in"]),
        "wg": bf(w["w_gate"]), "bg": w["b_gate"].reshape(1, N_BRANCH * D_MODEL),
        "wbr": bf(w["w_branch"]), "wo": bf(w["w_out"]),
        "wp": bf(w["w_ple"]), "wpg": bf(w["w_ple_gate"]),
    }


def _trunk_layer(x, p, w, pw, cache, batch, t):
    g = pw["g"]
    x = _ffn(x, g, *pw["ffn"][0], 0, 1)
    pr = _inproj(x, g, pw["w_in"])
    if cache is None:
        ya, h_new = _mamba(pr["z"], pr["xbc"], pr["dtwi"], None, None, w, batch, t)
        yb = _dsa(pr["qb"], pr["qi"], pr["dtwi"], pr["ki"], pr["kb"], pr["vb"], batch, t, t, t, 0)
        yc = _band(pr["qc"], pr["kc"], pr["vc"], w["rel_bias"], batch, t, BAND_TQ, BAND_SUB, BAND_TQ, BAND_KBLK,
                   0, 0, True)
        conv_src = pr["xbc"].reshape(batch, t, CONV_CH)
    else:
        past = cache["dsa_k"].shape[1]
        conv0 = jnp.pad(cache["conv"], ((0, 0), (CONV_PAD - (CONV_W - 1), 0), (0, 0)))
        h0 = cache["ssm"].reshape(batch, D_INNER, D_STATE)
        ya, h_new = _mamba(pr["z"], pr["xbc"], pr["dtwi"], conv0, h0, w, batch, t)

        n_valid = past + t
        s_keys = -(-n_valid // DSA_KB) * DSA_KB
        tq_pad = -(-t // DSA_TQ) * DSA_TQ

        def with_cache(c, new):
            width = new.shape[-1]
            a = jnp.concatenate([c.reshape(batch, past, width), new.reshape(batch, t, width)], axis=1)
            return jnp.pad(a, ((0, 0), (0, s_keys - n_valid), (0, 0))).reshape(batch * s_keys, width)

        def pad_q(a):
            a = jnp.pad(a.reshape(batch, t, a.shape[-1]), ((0, 0), (0, tq_pad - t), (0, 0)))
            return a.reshape(batch * tq_pad, a.shape[-1])

        yb = _dsa(pad_q(pr["qb"]), pad_q(pr["qi"]), pad_q(pr["dtwi"]), with_cache(cache["idx_k"], pr["ki"]),
                  with_cache(cache["dsa_k"], pr["kb"]), with_cache(cache["dsa_v"], pr["vb"]),
                  batch, tq_pad, s_keys, n_valid, past)
        yb = yb.reshape(batch, tq_pad, D_MODEL)[:, :t].reshape(batch * t, D_MODEL)

        nrows = cache["band_k"].shape[1]
        k_rows = -(-(nrows + t) // LANES) * LANES
        lead = k_rows - nrows - t

        def with_band(c, new):
            a = jnp.concatenate([c.reshape(batch, nrows, D_MODEL), new.reshape(batch, t, D_MODEL)], axis=1)
            return jnp.pad(a, ((0, 0), (lead, 0), (0, 0))).reshape(batch * k_rows, D_MODEL)

        yc = _band(pr["qc"], with_band(cache["band_k"], pr["kc"]), with_band(cache["band_v"], pr["vc"]),
                   w["rel_bias"], batch, t, t, t, k_rows, 1, past, past - nrows, False)
        conv_src = jnp.concatenate([cache["conv"], pr["xbc"].reshape(batch, t, CONV_CH)], axis=1)

    x = _merge(x, ya, yb, yc, g, pw["wg"], pw["bg"], pw["wbr"], pw["wo"])
    x = _ffn(x, g, *pw["ffn"][1], 4, 5)
    x = _ple(x, p.reshape(batch * t, PLE_DIM), g, pw["wp"], pw["wpg"])

    band_rows = min(BAND, t)
    kc3 = pr["kc"].reshape(batch, t, BAND_HEADS, BAND_HEAD_DIM)
    vc3 = pr["vc"].reshape(batch, t, BAND_HEADS, BAND_HEAD_DIM)
    state = (pr["kb"].reshape(batch, t, KV_HEADS, DSA_HEAD_DIM), pr["vb"].reshape(batch, t, KV_HEADS, DSA_HEAD_DIM),
             pr["ki"].reshape(batch, t, IDX_DIM), kc3[:, t - band_rows:], vc3[:, t - band_rows:],
             h_new.reshape(batch, SSM_HEADS, SSM_HEAD_DIM, D_STATE), conv_src[:, -(CONV_W - 1):])
    return x, state


def kernel(x_prompt, x_sample, p_prompt, p_sample, cache_dsa_k, cache_dsa_v, cache_idx_k, cache_band_k,
           cache_band_v, state_ssm, state_conv, norm_g, ffn_w13, ffn_w2, w_in, conv_w, conv_b, dt_bias,
           a_log, d_skip, ssm_norm_g, rel_bias, w_gate, b_gate, w_branch, w_out, w_ple, w_ple_gate):
    bp, tp, _ = x_prompt.shape
    bs, ts, _ = x_sample.shape
    yp = x_prompt.reshape(bp * tp, D_MODEL)
    ys = x_sample.reshape(bs * ts, D_MODEL)
    st_p, st_s = [], []
    for i in range(DEPTH):
        w = {"norm_g": norm_g[i], "ffn_w13": ffn_w13[i], "ffn_w2": ffn_w2[i], "w_in": w_in[i],
             "conv_w": conv_w[i], "conv_b": conv_b[i], "dt_bias": dt_bias[i], "a_log": a_log[i],
             "d_skip": d_skip[i], "ssm_norm_g": ssm_norm_g[i], "rel_bias": rel_bias[i],
             "w_gate": w_gate[i], "b_gate": b_gate[i], "w_branch": w_branch[i], "w_out": w_out[i],
             "w_ple": w_ple[i], "w_ple_gate": w_ple_gate[i]}
        pw = _prep_layer(w)
        yp, sp = _trunk_layer(yp, p_prompt[i], w, pw, None, bp, tp)
        st_p.append(sp)
        cache = {"dsa_k": cache_dsa_k[i], "dsa_v": cache_dsa_v[i], "idx_k": cache_idx_k[i],
                 "band_k": cache_band_k[i], "band_v": cache_band_v[i], "ssm": state_ssm[i], "conv": state_conv[i]}
        ys, ss = _trunk_layer(ys, p_sample[i], w, pw, cache, bs, ts)
        st_s.append(ss)
    outs_p = [jnp.stack(c) for c in zip(*st_p)]
    outs_s = [jnp.stack(c) for c in zip(*st_s)]
    return (yp.reshape(bp, tp, D_MODEL), ys.reshape(bs, ts, D_MODEL), *outs_p, *outs_s)
```

```python
import functools
import math

import numpy as np
import jax
import jax.numpy as jnp
from jax import lax
from jax.experimental import pallas as pl
from jax.experimental.pallas import tpu as pltpu

F32 = jnp.float32
BF16 = jnp.bfloat16

D_MODEL = 1024
DEPTH = 2
CHUNK = 64
EPS = 1e-6
HALF = 0.5
D_FF = 2816
PLE_DIM = 256
SSM_HEAD_DIM = 64
D_INNER = D_MODEL
SSM_HEADS = D_INNER // SSM_HEAD_DIM
N_GROUPS = 4
HEADS_PER_GROUP = SSM_HEADS // N_GROUPS
D_STATE = 128
CONV_W = 4
CONV_CH = D_INNER + 2 * N_GROUPS * D_STATE
DSA_HEAD_DIM = 128
DSA_HEADS = D_MODEL // DSA_HEAD_DIM
KV_HEADS = 2
KV_REP = DSA_HEADS // KV_HEADS
IDX_HEADS = 4
IDX_DIM = 64
TOPK_MAX = 256
BAND_HEAD_DIM = 64
BAND_HEADS = D_MODEL // BAND_HEAD_DIM
LEFT_CHUNKS = 8
BAND = LEFT_CHUNKS * CHUNK
REL_CLIP = 256
N_BRANCH = 3
IN_WIDTHS = (D_INNER, CONV_CH, SSM_HEADS,
             DSA_HEADS * DSA_HEAD_DIM, KV_HEADS * DSA_HEAD_DIM, KV_HEADS * DSA_HEAD_DIM,
             IDX_HEADS * IDX_DIM, IDX_DIM, IDX_HEADS,
             BAND_HEADS * BAND_HEAD_DIM, BAND_HEADS * BAND_HEAD_DIM, BAND_HEADS * BAND_HEAD_DIM)
IN_SPLITS = tuple(int(s) for s in np.cumsum(IN_WIDTHS)[:-1])

LANES = 128
SUBLANES = 8
KV_DIM = KV_HEADS * DSA_HEAD_DIM
GROUP_CH = D_INNER // N_GROUPS
NEG_BIG = -1e30
INT_MIN = -2 ** 31
LOG2E = math.log2(math.e)
VMEM_LIMIT = 56 * 1024 * 1024


def _mm(a, b):
    return jnp.dot(a, b, preferred_element_type=F32)


def _mm_nt(a, b):
    return lax.dot_general(a, b, (((1,), (1,)), ((), ())), preferred_element_type=F32)


def _mm_tn(a, b):
    return lax.dot_general(a, b, (((0,), (0,)), ((), ())), preferred_element_type=F32)


def _rms(x, g):
    return x * lax.rsqrt(jnp.mean(x * x, axis=-1, keepdims=True) + EPS) * g


def _sigmoid(x):
    return 1.0 / (1.0 + jnp.exp(-x))


def _silu(x):
    return x * _sigmoid(x)


def _resident(shape):
    return pl.BlockSpec(shape, lambda *_: (0,) * len(shape), pipeline_mode=pl.Buffered(1))


def _resident_slice(arr, lead, block=None, at=None):
    tail = tuple(arr.shape[len(lead):]) if block is None else tuple(block)
    idx = tuple(lead) + ((0,) * len(tail) if at is None else tuple(at))
    return pl.BlockSpec((None,) * len(lead) + tail, lambda *_: idx, pipeline_mode=pl.Buffered(1))


def _params(n_grid_dims):
    return pltpu.CompilerParams(dimension_semantics=("arbitrary",) * n_grid_dims,
                                vmem_limit_bytes=VMEM_LIMIT)


def _row_tile(n_rows, want):
    t = min(want, n_rows)
    assert n_rows % t == 0
    return t


def _fold_rows(x, op):
    parts = [x[i * SUBLANES:(i + 1) * SUBLANES] for i in range(x.shape[0] // SUBLANES)]
    while len(parts) > 1:
        parts = [op(parts[i], parts[i + 1]) for i in range(0, len(parts) - 1, 2)] + parts[len(parts) & ~1:]
    return parts[0]


FF_CHUNK = 256


def _ffn_body(x_ref, g_ref, wa_ref, wb_ref, w2_ref, o_ref, *, g_pre, g_post):
    x = x_ref[...]
    u = _rms(x, g_ref[g_pre:g_pre + 1, :]).astype(BF16)
    acc = jnp.zeros(x.shape, F32)
    for c in range(D_FF // FF_CHUNK):
        sl = slice(c * FF_CHUNK, (c + 1) * FF_CHUNK)
        a = _mm(u, wa_ref[:, sl])
        b = _mm(u, wb_ref[:, sl])
        acc = acc + _mm((_silu(a) * b).astype(BF16), w2_ref[sl, :])
    o_ref[...] = x + HALF * _rms(acc, g_ref[g_post:g_post + 1, :])


def _ffn(x, pw, layer, j, g_pre, g_post):
    n = x.shape[0]
    tm = _row_tile(n, 512)
    row = lambda i: (i, 0)
    half = (D_MODEL, D_FF)
    return pl.pallas_call(
        functools.partial(_ffn_body, g_pre=g_pre, g_post=g_post),
        grid=(n // tm,),
        in_specs=[pl.BlockSpec((tm, D_MODEL), row), _resident_slice(pw["g"], (layer,)),
                  _resident_slice(pw["w13"], (layer, j), half, (0, 0)),
                  _resident_slice(pw["w13"], (layer, j), half, (0, 1)),
                  _resident_slice(pw["w2"], (layer, j))],
        out_specs=pl.BlockSpec((tm, D_MODEL), row),
        out_shape=jax.ShapeDtypeStruct((n, D_MODEL), F32),
        compiler_params=_params(1),
        name="ffn",
    )(x, pw["g"], pw["w13"], pw["w13"], pw["w2"])


_INPROJ_GROUPS = (D_INNER, CONV_CH, D_MODEL, KV_DIM, KV_DIM, IDX_HEADS * IDX_DIM, D_MODEL, D_MODEL, D_MODEL,
                  LANES, LANES)
_INPROJ_OUT = (
    ("z", 0, D_INNER, F32, None),
    ("xbc", 1, CONV_CH, F32, None),
    ("qb", 2, D_MODEL, BF16, DSA_HEAD_DIM ** -0.5 * LOG2E),
    ("kb", 3, KV_DIM, F32, None),
    ("vb", 4, KV_DIM, F32, None),
    ("qi", 5, IDX_HEADS * IDX_DIM, F32, None),
    ("qc", 6, D_MODEL, BF16, BAND_HEAD_DIM ** -0.5 * LOG2E),
    ("kc", 7, D_MODEL, F32, None),
    ("kc_bf", 7, D_MODEL, BF16, None),
    ("vc", 8, D_MODEL, F32, None),
    ("vc_bf", 8, D_MODEL, BF16, None),
    ("ki", 9, IDX_DIM, F32, None),
    ("dtwi", 10, LANES, F32, None),
)
WI_LANE = SSM_HEADS


def _pack_w_in(w_in):
    z, xbc, dt, qb, kb, vb, qi, ki, wi, qc, kc, vc = jnp.split(w_in, IN_SPLITS, axis=-1)
    pad = lambda w: jnp.pad(w, ((0, 0), (0, 0), (0, LANES - w.shape[-1])))
    cols = [z, xbc, qb, kb, vb, qi, qc, kc, vc, pad(ki), pad(jnp.concatenate([dt, wi], axis=-1))]
    return jnp.concatenate(cols, axis=-1).astype(BF16)


def _inproj_body(x_ref, g_ref, w_ref, *out_refs):
    u = _rms(x_ref[...], g_ref[2:3, :]).astype(BF16)
    starts = np.concatenate([[0], np.cumsum(_INPROJ_GROUPS)])
    for grp, width in enumerate(_INPROJ_GROUPS):
        r = _mm(u, w_ref[:, int(starts[grp]):int(starts[grp]) + width])
        for (_, og, stored, dtype, scale), o_ref in zip(_INPROJ_OUT, out_refs):
            if og == grp:
                v = r if scale is None else r * scale
                o_ref[...] = (v if stored == width else v[:, :stored]).astype(dtype)


def _inproj(x, pw, layer):
    n = x.shape[0]
    tm = _row_tile(n, 256)
    row = lambda i: (i, 0)
    outs = pl.pallas_call(
        _inproj_body,
        grid=(n // tm,),
        in_specs=[pl.BlockSpec((tm, D_MODEL), row), _resident_slice(pw["g"], (layer,)),
                  _resident_slice(pw["w_in"], (layer,))],
        out_specs=[pl.BlockSpec((tm, o[2]), row) for o in _INPROJ_OUT],
        out_shape=[jax.ShapeDtypeStruct((n, o[2]), o[3]) for o in _INPROJ_OUT],
        compiler_params=_params(1),
        name="inproj",
    )(x, pw["g"], pw["w_in"])
    return {o[0]: a for o, a in zip(_INPROJ_OUT, outs)}


CONV_PAD = 8


def _split3(x):
    hi = x.astype(BF16)
    r = x - hi.astype(F32)
    mid = r.astype(BF16)
    lo = (r - mid.astype(F32)).astype(BF16)
    return hi, mid, lo


def _expand_heads(x, e):
    hi, mid, lo = _split3(x)
    return _mm(hi, e) + _mm(mid, e) + _mm(lo, e)


def _cumsum_rows(x):
    n = x.shape[0]
    row = lax.broadcasted_iota(jnp.int32, x.shape, 0)
    d = 1
    while d < n:
        x = x + jnp.where(row >= d, pltpu.roll(x, d, 0), 0.0)
        d *= 2
    return x


def _mamba_body(*refs, has_state, n_chunks):
    if has_state:
        (z_ref, xbc_ref, dtwi_ref, conv0_ref, h0_ref, cw_ref, cb_ref, dtb_ref, alog_ref, dskip_ref,
         ng_ref, e_ref, y_ref, hout_ref, xp_scr, ht_scr) = refs
    else:
        (z_ref, xbc_ref, dtwi_ref, cw_ref, cb_ref, dtb_ref, alog_ref, dskip_ref,
         ng_ref, e_ref, y_ref, hout_ref, xp_scr, ht_scr) = refs
    c = pl.program_id(1)
    q = CHUNK

    @pl.when(c == 0)
    def _():
        if has_state:
            xp_scr[0:CONV_PAD, :] = conv0_ref[...]
            ht_scr[...] = h0_ref[...].T
        else:
            xp_scr[0:CONV_PAD, :] = jnp.zeros((CONV_PAD, CONV_CH), F32)
            ht_scr[...] = jnp.zeros(ht_scr.shape, F32)

    @pl.when(c > 0)
    def _():
        xp_scr[0:CONV_PAD, :] = xp_scr[q:q + CONV_PAD, :]

    xp_scr[CONV_PAD:CONV_PAD + q, :] = xbc_ref[...]
    acc = cb_ref[...]
    for k in range(CONV_W):
        r0 = CONV_PAD - (CONV_W - 1) + k
        acc = acc + xp_scr[r0:r0 + q, :] * cw_ref[k:k + 1, :]
    xc = _silu(acc)
    xs = xc[:, :D_INNER]
    bm = xc[:, D_INNER:D_INNER + N_GROUPS * D_STATE]
    cm = xc[:, D_INNER + N_GROUPS * D_STATE:]

    lane = lax.broadcasted_iota(jnp.int32, (q, LANES), 1)
    pre = dtwi_ref[...] + dtb_ref[...]
    dt = jnp.maximum(pre, 0.0) + jnp.log1p(jnp.exp(-jnp.abs(pre)))
    dt = jnp.where(lane < SSM_HEADS, dt, 0.0)
    dta = dt * (-jnp.exp(alog_ref[...]))
    cum = _cumsum_rows(dta)

    e = e_ref[...]
    ecol = _expand_heads(cum, e)
    dtx = _expand_heads(dt, e)
    li = lax.broadcasted_iota(jnp.int32, (q, D_INNER), 0)
    si = lax.broadcasted_iota(jnp.int32, (q, D_INNER), 1) & (q - 1)
    erow = jnp.sum(jnp.where(li == si, ecol, 0.0), axis=0, keepdims=True)
    elast = ecol[q - 1:q, :]
    causal = (li >= si)[:, :GROUP_CH]
    xdt = xs * dtx
    xdec = (xdt * jnp.exp(elast - ecol)).astype(BF16)
    exp_e = jnp.exp(ecol)
    chunk_decay = jnp.exp(elast)
    bdr = lax.broadcasted_iota(jnp.int32, (GROUP_CH, GROUP_CH), 0) // SSM_HEAD_DIM
    bdc = lax.broadcasted_iota(jnp.int32, (GROUP_CH, GROUP_CH), 1) // SSM_HEAD_DIM
    block_diag = bdr == bdc

    ys = []
    for g in range(N_GROUPS):
        sl = slice(g * GROUP_CH, (g + 1) * GROUP_CH)
        nl = slice(g * D_STATE, (g + 1) * D_STATE)
        bg = bm[:, nl].astype(BF16)
        cg = cm[:, nl].astype(BF16)
        cb = _mm_nt(cg, bg)
        cbt = jnp.concatenate([cb] * HEADS_PER_GROUP, axis=1)
        decay = jnp.exp(jnp.where(causal, ecol[:, sl] - erow[:, sl], NEG_BIG))
        m = (cbt * decay).astype(BF16)
        xg = xdt[:, sl]
        bd = jnp.where(block_diag, jnp.concatenate([xg] * HEADS_PER_GROUP, axis=0), 0.0).astype(BF16)
        y_diag = _mm(m, bd)
        ht_g = ht_scr[:, sl]
        y_off = _mm(cg, ht_g.astype(BF16)) * exp_e[:, sl]
        ys.append(y_diag + y_off)
        ht_scr[:, sl] = ht_g * chunk_decay[:, sl] + _mm_tn(bg, xdec[:, sl])
    y = jnp.concatenate(ys, axis=1) + dskip_ref[...] * xs
    y = y * _silu(z_ref[...])
    outs = []
    for g in range(N_GROUPS):
        yg = y[:, g * GROUP_CH:(g + 1) * GROUP_CH]
        outs.append(yg * lax.rsqrt(jnp.mean(yg * yg, axis=-1, keepdims=True) + EPS))
    y_ref[...] = (jnp.concatenate(outs, axis=1) * ng_ref[...]).astype(y_ref.dtype)

    @pl.when(c == n_chunks - 1)
    def _():
        hout_ref[...] = ht_scr[...].T


def _head_expand_matrix():
    e = np.zeros((LANES, D_INNER), np.float32)
    for h in range(SSM_HEADS):
        e[h, h * SSM_HEAD_DIM:(h + 1) * SSM_HEAD_DIM] = 1.0
    return jnp.asarray(e, BF16)


def _mamba(z, xbc, dtwi, conv0, h0, lw, batch, t):
    nc = t // CHUNK
    has_state = h0 is not None
    row = lambda b, c: (b * nc + c, 0)
    per_b = lambda b, c: (b, 0, 0)
    pad16 = lambda v: jnp.pad(v.reshape(1, SSM_HEADS), ((0, 0), (0, LANES - SSM_HEADS)))
    small = [lw["conv_w"], lw["conv_b"].reshape(1, CONV_CH), pad16(lw["dt_bias"]), pad16(lw["a_log"]),
             jnp.repeat(lw["d_skip"], SSM_HEAD_DIM).reshape(1, D_INNER),
             lw["ssm_norm_g"].reshape(1, D_INNER), _head_expand_matrix()]
    ins = [z, xbc, dtwi]
    in_specs = [pl.BlockSpec((CHUNK, D_INNER), row), pl.BlockSpec((CHUNK, CONV_CH), row),
                pl.BlockSpec((CHUNK, LANES), row)]
    if has_state:
        ins += [conv0, h0]
        in_specs += [pl.BlockSpec((None, CONV_PAD, CONV_CH), per_b),
                     pl.BlockSpec((None, D_INNER, D_STATE), per_b)]
    ins += small
    in_specs += [_resident(a.shape) for a in small]
    y, h_out = pl.pallas_call(
        functools.partial(_mamba_body, has_state=has_state, n_chunks=nc),
        grid=(batch, nc),
        in_specs=in_specs,
        out_specs=[pl.BlockSpec((CHUNK, D_INNER), row), pl.BlockSpec((None, D_INNER, D_STATE), per_b)],
        out_shape=[jax.ShapeDtypeStruct((batch * t, D_INNER), BF16),
                   jax.ShapeDtypeStruct((batch, D_INNER, D_STATE), F32)],
        scratch_shapes=[pltpu.VMEM((CONV_PAD + CHUNK, CONV_CH), F32), pltpu.VMEM((D_STATE, D_INNER), F32)],
        compiler_params=_params(2),
        name="mamba",
    )(*ins)
    return y, h_out


DSA_TQ = LANES
DSA_KB = 512
IDX_CAT = 4 * IDX_DIM
DSA_OT_ROWS = 2 * DSA_HEAD_DIM


def _hi_lo(x):
    hi = x.astype(BF16).astype(F32)
    return hi, (x - hi).astype(BF16).astype(F32)


def _dsa_body(qb_ref, qi_ref, dtwi_ref, ki_ref, k_ref, v_ref, o_ref,
              kcat_scr, kbf_scr, vt_scr, tri_scr, key_scr, mask_scr, s_scr,
              *, n_kb, n_valid, q_pos0, topk, visible):
    tq, kb_rows = DSA_TQ, DSA_KB
    j = pl.program_id(1)

    @pl.when(j == 0)
    def _():
        hi, lo = _hi_lo(ki_ref[...])
        kcat_scr[...] = jnp.concatenate([hi, lo, hi, jnp.zeros_like(hi)], axis=1).astype(BF16)
        kbf_scr[...] = k_ref[...].astype(BF16)
        ones = jnp.ones((DSA_HEAD_DIM, kb_rows), F32)
        for kb in range(n_kb):
            vt = v_ref[kb * kb_rows:(kb + 1) * kb_rows, :].T
            for g in range(KV_HEADS):
                vt_g = vt[g * DSA_HEAD_DIM:(g + 1) * DSA_HEAD_DIM, :]
                vt_scr[g, kb] = jnp.concatenate([vt_g, ones], axis=0).astype(BF16)
        tr = lax.broadcasted_iota(jnp.int32, (kb_rows, kb_rows), 0)
        tc = lax.broadcasted_iota(jnp.int32, (kb_rows, kb_rows), 1)
        tri_scr[...] = jnp.where(tc < tr, 1.0, 0.0).astype(BF16)

    qpos = q_pos0 + j * tq + lax.broadcasted_iota(jnp.int32, (1, tq), 1)
    q_end = jnp.minimum(((qpos >> 6) + 1) << 6, n_valid)
    k_eff = jnp.minimum(q_end, topk).astype(F32)
    last_end = jnp.minimum((((q_pos0 + (j + 1) * tq - 1) >> 6) + 1) << 6, n_valid)
    nkb = (last_end + (kb_rows - 1)) // kb_rows

    wit = (dtwi_ref[...] * (IDX_DIM ** -0.5 * IDX_HEADS ** -0.5)).T
    qi = qi_ref[...]
    qparts = []
    for h in range(IDX_HEADS):
        hi, lo = _hi_lo(qi[:, h * IDX_DIM:(h + 1) * IDX_DIM])
        qparts.append(jnp.concatenate([hi, hi, lo, jnp.zeros_like(hi)], axis=1))
    qcat = jnp.concatenate(qparts, axis=0).astype(BF16)
    krow = lax.broadcasted_iota(jnp.int32, (kb_rows, 1), 0)
    qb = qb_ref[...]
    q4 = [jnp.concatenate([qb[:, (g * KV_REP + r) * DSA_HEAD_DIM:(g * KV_REP + r + 1) * DSA_HEAD_DIM]
                           for r in range(KV_REP)], axis=0) for g in range(KV_HEADS)]

    def run(n_vis):
        for kb in range(n_vis):
            logit = _mm_nt(kcat_scr[kb * kb_rows:(kb + 1) * kb_rows, :], qcat)
            sc = jnp.zeros((kb_rows, tq), F32)
            for h in range(IDX_HEADS):
                sc = sc + jnp.maximum(logit[:, h * tq:(h + 1) * tq], 0.0) * wit[WI_LANE + h:WI_LANE + h + 1, :]
            sc = jnp.where(kb * kb_rows + krow < q_end, sc, -jnp.inf)
            bits = lax.bitcast_convert_type(sc, jnp.int32)
            key_scr[kb] = jnp.where(bits < 0, bits ^ jnp.int32(0x7FFFFFFF), bits)

        def count(pred):
            acc = _fold_rows(jnp.where(pred(key_scr[0]), 1.0, 0.0), jnp.add)
            for kb in range(1, n_vis):
                acc = acc + _fold_rows(jnp.where(pred(key_scr[kb]), 1.0, 0.0), jnp.add)
            return jnp.sum(acc, axis=0, keepdims=True)

        def radix_step(i, tu):
            cand = tu | lax.shift_left(jnp.int32(1), 31 - i)
            thr_c = cand ^ jnp.int32(INT_MIN)
            return jnp.where(count(lambda k: k >= thr_c) >= k_eff, cand, tu)

        tu = lax.fori_loop(0, 32, radix_step, jnp.zeros((1, tq), jnp.int32))
        thr = tu ^ jnp.int32(INT_MIN)

        need = k_eff - count(lambda k: k > thr)
        before = jnp.zeros((1, tq), F32)
        for kb in range(n_vis):
            key = key_scr[kb]
            eq = jnp.where(key == thr, 1.0, 0.0)
            rank = _mm(tri_scr[...], eq.astype(BF16)) + before
            take = jnp.where(key > thr, 1.0, jnp.where(rank < need, eq, 0.0))
            mask_scr[kb] = jnp.where(take > 0.0, 0.0, NEG_BIG)
            before = before + jnp.sum(_fold_rows(eq, jnp.add), axis=0, keepdims=True)

        m8 = [jnp.full((SUBLANES, KV_REP * tq), NEG_BIG, F32) for _ in range(KV_HEADS)]
        for kb in range(n_vis):
            mask4 = jnp.concatenate([mask_scr[kb]] * KV_REP, axis=1)
            for g in range(KV_HEADS):
                kg = kbf_scr[kb * kb_rows:(kb + 1) * kb_rows, g * DSA_HEAD_DIM:(g + 1) * DSA_HEAD_DIM]
                st = _mm_nt(kg, q4[g]) + mask4
                s_scr[kb, g] = st
                m8[g] = jnp.maximum(m8[g], _fold_rows(st, jnp.maximum))
        m = [jnp.max(m8[g], axis=0, keepdims=True) for g in range(KV_HEADS)]
        acc = [None] * KV_HEADS
        for kb in range(n_vis):
            for g in range(KV_HEADS):
                p = jnp.exp2(s_scr[kb, g] - m[g]).astype(BF16)
                pv = _mm(vt_scr[g, kb], p)
                acc[g] = pv if acc[g] is None else acc[g] + pv
        outs = []
        for g in range(KV_HEADS):
            ot = acc[g][0:DSA_HEAD_DIM, :] / acc[g][DSA_HEAD_DIM:DSA_HEAD_DIM + 1, :]
            outs += [ot[:, r * tq:(r + 1) * tq].T for r in range(KV_REP)]
        o_ref[...] = jnp.concatenate(outs, axis=1).astype(o_ref.dtype)

    if len(visible) == 1:
        run(visible[0])
    else:
        for n_vis in visible:
            pl.when(nkb == n_vis)(functools.partial(run, n_vis))


def _dsa(qb, qi, dtwi, ki_all, k_all, v_all, batch, t, s_keys, n_valid, q_pos0):
    tq, kb_rows = DSA_TQ, DSA_KB
    assert t % tq == 0 and s_keys % kb_rows == 0
    nq = t // tq
    n_kb = s_keys // kb_rows
    topk = min(TOPK_MAX, n_valid // 4)
    qrow = lambda b, j: (b * nq + j, 0)
    krow = lambda b, j: (b, 0)

    def visible_blocks(j):
        last_end = min(((((q_pos0 + (j + 1) * tq - 1) >> 6) + 1) << 6), n_valid)
        return (last_end + kb_rows - 1) // kb_rows

    visible = tuple(sorted({visible_blocks(j) for j in range(nq)}))
    return pl.pallas_call(
        functools.partial(_dsa_body, n_kb=n_kb, n_valid=n_valid, q_pos0=q_pos0, topk=topk, visible=visible),
        grid=(batch, nq),
        in_specs=[pl.BlockSpec((tq, D_MODEL), qrow), pl.BlockSpec((tq, IDX_HEADS * IDX_DIM), qrow),
                  pl.BlockSpec((tq, LANES), qrow), pl.BlockSpec((s_keys, IDX_DIM), krow),
                  pl.BlockSpec((s_keys, KV_DIM), krow), pl.BlockSpec((s_keys, KV_DIM), krow)],
        out_specs=pl.BlockSpec((tq, D_MODEL), qrow),
        out_shape=jax.ShapeDtypeStruct((batch * t, D_MODEL), BF16),
        scratch_shapes=[pltpu.VMEM((s_keys, IDX_CAT), BF16), pltpu.VMEM((s_keys, KV_DIM), BF16),
                        pltpu.VMEM((KV_HEADS, n_kb, DSA_OT_ROWS, kb_rows), BF16),
                        pltpu.VMEM((kb_rows, kb_rows), BF16),
                        pltpu.VMEM((n_kb, kb_rows, tq), jnp.int32), pltpu.VMEM((n_kb, kb_rows, tq), F32),
                        pltpu.VMEM((n_kb, KV_HEADS, kb_rows, KV_REP * tq), F32)],
        compiler_params=_params(2),
        name="dsa",
    )(qb, qi, dtwi, ki_all, k_all, v_all)


def _band_body(*refs, tq, sub, n_kblk, q_pos0, k_min, clamped):
    q_ref = refs[0]
    k_refs = refs[1:1 + n_kblk]
    v_refs = refs[1 + n_kblk:1 + 2 * n_kblk]
    vec_ref, o_ref, bias_scr = refs[1 + 2 * n_kblk:]
    w = sum(r.shape[0] for r in k_refs)
    wsub = w - tq + sub
    i = pl.program_id(1)

    @pl.when((pl.program_id(0) == 0) & (i == 0))
    def _():
        r = lax.broadcasted_iota(jnp.int32, (sub, wsub), 0)
        c = lax.broadcasted_iota(jnp.int32, (sub, wsub), 1)
        dchunk = (r >> 6) + (wsub - sub) // CHUNK - (c >> 6)
        band_mask = jnp.where((dchunk >= 0) & (dchunk <= LEFT_CHUNKS), 0.0, NEG_BIG)
        for h in range(BAND_HEADS):
            rows = jnp.broadcast_to(vec_ref[h:h + 1, :], (sub, vec_ref.shape[1]))
            toeplitz = pltpu.roll(rows, 0, 1, stride=1, stride_axis=0)[:, :wsub]
            bias_scr[h // 2, (h % 2) * sub:(h % 2 + 1) * sub, :] = toeplitz * LOG2E + band_mask

    def window(blocks, lo, lanes):
        parts, r0 = [], 0
        for blk in blocks:
            a, b = max(lo, r0), min(lo + wsub, r0 + blk.shape[0])
            if a < b:
                parts.append(blk[a - r0:b - r0, lanes])
            r0 += blk.shape[0]
        return parts[0] if len(parts) == 1 else jnp.concatenate(parts, axis=0)

    lane = lax.broadcasted_iota(jnp.int32, (sub, LANES), 1)
    first_head = lane < BAND_HEAD_DIM
    keep_a = jnp.where(first_head, 1.0, 0.0).astype(BF16)
    keep_b = jnp.where(first_head, 0.0, 1.0).astype(BF16)
    ones = jnp.ones((wsub, LANES), BF16)
    chains = [(hp, c2) for hp in range(BAND_HEADS // 2) for c2 in range(tq // sub)]

    def scores(hp, c2, mask_missing_keys):
        lanes = slice(hp * LANES, (hp + 1) * LANES)
        qp = q_ref[c2 * sub:(c2 + 1) * sub, lanes]
        q2 = jnp.concatenate([qp * keep_a, qp * keep_b], axis=0)
        s = _mm_nt(q2, window(k_refs, c2 * sub, lanes)) + bias_scr[hp]
        if mask_missing_keys:
            kpos = q_pos0 + i * tq + c2 * sub + (sub - wsub) + lax.broadcasted_iota(jnp.int32, (1, wsub), 1)
            s = s + jnp.where(kpos >= k_min, 0.0, NEG_BIG)
        return s

    def attend(hp, c2, s):
        lanes = slice(hp * LANES, (hp + 1) * LANES)
        p = jnp.exp2(s - jnp.max(s, axis=1, keepdims=True)).astype(BF16)
        o = _mm(p, jnp.concatenate([window(v_refs, c2 * sub, lanes), ones], axis=1))
        oa = o[:sub, :LANES] / o[:sub, LANES:LANES + 1]
        ob = o[sub:, :LANES] / o[sub:, LANES:LANES + 1]
        o_ref[c2 * sub:(c2 + 1) * sub, lanes] = jnp.where(first_head, oa, ob).astype(o_ref.dtype)

    def heads(mask_missing_keys):
        ahead, pending = 2, {}
        for n in range(len(chains) + ahead):
            if n < len(chains):
                pending[n] = scores(*chains[n], mask_missing_keys)
            if n >= ahead:
                attend(*chains[n - ahead], pending.pop(n - ahead))

    if clamped:
        first_full = -(-(w - tq) // tq)
        pl.when(i < first_full)(functools.partial(heads, True))
        pl.when(i >= first_full)(functools.partial(heads, False))
    else:
        heads(False)


def _band_bias_vec(rel_bias, sub, wsub):
    l = -(-(wsub + sub) // LANES) * LANES
    m = np.arange(l)
    d = np.where(m < wsub, m, m - l)
    rel = np.clip(wsub - sub - d, -REL_CLIP, REL_CLIP) + REL_CLIP
    return rel_bias[jnp.asarray(rel)].T


def _band(q, k, v, rel_bias, batch, t, tq, sub, k_block_rows, n_kblk, q_pos0, k_min, clamped):
    nq = t // tq
    w = n_kblk * k_block_rows
    wsub = w - tq + sub
    assert wsub % LANES == 0 and (wsub - sub) % CHUNK == 0
    qrow = lambda b, i: (b * nq + i, 0)

    def krow(off):
        if clamped:
            return lambda b, i: (b * nq + jnp.maximum(i - (n_kblk - 1) + off, 0), 0)
        return lambda b, i: (b * n_kblk + off, 0)

    vec = _band_bias_vec(rel_bias, sub, wsub)
    kspecs = [pl.BlockSpec((k_block_rows, D_MODEL), krow(o)) for o in range(n_kblk)]
    return pl.pallas_call(
        functools.partial(_band_body, tq=tq, sub=sub, n_kblk=n_kblk, q_pos0=q_pos0, k_min=k_min, clamped=clamped),
        grid=(batch, nq),
        in_specs=[pl.BlockSpec((tq, D_MODEL), qrow)] + kspecs + kspecs + [_resident(vec.shape)],
        out_specs=pl.BlockSpec((tq, D_MODEL), qrow),
        out_shape=jax.ShapeDtypeStruct((batch * t, D_MODEL), BF16),
        scratch_shapes=[pltpu.VMEM((BAND_HEADS // 2, 2 * sub, wsub), F32)],
        compiler_params=_params(2),
        name="band",
    )(q, *([k] * n_kblk), *([v] * n_kblk), vec)


def _merge_body(x_ref, ya_ref, yb_ref, yc_ref, g_ref, wg_ref, bg_ref, wbr_ref, wo_ref, o_ref):
    x = x_ref[...]
    u = _rms(x, g_ref[2:3, :]).astype(BF16)
    mix = jnp.zeros(x.shape, F32)
    for k, y_ref in enumerate((ya_ref, yb_ref, yc_ref)):
        sl = slice(k * D_MODEL, (k + 1) * D_MODEL)
        gate = _sigmoid(_mm(u, wg_ref[:, sl]) + bg_ref[:, sl])
        mix = mix + gate * _mm(y_ref[...], wbr_ref[k])
    o_ref[...] = x + _rms(_mm(mix.astype(BF16), wo_ref[...]), g_ref[3:4, :])


def _merge(x, ya, yb, yc, pw, layer):
    n = x.shape[0]
    tm = _row_tile(n, 256)
    row = lambda i: (i, 0)
    tile = pl.BlockSpec((tm, D_MODEL), row)
    return pl.pallas_call(
        _merge_body,
        grid=(n // tm,),
        in_specs=[tile, tile, tile, tile]
        + [_resident_slice(pw[k], (layer,)) for k in ("g", "wg", "bg", "wbr", "wo")],
        out_specs=tile,
        out_shape=jax.ShapeDtypeStruct((n, D_MODEL), F32),
        compiler_params=_params(1),
        name="merge",
    )(x, ya, yb, yc, pw["g"], pw["wg"], pw["bg"], pw["wbr"], pw["wo"])


def _ple_body(x_ref, p_ref, g_ref, wp_ref, wpg_ref, o_ref):
    x = x_ref[...]
    e = _mm(p_ref[...].astype(BF16), wp_ref[...])
    pg = _sigmoid(_mm(_rms(x, g_ref[6:7, :]).astype(BF16), wpg_ref[...]))
    o_ref[...] = x + _rms(pg * e, g_ref[7:8, :])


def _ple(x, p, pw, layer):
    n = x.shape[0]
    tm = _row_tile(n, 512)
    row = lambda i: (i, 0)
    return pl.pallas_call(
        _ple_body,
        grid=(n // tm,),
        in_specs=[pl.BlockSpec((tm, D_MODEL), row), pl.BlockSpec((tm, PLE_DIM), row)]
        + [_resident_slice(pw[k], (layer,)) for k in ("g", "wp", "wpg")],
        out_specs=pl.BlockSpec((tm, D_MODEL), row),
        out_shape=jax.ShapeDtypeStruct((n, D_MODEL), F32),
        compiler_params=_params(1),
        name="ple",
    )(x, p, pw["g"], pw["wp"], pw["wpg"])


BAND_TQ = 256
BAND_SUB = 128
BAND_KBLK = 1 + -(-BAND // BAND_TQ)


def _prep_weights(norm_g, ffn_w13, ffn_w2, w_in, w_gate, b_gate, w_branch, w_out, w_ple, w_ple_gate):
    bf = lambda a: a.astype(BF16)
    return {"g": norm_g, "w13": bf(ffn_w13), "w2": bf(ffn_w2), "w_in": _pack_w_in(w_in), "wg": bf(w_gate),
            "bg": b_gate.reshape(DEPTH, 1, N_BRANCH * D_MODEL), "wbr": bf(w_branch), "wo": bf(w_out),
            "wp": bf(w_ple), "wpg": bf(w_ple_gate)}


def _trunk_layer(x, p, w, pw, layer, cache, batch, t):
    x = _ffn(x, pw, layer, 0, 0, 1)
    pr = _inproj(x, pw, layer)
    if cache is None:
        ya, h_new = _mamba(pr["z"], pr["xbc"], pr["dtwi"], None, None, w, batch, t)
        yb = _dsa(pr["qb"], pr["qi"], pr["dtwi"], pr["ki"], pr["kb"], pr["vb"], batch, t, t, t, 0)
        yc = _band(pr["qc"], pr["kc_bf"], pr["vc_bf"], w["rel_bias"], batch, t, BAND_TQ, BAND_SUB, BAND_TQ,
                   BAND_KBLK, 0, 0, True)
        conv_src = pr["xbc"].reshape(batch, t, CONV_CH)
    else:
        past = cache["dsa_k"].shape[1]
        conv0 = jnp.pad(cache["conv"], ((0, 0), (CONV_PAD - (CONV_W - 1), 0), (0, 0)))
        h0 = cache["ssm"].reshape(batch, D_INNER, D_STATE)
        ya, h_new = _mamba(pr["z"], pr["xbc"], pr["dtwi"], conv0, h0, w, batch, t)

        n_valid = past + t
        s_keys = -(-n_valid // DSA_KB) * DSA_KB
        tq_pad = -(-t // DSA_TQ) * DSA_TQ

        def with_cache(c, new):
            width = new.shape[-1]
            a = jnp.concatenate([c.reshape(batch, past, width), new.reshape(batch, t, width)], axis=1)
            return jnp.pad(a, ((0, 0), (0, s_keys - n_valid), (0, 0))).reshape(batch * s_keys, width)

        def pad_q(a):
            a = jnp.pad(a.reshape(batch, t, a.shape[-1]), ((0, 0), (0, tq_pad - t), (0, 0)))
            return a.reshape(batch * tq_pad, a.shape[-1])

        yb = _dsa(pad_q(pr["qb"]), pad_q(pr["qi"]), pad_q(pr["dtwi"]), with_cache(cache["idx_k"], pr["ki"]),
                  with_cache(cache["dsa_k"], pr["kb"]), with_cache(cache["dsa_v"], pr["vb"]),
                  batch, tq_pad, s_keys, n_valid, past)
        yb = yb.reshape(batch, tq_pad, D_MODEL)[:, :t].reshape(batch * t, D_MODEL)

        nrows = cache["band_k"].shape[1]
        k_rows = -(-(nrows + t) // LANES) * LANES
        lead = k_rows - nrows - t

        def with_band(c, new):
            a = jnp.concatenate([c.reshape(batch, nrows, D_MODEL).astype(BF16), new.reshape(batch, t, D_MODEL)],
                                axis=1)
            return jnp.pad(a, ((0, 0), (lead, 0), (0, 0))).reshape(batch * k_rows, D_MODEL)

        yc = _band(pr["qc"], with_band(cache["band_k"], pr["kc_bf"]), with_band(cache["band_v"], pr["vc_bf"]),
                   w["rel_bias"], batch, t, t, t, k_rows, 1, past, past - nrows, False)
        conv_src = jnp.concatenate([cache["conv"], pr["xbc"].reshape(batch, t, CONV_CH)], axis=1)

    x = _merge(x, ya, yb, yc, pw, layer)
    x = _ffn(x, pw, layer, 1, 4, 5)
    x = _ple(x, p.reshape(batch * t, PLE_DIM), pw, layer)

    band_rows = min(BAND, t)
    kc3 = pr["kc"].reshape(batch, t, BAND_HEADS, BAND_HEAD_DIM)
    vc3 = pr["vc"].reshape(batch, t, BAND_HEADS, BAND_HEAD_DIM)
    state = (pr["kb"].reshape(batch, t, KV_HEADS, DSA_HEAD_DIM), pr["vb"].reshape(batch, t, KV_HEADS, DSA_HEAD_DIM),
             pr["ki"].reshape(batch, t, IDX_DIM), kc3[:, t - band_rows:], vc3[:, t - band_rows:],
             h_new.reshape(batch, SSM_HEADS, SSM_HEAD_DIM, D_STATE), conv_src[:, -(CONV_W - 1):])
    return x, state


def kernel(x_prompt, x_sample, p_prompt, p_sample, cache_dsa_k, cache_dsa_v, cache_idx_k, cache_band_k,
           cache_band_v, state_ssm, state_conv, norm_g, ffn_w13, ffn_w2, w_in, conv_w, conv_b, dt_bias,
           a_log, d_skip, ssm_norm_g, rel_bias, w_gate, b_gate, w_branch, w_out, w_ple, w_ple_gate):
    bp, tp, _ = x_prompt.shape
    bs, ts, _ = x_sample.shape
    yp = x_prompt.reshape(bp * tp, D_MODEL)
    ys = x_sample.reshape(bs * ts, D_MODEL)
    st_p, st_s = [], []
    pw = _prep_weights(norm_g, ffn_w13, ffn_w2, w_in, w_gate, b_gate, w_branch, w_out, w_ple, w_ple_gate)
    for i in range(DEPTH):
        w = {"conv_w": conv_w[i], "conv_b": conv_b[i], "dt_bias": dt_bias[i], "a_log": a_log[i],
             "d_skip": d_skip[i], "ssm_norm_g": ssm_norm_g[i], "rel_bias": rel_bias[i]}
        yp, sp = _trunk_layer(yp, p_prompt[i], w, pw, i, None, bp, tp)
        st_p.append(sp)
        cache = {"dsa_k": cache_dsa_k[i], "dsa_v": cache_dsa_v[i], "idx_k": cache_idx_k[i],
                 "band_k": cache_band_k[i], "band_v": cache_band_v[i], "ssm": state_ssm[i], "conv": state_conv[i]}
        ys, ss = _trunk_layer(ys, p_sample[i], w, pw, i, cache, bs, ts)
        st_s.append(ss)
    outs_p = [jnp.stack(c) for c in zip(*st_p)]
    outs_s = [jnp.stack(c) for c in zip(*st_s)]
    return (yp.reshape(bp, tp, D_MODEL), ys.reshape(bs, ts, D_MODEL), *outs_p, *outs_s)
```

```python
import functools
import math

import numpy as np
import jax
import jax.numpy as jnp
from jax import lax
from jax.experimental import pallas as pl
from jax.experimental.pallas import tpu as pltpu

F32 = jnp.float32
BF16 = jnp.bfloat16

D_MODEL = 1024
DEPTH = 2
CHUNK = 64
EPS = 1e-6
HALF = 0.5
D_FF = 2816
PLE_DIM = 256
SSM_HEAD_DIM = 64
D_INNER = D_MODEL
SSM_HEADS = D_INNER // SSM_HEAD_DIM
N_GROUPS = 4
HEADS_PER_GROUP = SSM_HEADS // N_GROUPS
D_STATE = 128
CONV_W = 4
CONV_CH = D_INNER + 2 * N_GROUPS * D_STATE
DSA_HEAD_DIM = 128
DSA_HEADS = D_MODEL // DSA_HEAD_DIM
KV_HEADS = 2
KV_REP = DSA_HEADS // KV_HEADS
IDX_HEADS = 4
IDX_DIM = 64
TOPK_MAX = 256
BAND_HEAD_DIM = 64
BAND_HEADS = D_MODEL // BAND_HEAD_DIM
LEFT_CHUNKS = 8
BAND = LEFT_CHUNKS * CHUNK
REL_CLIP = 256
N_BRANCH = 3
IN_WIDTHS = (D_INNER, CONV_CH, SSM_HEADS,
             DSA_HEADS * DSA_HEAD_DIM, KV_HEADS * DSA_HEAD_DIM, KV_HEADS * DSA_HEAD_DIM,
             IDX_HEADS * IDX_DIM, IDX_DIM, IDX_HEADS,
             BAND_HEADS * BAND_HEAD_DIM, BAND_HEADS * BAND_HEAD_DIM, BAND_HEADS * BAND_HEAD_DIM)
IN_SPLITS = tuple(int(s) for s in np.cumsum(IN_WIDTHS)[:-1])

LANES = 128
SUBLANES = 8
KV_DIM = KV_HEADS * DSA_HEAD_DIM
GROUP_CH = D_INNER // N_GROUPS
NEG_BIG = -1e30
INT_MIN = -2 ** 31
LOG2E = math.log2(math.e)
VMEM_LIMIT = 56 * 1024 * 1024


def _mm(a, b):
    return jnp.dot(a, b, preferred_element_type=F32)


def _mm_nt(a, b):
    return lax.dot_general(a, b, (((1,), (1,)), ((), ())), preferred_element_type=F32)


def _mm_tn(a, b):
    return lax.dot_general(a, b, (((0,), (0,)), ((), ())), preferred_element_type=F32)


def _rms(x, g):
    return x * lax.rsqrt(jnp.mean(x * x, axis=-1, keepdims=True) + EPS) * g


def _sigmoid(x):
    return 1.0 / (1.0 + jnp.exp(-x))


def _silu(x):
    return x * _sigmoid(x)


def _resident(shape):
    return pl.BlockSpec(shape, lambda *_: (0,) * len(shape), pipeline_mode=pl.Buffered(1))


def _resident_slice(arr, lead, block=None, at=None):
    tail = tuple(arr.shape[len(lead):]) if block is None else tuple(block)
    idx = tuple(lead) + ((0,) * len(tail) if at is None else tuple(at))
    return pl.BlockSpec((None,) * len(lead) + tail, lambda *_: idx, pipeline_mode=pl.Buffered(1))


def _params(n_grid_dims):
    return pltpu.CompilerParams(dimension_semantics=("arbitrary",) * n_grid_dims,
                                vmem_limit_bytes=VMEM_LIMIT)


def _row_tile(n_rows, want):
    t = min(want, n_rows)
    assert n_rows % t == 0
    return t


def _fold_rows(x, op, tile_rows=SUBLANES):
    parts = [x[i * tile_rows:(i + 1) * tile_rows] for i in range(x.shape[0] // tile_rows)]
    while len(parts) > 1:
        parts = [op(parts[i], parts[i + 1]) for i in range(0, len(parts) - 1, 2)] + parts[len(parts) & ~1:]
    return parts[0]


FF_CHUNK = 256


def _ffn_body(x_ref, g_ref, wa_ref, wb_ref, w2_ref, o_ref, *, g_pre, g_post):
    x = x_ref[...]
    u = _rms(x, g_ref[g_pre:g_pre + 1, :]).astype(BF16)
    acc = jnp.zeros(x.shape, F32)
    for c in range(D_FF // FF_CHUNK):
        sl = slice(c * FF_CHUNK, (c + 1) * FF_CHUNK)
        a = _mm(u, wa_ref[:, sl])
        b = _mm(u, wb_ref[:, sl])
        acc = acc + _mm((_silu(a) * b).astype(BF16), w2_ref[sl, :])
    o_ref[...] = x + HALF * _rms(acc, g_ref[g_post:g_post + 1, :])


def _ffn(x, pw, layer, j, g_pre, g_post):
    n = x.shape[0]
    tm = _row_tile(n, 512)
    row = lambda i: (i, 0)
    half = (D_MODEL, D_FF)
    return pl.pallas_call(
        functools.partial(_ffn_body, g_pre=g_pre, g_post=g_post),
        grid=(n // tm,),
        in_specs=[pl.BlockSpec((tm, D_MODEL), row), _resident_slice(pw["g"], (layer,)),
                  _resident_slice(pw["w13"], (layer, j), half, (0, 0)),
                  _resident_slice(pw["w13"], (layer, j), half, (0, 1)),
                  _resident_slice(pw["w2"], (layer, j))],
        out_specs=pl.BlockSpec((tm, D_MODEL), row),
        out_shape=jax.ShapeDtypeStruct((n, D_MODEL), F32),
        compiler_params=_params(1),
        name="ffn",
    )(x, pw["g"], pw["w13"], pw["w13"], pw["w2"])


_INPROJ_GROUPS = (D_INNER, CONV_CH, D_MODEL, KV_DIM, KV_DIM, IDX_HEADS * IDX_DIM, D_MODEL, D_MODEL, D_MODEL,
                  LANES, LANES)
_INPROJ_OUT = (
    ("z", 0, D_INNER, F32, None, "layer"),
    ("xbc", 1, CONV_CH, F32, None, "layer"),
    ("qb", 2, D_MODEL, BF16, DSA_HEAD_DIM ** -0.5 * LOG2E, "layer"),
    ("kb", 3, KV_DIM, F32, None, "stack"),
    ("vb", 4, KV_DIM, F32, None, "stack"),
    ("qi", 5, IDX_HEADS * IDX_DIM, F32, None, "layer"),
    ("qc", 6, D_MODEL, BF16, BAND_HEAD_DIM ** -0.5 * LOG2E, "layer"),
    ("kc", 7, D_MODEL, F32, None, "tail"),
    ("kc_bf", 7, D_MODEL, BF16, None, "layer"),
    ("vc", 8, D_MODEL, F32, None, "tail"),
    ("vc_bf", 8, D_MODEL, BF16, None, "layer"),
    ("ki", 9, IDX_DIM, F32, None, "stack"),
    ("dtwi", 10, LANES, F32, None, "layer"),
)
WI_LANE = SSM_HEADS


def _pack_w_in(w_in):
    z, xbc, dt, qb, kb, vb, qi, ki, wi, qc, kc, vc = jnp.split(w_in, IN_SPLITS, axis=-1)
    pad = lambda w: jnp.pad(w, ((0, 0), (0, 0), (0, LANES - w.shape[-1])))
    cols = [z, xbc, qb, kb, vb, qi, qc, kc, vc, pad(ki), pad(jnp.concatenate([dt, wi], axis=-1))]
    return jnp.concatenate(cols, axis=-1).astype(BF16)


def _inproj_body(x_ref, g_ref, w_ref, *refs):
    out_refs = refs[len(refs) - len(_INPROJ_OUT):]
    u = _rms(x_ref[...], g_ref[2:3, :]).astype(BF16)
    starts = np.concatenate([[0], np.cumsum(_INPROJ_GROUPS)])
    for grp, width in enumerate(_INPROJ_GROUPS):
        r = _mm(u, w_ref[:, int(starts[grp]):int(starts[grp]) + width])
        for (_, og, stored, dtype, scale, _), o_ref in zip(_INPROJ_OUT, out_refs):
            if og == grp:
                v = r if scale is None else r * scale
                o_ref[...] = (v if stored == width else v[:, :stored]).astype(dtype)


def _inproj(x, pw, layer, batch, t, shared):
    n = x.shape[0]
    tm = _row_tile(n, 256)
    tail = min(BAND, t)
    row = lambda i: (i, 0)
    stack_row = lambda i: (layer, i, 0)
    if tail == t:
        tail_row = stack_row
    else:
        assert t % tm == 0 and tail % tm == 0
        per_seq, per_tail = t // tm, tail // tm
        tail_row = lambda i: (layer, (i // per_seq) * per_tail + jnp.maximum(i % per_seq - (per_seq - per_tail), 0), 0)
    out_specs, out_shape, stacked = [], [], []
    for k, (name, _, width, dtype, _, kind) in enumerate(_INPROJ_OUT):
        if kind == "layer":
            out_specs.append(pl.BlockSpec((tm, width), row))
            out_shape.append(jax.ShapeDtypeStruct((n, width), dtype))
        else:
            rows = n if kind == "stack" else batch * tail
            out_specs.append(pl.BlockSpec((None, tm, width), stack_row if kind == "stack" else tail_row))
            out_shape.append(jax.ShapeDtypeStruct((DEPTH, rows, width), dtype))
            stacked.append((name, k))
    carried = [] if shared is None else [shared[name] for name, _ in stacked]
    aliases = {} if shared is None else {3 + a: k for a, (_, k) in enumerate(stacked)}
    outs = pl.pallas_call(
        _inproj_body,
        grid=(n // tm,),
        in_specs=[pl.BlockSpec((tm, D_MODEL), row), _resident_slice(pw["g"], (layer,)),
                  _resident_slice(pw["w_in"], (layer,))] + [pl.BlockSpec(memory_space=pl.ANY)] * len(carried),
        out_specs=out_specs,
        out_shape=out_shape,
        input_output_aliases=aliases,
        compiler_params=_params(1),
        name="inproj",
    )(x, pw["g"], pw["w_in"], *carried)
    pr = {o[0]: a for o, a in zip(_INPROJ_OUT, outs)}
    return pr, {name: pr[name] for name, _ in stacked}


CONV_PAD = 8


def _split3(x):
    hi = x.astype(BF16)
    r = x - hi.astype(F32)
    mid = r.astype(BF16)
    lo = (r - mid.astype(F32)).astype(BF16)
    return hi, mid, lo


def _expand_heads(x, e):
    hi, mid, lo = _split3(x)
    return _mm(hi, e) + _mm(mid, e) + _mm(lo, e)


def _cumsum_rows(x):
    n = x.shape[0]
    row = lax.broadcasted_iota(jnp.int32, x.shape, 0)
    d = 1
    while d < n:
        x = x + jnp.where(row >= d, pltpu.roll(x, d, 0), 0.0)
        d *= 2
    return x


def _mamba_body(*refs, has_state, n_chunks):
    if has_state:
        (z_ref, xbc_ref, dtwi_ref, conv0_ref, h0_ref, cw_ref, cb_ref, dtb_ref, alog_ref, dskip_ref,
         ng_ref, e_ref, y_ref, hout_ref, xp_scr, ht_scr) = refs
    else:
        (z_ref, xbc_ref, dtwi_ref, cw_ref, cb_ref, dtb_ref, alog_ref, dskip_ref,
         ng_ref, e_ref, y_ref, hout_ref, xp_scr, ht_scr) = refs
    c = pl.program_id(1)
    q = CHUNK

    @pl.when(c == 0)
    def _():
        if has_state:
            xp_scr[0:CONV_PAD, :] = conv0_ref[...]
            ht_scr[...] = h0_ref[...].T
        else:
            xp_scr[0:CONV_PAD, :] = jnp.zeros((CONV_PAD, CONV_CH), F32)
            ht_scr[...] = jnp.zeros(ht_scr.shape, F32)

    @pl.when(c > 0)
    def _():
        xp_scr[0:CONV_PAD, :] = xp_scr[q:q + CONV_PAD, :]

    xp_scr[CONV_PAD:CONV_PAD + q, :] = xbc_ref[...]
    acc = cb_ref[...]
    for k in range(CONV_W):
        r0 = CONV_PAD - (CONV_W - 1) + k
        acc = acc + xp_scr[r0:r0 + q, :] * cw_ref[k:k + 1, :]
    xc = _silu(acc)
    xs = xc[:, :D_INNER]
    bm = xc[:, D_INNER:D_INNER + N_GROUPS * D_STATE]
    cm = xc[:, D_INNER + N_GROUPS * D_STATE:]

    lane = lax.broadcasted_iota(jnp.int32, (q, LANES), 1)
    pre = dtwi_ref[...] + dtb_ref[...]
    dt = jnp.maximum(pre, 0.0) + jnp.log1p(jnp.exp(-jnp.abs(pre)))
    dt = jnp.where(lane < SSM_HEADS, dt, 0.0)
    dta = dt * (-jnp.exp(alog_ref[...]))
    cum = _cumsum_rows(dta)

    e = e_ref[...]
    ecol = _expand_heads(cum, e)
    dtx = _expand_heads(dt, e)
    li = lax.broadcasted_iota(jnp.int32, (q, D_INNER), 0)
    si = lax.broadcasted_iota(jnp.int32, (q, D_INNER), 1) & (q - 1)
    erow = jnp.sum(jnp.where(li == si, ecol, 0.0), axis=0, keepdims=True)
    elast = ecol[q - 1:q, :]
    causal = (li >= si)[:, :GROUP_CH]
    xdt = xs * dtx
    xdec = (xdt * jnp.exp(elast - ecol)).astype(BF16)
    exp_e = jnp.exp(ecol)
    chunk_decay = jnp.exp(elast)
    bdr = lax.broadcasted_iota(jnp.int32, (GROUP_CH, GROUP_CH), 0) // SSM_HEAD_DIM
    bdc = lax.broadcasted_iota(jnp.int32, (GROUP_CH, GROUP_CH), 1) // SSM_HEAD_DIM
    block_diag = bdr == bdc

    ys = []
    for g in range(N_GROUPS):
        sl = slice(g * GROUP_CH, (g + 1) * GROUP_CH)
        nl = slice(g * D_STATE, (g + 1) * D_STATE)
        bg = bm[:, nl].astype(BF16)
        cg = cm[:, nl].astype(BF16)
        cb = _mm_nt(cg, bg)
        cbt = jnp.concatenate([cb] * HEADS_PER_GROUP, axis=1)
        decay = jnp.exp(jnp.where(causal, ecol[:, sl] - erow[:, sl], NEG_BIG))
        m = (cbt * decay).astype(BF16)
        xg = xdt[:, sl]
        bd = jnp.where(block_diag, jnp.concatenate([xg] * HEADS_PER_GROUP, axis=0), 0.0).astype(BF16)
        y_diag = _mm(m, bd)
        ht_g = ht_scr[:, sl]
        y_off = _mm(cg, ht_g.astype(BF16)) * exp_e[:, sl]
        ys.append(y_diag + y_off)
        ht_scr[:, sl] = ht_g * chunk_decay[:, sl] + _mm_tn(bg, xdec[:, sl])
    y = jnp.concatenate(ys, axis=1) + dskip_ref[...] * xs
    y = y * _silu(z_ref[...])
    outs = []
    for g in range(N_GROUPS):
        yg = y[:, g * GROUP_CH:(g + 1) * GROUP_CH]
        outs.append(yg * lax.rsqrt(jnp.mean(yg * yg, axis=-1, keepdims=True) + EPS))
    y_ref[...] = (jnp.concatenate(outs, axis=1) * ng_ref[...]).astype(y_ref.dtype)

    @pl.when(c == n_chunks - 1)
    def _():
        hout_ref[...] = ht_scr[...].T


def _head_expand_matrix():
    e = np.zeros((LANES, D_INNER), np.float32)
    for h in range(SSM_HEADS):
        e[h, h * SSM_HEAD_DIM:(h + 1) * SSM_HEAD_DIM] = 1.0
    return jnp.asarray(e, BF16)


def _mamba(z, xbc, dtwi, conv0, h0, lw, batch, t):
    nc = t // CHUNK
    has_state = h0 is not None
    row = lambda b, c: (b * nc + c, 0)
    per_b = lambda b, c: (b, 0, 0)
    pad16 = lambda v: jnp.pad(v.reshape(1, SSM_HEADS), ((0, 0), (0, LANES - SSM_HEADS)))
    small = [lw["conv_w"], lw["conv_b"].reshape(1, CONV_CH), pad16(lw["dt_bias"]), pad16(lw["a_log"]),
             jnp.repeat(lw["d_skip"], SSM_HEAD_DIM).reshape(1, D_INNER),
             lw["ssm_norm_g"].reshape(1, D_INNER), _head_expand_matrix()]
    ins = [z, xbc, dtwi]
    in_specs = [pl.BlockSpec((CHUNK, D_INNER), row), pl.BlockSpec((CHUNK, CONV_CH), row),
                pl.BlockSpec((CHUNK, LANES), row)]
    if has_state:
        ins += [conv0, h0]
        in_specs += [pl.BlockSpec((None, CONV_PAD, CONV_CH), per_b),
                     pl.BlockSpec((None, D_INNER, D_STATE), per_b)]
    ins += small
    in_specs += [_resident(a.shape) for a in small]
    y, h_out = pl.pallas_call(
        functools.partial(_mamba_body, has_state=has_state, n_chunks=nc),
        grid=(batch, nc),
        in_specs=in_specs,
        out_specs=[pl.BlockSpec((CHUNK, D_INNER), row), pl.BlockSpec((None, D_INNER, D_STATE), per_b)],
        out_shape=[jax.ShapeDtypeStruct((batch * t, D_INNER), BF16),
                   jax.ShapeDtypeStruct((batch, D_INNER, D_STATE), F32)],
        scratch_shapes=[pltpu.VMEM((CONV_PAD + CHUNK, CONV_CH), F32), pltpu.VMEM((D_STATE, D_INNER), F32)],
        compiler_params=_params(2),
        name="mamba",
    )(*ins)
    return y, h_out


DSA_TQ = LANES
DSA_KB = 512
IDX_CAT = 4 * IDX_DIM
DSA_OT_ROWS = 2 * DSA_HEAD_DIM
HALF_OFFSET = 1 << 15
PACKED_ROWS = 2 * SUBLANES


def _hi_lo(x):
    hi = x.astype(BF16).astype(F32)
    return hi, (x - hi).astype(BF16).astype(F32)


def _dsa_body(qb_ref, qi_ref, dtwi_ref, ki_ref, k_ref, v_ref, o_ref,
              kcat_scr, kbf_scr, vt_scr, tri_scr, key_scr, hi_scr, lo_scr, mask_scr, s_scr,
              *, n_kb, n_valid, q_pos0, topk, visible):
    tq, kb_rows = DSA_TQ, DSA_KB
    j = pl.program_id(1)

    @pl.when(j == 0)
    def _():
        hi, lo = _hi_lo(ki_ref[...])
        kcat_scr[...] = jnp.concatenate([hi, lo, hi, jnp.zeros_like(hi)], axis=1).astype(BF16)
        kbf_scr[...] = k_ref[...].astype(BF16)
        ones = jnp.ones((DSA_HEAD_DIM, kb_rows), F32)
        for kb in range(n_kb):
            vt = v_ref[kb * kb_rows:(kb + 1) * kb_rows, :].T
            for g in range(KV_HEADS):
                vt_g = vt[g * DSA_HEAD_DIM:(g + 1) * DSA_HEAD_DIM, :]
                vt_scr[g, kb] = jnp.concatenate([vt_g, ones], axis=0).astype(BF16)
        tr = lax.broadcasted_iota(jnp.int32, (kb_rows, kb_rows), 0)
        tc = lax.broadcasted_iota(jnp.int32, (kb_rows, kb_rows), 1)
        tri_scr[...] = jnp.where(tc < tr, 1.0, 0.0).astype(BF16)

    qpos = q_pos0 + j * tq + lax.broadcasted_iota(jnp.int32, (1, tq), 1)
    q_end = jnp.minimum(((qpos >> 6) + 1) << 6, n_valid)
    k_eff = jnp.minimum(q_end, topk).astype(F32)
    last_end = jnp.minimum((((q_pos0 + (j + 1) * tq - 1) >> 6) + 1) << 6, n_valid)
    nkb = (last_end + (kb_rows - 1)) // kb_rows

    wit = (dtwi_ref[...] * (IDX_DIM ** -0.5 * IDX_HEADS ** -0.5)).T
    qi = qi_ref[...]
    qparts = []
    for h in range(IDX_HEADS):
        hi, lo = _hi_lo(qi[:, h * IDX_DIM:(h + 1) * IDX_DIM])
        qparts.append(jnp.concatenate([hi, hi, lo, jnp.zeros_like(hi)], axis=1))
    qcat = jnp.concatenate(qparts, axis=0).astype(BF16)
    krow = lax.broadcasted_iota(jnp.int32, (kb_rows, 1), 0)
    qb = qb_ref[...]
    q4 = [jnp.concatenate([qb[:, (g * KV_REP + r) * DSA_HEAD_DIM:(g * KV_REP + r + 1) * DSA_HEAD_DIM]
                           for r in range(KV_REP)], axis=0) for g in range(KV_HEADS)]

    def run(n_vis):
        for kb in range(n_vis):
            logit = _mm_nt(kcat_scr[kb * kb_rows:(kb + 1) * kb_rows, :], qcat)
            sc = jnp.zeros((kb_rows, tq), F32)
            for h in range(IDX_HEADS):
                sc = sc + jnp.maximum(logit[:, h * tq:(h + 1) * tq], 0.0) * wit[WI_LANE + h:WI_LANE + h + 1, :]
            sc = jnp.where(kb * kb_rows + krow < q_end, sc, -jnp.inf)
            bits = lax.bitcast_convert_type(sc, jnp.int32)
            key = jnp.where(bits < 0, bits ^ jnp.int32(0x7FFFFFFF), bits)
            key_scr[kb] = key
            hi_scr[kb] = (key >> 16).astype(jnp.int16)
            lo_scr[kb] = ((key & 0xFFFF) - HALF_OFFSET).astype(jnp.int16)

        def count(pred):
            acc = _fold_rows(jnp.where(pred(key_scr[0]), 1.0, 0.0), jnp.add)
            for kb in range(1, n_vis):
                acc = acc + _fold_rows(jnp.where(pred(key_scr[kb]), 1.0, 0.0), jnp.add)
            return jnp.sum(acc, axis=0, keepdims=True)

        def count16(scr, pred):
            acc = None
            for kb in range(n_vis):
                hit = jnp.where(pred(scr[kb]), jnp.int16(1), jnp.int16(0))
                part = _fold_rows(hit, jnp.add, tile_rows=PACKED_ROWS)
                acc = part if acc is None else acc + part
            return jnp.sum(acc.astype(F32), axis=0, keepdims=True)

        def select16(scr, k_needed):
            def step(i, tu):
                cand = tu | lax.shift_left(jnp.int32(1), 15 - i)
                thr16 = (cand - HALF_OFFSET).astype(jnp.int16)
                return jnp.where(count16(scr, lambda v: v >= thr16) >= k_needed, cand, tu)
            return lax.fori_loop(0, 16, step, jnp.zeros((1, tq), jnp.int32)) - HALF_OFFSET

        t_hi = select16(hi_scr, k_eff)
        t_hi16 = t_hi.astype(jnp.int16)
        above = count16(hi_scr, lambda v: v > t_hi16)
        for kb in range(n_vis):
            lo_scr[kb] = jnp.where(hi_scr[kb] == t_hi16, lo_scr[kb], jnp.int16(-HALF_OFFSET))
        t_lo = select16(lo_scr, k_eff - above) + HALF_OFFSET
        thr = lax.shift_left(t_hi, 16) | t_lo

        need = k_eff - count(lambda k: k > thr)
        before = jnp.zeros((1, tq), F32)
        for kb in range(n_vis):
            key = key_scr[kb]
            eq = jnp.where(key == thr, 1.0, 0.0)
            rank = _mm(tri_scr[...], eq.astype(BF16)) + before
            take = jnp.where(key > thr, 1.0, jnp.where(rank < need, eq, 0.0))
            mask_scr[kb] = jnp.where(take > 0.0, 0.0, NEG_BIG)
            before = before + jnp.sum(_fold_rows(eq, jnp.add), axis=0, keepdims=True)

        m8 = [jnp.full((SUBLANES, KV_REP * tq), NEG_BIG, F32) for _ in range(KV_HEADS)]
        for kb in range(n_vis):
            mask4 = jnp.concatenate([mask_scr[kb]] * KV_REP, axis=1)
            for g in range(KV_HEADS):
                kg = kbf_scr[kb * kb_rows:(kb + 1) * kb_rows, g * DSA_HEAD_DIM:(g + 1) * DSA_HEAD_DIM]
                st = _mm_nt(kg, q4[g]) + mask4
                s_scr[kb, g] = st
                m8[g] = jnp.maximum(m8[g], _fold_rows(st, jnp.maximum))
        m = [jnp.max(m8[g], axis=0, keepdims=True) for g in range(KV_HEADS)]
        acc = [None] * KV_HEADS
        for kb in range(n_vis):
            for g in range(KV_HEADS):
                p = jnp.exp2(s_scr[kb, g] - m[g]).astype(BF16)
                pv = _mm(vt_scr[g, kb], p)
                acc[g] = pv if acc[g] is None else acc[g] + pv
        outs = []
        for g in range(KV_HEADS):
            ot = acc[g][0:DSA_HEAD_DIM, :] / acc[g][DSA_HEAD_DIM:DSA_HEAD_DIM + 1, :]
            outs += [ot[:, r * tq:(r + 1) * tq].T for r in range(KV_REP)]
        o_ref[...] = jnp.concatenate(outs, axis=1).astype(o_ref.dtype)

    if len(visible) == 1:
        run(visible[0])
    else:
        for n_vis in visible:
            pl.when(nkb == n_vis)(functools.partial(run, n_vis))


def _dsa(qb, qi, dtwi, ki_all, k_all, v_all, key_layer, batch, t, s_keys, n_valid, q_pos0):
    tq, kb_rows = DSA_TQ, DSA_KB
    assert t % tq == 0 and s_keys % kb_rows == 0
    nq = t // tq
    n_kb = s_keys // kb_rows
    topk = min(TOPK_MAX, n_valid // 4)
    qrow = lambda b, j: (b * nq + j, 0)
    krow = lambda b, j: (key_layer, b, 0)

    def visible_blocks(j):
        last_end = min(((((q_pos0 + (j + 1) * tq - 1) >> 6) + 1) << 6), n_valid)
        return (last_end + kb_rows - 1) // kb_rows

    visible = tuple(sorted({visible_blocks(j) for j in range(nq)}))
    return pl.pallas_call(
        functools.partial(_dsa_body, n_kb=n_kb, n_valid=n_valid, q_pos0=q_pos0, topk=topk, visible=visible),
        grid=(batch, nq),
        in_specs=[pl.BlockSpec((tq, D_MODEL), qrow), pl.BlockSpec((tq, IDX_HEADS * IDX_DIM), qrow),
                  pl.BlockSpec((tq, LANES), qrow), pl.BlockSpec((None, s_keys, IDX_DIM), krow),
                  pl.BlockSpec((None, s_keys, KV_DIM), krow), pl.BlockSpec((None, s_keys, KV_DIM), krow)],
        out_specs=pl.BlockSpec((tq, D_MODEL), qrow),
        out_shape=jax.ShapeDtypeStruct((batch * t, D_MODEL), BF16),
        scratch_shapes=[pltpu.VMEM((s_keys, IDX_CAT), BF16), pltpu.VMEM((s_keys, KV_DIM), BF16),
                        pltpu.VMEM((KV_HEADS, n_kb, DSA_OT_ROWS, kb_rows), BF16),
                        pltpu.VMEM((kb_rows, kb_rows), BF16),
                        pltpu.VMEM((n_kb, kb_rows, tq), jnp.int32), pltpu.VMEM((n_kb, kb_rows, tq), jnp.int16),
                        pltpu.VMEM((n_kb, kb_rows, tq), jnp.int16), pltpu.VMEM((n_kb, kb_rows, tq), F32),
                        pltpu.VMEM((n_kb, KV_HEADS, kb_rows, KV_REP * tq), F32)],
        compiler_params=_params(2),
        name="dsa",
    )(qb, qi, dtwi, ki_all, k_all, v_all)


def _band_body(*refs, tq, sub, n_kblk, q_pos0, k_min, clamped):
    q_ref = refs[0]
    k_refs = refs[1:1 + n_kblk]
    v_refs = refs[1 + n_kblk:1 + 2 * n_kblk]
    vec_ref, o_ref, bias_scr = refs[1 + 2 * n_kblk:]
    w = sum(r.shape[0] for r in k_refs)
    wsub = w - tq + sub
    i = pl.program_id(1)

    @pl.when((pl.program_id(0) == 0) & (i == 0))
    def _():
        r = lax.broadcasted_iota(jnp.int32, (sub, wsub), 0)
        c = lax.broadcasted_iota(jnp.int32, (sub, wsub), 1)
        dchunk = (r >> 6) + (wsub - sub) // CHUNK - (c >> 6)
        band_mask = jnp.where((dchunk >= 0) & (dchunk <= LEFT_CHUNKS), 0.0, NEG_BIG)
        for h in range(BAND_HEADS):
            rows = jnp.broadcast_to(vec_ref[h:h + 1, :], (sub, vec_ref.shape[1]))
            toeplitz = pltpu.roll(rows, 0, 1, stride=1, stride_axis=0)[:, :wsub]
            bias_scr[h // 2, (h % 2) * sub:(h % 2 + 1) * sub, :] = toeplitz * LOG2E + band_mask

    def window(blocks, lo, lanes):
        parts, r0 = [], 0
        for blk in blocks:
            a, b = max(lo, r0), min(lo + wsub, r0 + blk.shape[0])
            if a < b:
                parts.append(blk[a - r0:b - r0, lanes])
            r0 += blk.shape[0]
        return parts[0] if len(parts) == 1 else jnp.concatenate(parts, axis=0)

    lane = lax.broadcasted_iota(jnp.int32, (sub, LANES), 1)
    first_head = lane < BAND_HEAD_DIM
    keep_a = jnp.where(first_head, 1.0, 0.0).astype(BF16)
    keep_b = jnp.where(first_head, 0.0, 1.0).astype(BF16)
    ones = jnp.ones((wsub, LANES), BF16)
    chains = [(hp, c2) for hp in range(BAND_HEADS // 2) for c2 in range(tq // sub)]

    def scores(hp, c2, mask_missing_keys):
        lanes = slice(hp * LANES, (hp + 1) * LANES)
        qp = q_ref[c2 * sub:(c2 + 1) * sub, lanes]
        q2 = jnp.concatenate([qp * keep_a, qp * keep_b], axis=0)
        s = _mm_nt(q2, window(k_refs, c2 * sub, lanes)) + bias_scr[hp]
        if mask_missing_keys:
            kpos = q_pos0 + i * tq + c2 * sub + (sub - wsub) + lax.broadcasted_iota(jnp.int32, (1, wsub), 1)
            s = s + jnp.where(kpos >= k_min, 0.0, NEG_BIG)
        return s

    def attend(hp, c2, s):
        lanes = slice(hp * LANES, (hp + 1) * LANES)
        p = jnp.exp2(s - jnp.max(s, axis=1, keepdims=True)).astype(BF16)
        o = _mm(p, jnp.concatenate([window(v_refs, c2 * sub, lanes), ones], axis=1))
        oa = o[:sub, :LANES] / o[:sub, LANES:LANES + 1]
        ob = o[sub:, :LANES] / o[sub:, LANES:LANES + 1]
        o_ref[c2 * sub:(c2 + 1) * sub, lanes] = jnp.where(first_head, oa, ob).astype(o_ref.dtype)

    def heads(mask_missing_keys):
        ahead, pending = 2, {}
        for n in range(len(chains) + ahead):
            if n < len(chains):
                pending[n] = scores(*chains[n], mask_missing_keys)
            if n >= ahead:
                attend(*chains[n - ahead], pending.pop(n - ahead))

    if clamped:
        first_full = -(-(w - tq) // tq)
        pl.when(i < first_full)(functools.partial(heads, True))
        pl.when(i >= first_full)(functools.partial(heads, False))
    else:
        heads(False)


def _band_bias_vec(rel_bias, sub, wsub):
    l = -(-(wsub + sub) // LANES) * LANES
    m = np.arange(l)
    d = np.where(m < wsub, m, m - l)
    rel = np.clip(wsub - sub - d, -REL_CLIP, REL_CLIP) + REL_CLIP
    return rel_bias[jnp.asarray(rel)].T


def _band(q, k, v, rel_bias, batch, t, tq, sub, k_block_rows, n_kblk, q_pos0, k_min, clamped):
    nq = t // tq
    w = n_kblk * k_block_rows
    wsub = w - tq + sub
    assert wsub % LANES == 0 and (wsub - sub) % CHUNK == 0
    qrow = lambda b, i: (b * nq + i, 0)

    def krow(off):
        if clamped:
            return lambda b, i: (b * nq + jnp.maximum(i - (n_kblk - 1) + off, 0), 0)
        return lambda b, i: (b * n_kblk + off, 0)

    vec = _band_bias_vec(rel_bias, sub, wsub)
    kspecs = [pl.BlockSpec((k_block_rows, D_MODEL), krow(o)) for o in range(n_kblk)]
    return pl.pallas_call(
        functools.partial(_band_body, tq=tq, sub=sub, n_kblk=n_kblk, q_pos0=q_pos0, k_min=k_min, clamped=clamped),
        grid=(batch, nq),
        in_specs=[pl.BlockSpec((tq, D_MODEL), qrow)] + kspecs + kspecs + [_resident(vec.shape)],
        out_specs=pl.BlockSpec((tq, D_MODEL), qrow),
        out_shape=jax.ShapeDtypeStruct((batch * t, D_MODEL), BF16),
        scratch_shapes=[pltpu.VMEM((BAND_HEADS // 2, 2 * sub, wsub), F32)],
        compiler_params=_params(2),
        name="band",
    )(q, *([k] * n_kblk), *([v] * n_kblk), vec)


def _merge_body(x_ref, ya_ref, yb_ref, yc_ref, g_ref, wg_ref, bg_ref, wbr_ref, wo_ref, o_ref):
    x = x_ref[...]
    u = _rms(x, g_ref[2:3, :]).astype(BF16)
    mix = jnp.zeros(x.shape, F32)
    for k, y_ref in enumerate((ya_ref, yb_ref, yc_ref)):
        sl = slice(k * D_MODEL, (k + 1) * D_MODEL)
        gate = _sigmoid(_mm(u, wg_ref[:, sl]) + bg_ref[:, sl])
        mix = mix + gate * _mm(y_ref[...], wbr_ref[k])
    o_ref[...] = x + _rms(_mm(mix.astype(BF16), wo_ref[...]), g_ref[3:4, :])


def _merge(x, ya, yb, yc, pw, layer):
    n = x.shape[0]
    tm = _row_tile(n, 256)
    row = lambda i: (i, 0)
    tile = pl.BlockSpec((tm, D_MODEL), row)
    return pl.pallas_call(
        _merge_body,
        grid=(n // tm,),
        in_specs=[tile, tile, tile, tile]
        + [_resident_slice(pw[k], (layer,)) for k in ("g", "wg", "bg", "wbr", "wo")],
        out_specs=tile,
        out_shape=jax.ShapeDtypeStruct((n, D_MODEL), F32),
        compiler_params=_params(1),
        name="merge",
    )(x, ya, yb, yc, pw["g"], pw["wg"], pw["bg"], pw["wbr"], pw["wo"])


def _ple_body(x_ref, p_ref, g_ref, wp_ref, wpg_ref, o_ref):
    x = x_ref[...]
    e = _mm(p_ref[...].astype(BF16), wp_ref[...])
    pg = _sigmoid(_mm(_rms(x, g_ref[6:7, :]).astype(BF16), wpg_ref[...]))
    o_ref[...] = x + _rms(pg * e, g_ref[7:8, :])


def _ple(x, p, pw, layer):
    n = x.shape[0]
    tm = _row_tile(n, 512)
    row = lambda i: (i, 0)
    return pl.pallas_call(
        _ple_body,
        grid=(n // tm,),
        in_specs=[pl.BlockSpec((tm, D_MODEL), row), pl.BlockSpec((tm, PLE_DIM), row)]
        + [_resident_slice(pw[k], (layer,)) for k in ("g", "wp", "wpg")],
        out_specs=pl.BlockSpec((tm, D_MODEL), row),
        out_shape=jax.ShapeDtypeStruct((n, D_MODEL), F32),
        compiler_params=_params(1),
        name="ple",
    )(x, p, pw["g"], pw["wp"], pw["wpg"])


BAND_TQ = 256
BAND_SUB = 128
BAND_KBLK = 1 + -(-BAND // BAND_TQ)


def _prep_weights(norm_g, ffn_w13, ffn_w2, w_in, w_gate, b_gate, w_branch, w_out, w_ple, w_ple_gate):
    bf = lambda a: a.astype(BF16)
    return {"g": norm_g, "w13": bf(ffn_w13), "w2": bf(ffn_w2), "w_in": _pack_w_in(w_in), "wg": bf(w_gate),
            "bg": b_gate.reshape(DEPTH, 1, N_BRANCH * D_MODEL), "wbr": bf(w_branch), "wo": bf(w_out),
            "wp": bf(w_ple), "wpg": bf(w_ple_gate)}


def _trunk_layer(x, p, w, pw, layer, cache, batch, t, shared):
    x = _ffn(x, pw, layer, 0, 0, 1)
    pr, shared = _inproj(x, pw, layer, batch, t, shared)
    if cache is None:
        ya, h_new = _mamba(pr["z"], pr["xbc"], pr["dtwi"], None, None, w, batch, t)
        yb = _dsa(pr["qb"], pr["qi"], pr["dtwi"], pr["ki"], pr["kb"], pr["vb"], layer, batch, t, t, t, 0)
        yc = _band(pr["qc"], pr["kc_bf"], pr["vc_bf"], w["rel_bias"], batch, t, BAND_TQ, BAND_SUB, BAND_TQ,
                   BAND_KBLK, 0, 0, True)
        conv_src = pr["xbc"].reshape(batch, t, CONV_CH)
    else:
        past = cache["dsa_k"].shape[1]
        conv0 = jnp.pad(cache["conv"], ((0, 0), (CONV_PAD - (CONV_W - 1), 0), (0, 0)))
        h0 = cache["ssm"].reshape(batch, D_INNER, D_STATE)
        ya, h_new = _mamba(pr["z"], pr["xbc"], pr["dtwi"], conv0, h0, w, batch, t)

        n_valid = past + t
        s_keys = -(-n_valid // DSA_KB) * DSA_KB
        tq_pad = -(-t // DSA_TQ) * DSA_TQ

        def with_cache(c, new):
            width = new.shape[-1]
            a = jnp.concatenate([c.reshape(batch, past, width), new[layer].reshape(batch, t, width)], axis=1)
            return jnp.pad(a, ((0, 0), (0, s_keys - n_valid), (0, 0))).reshape(1, batch * s_keys, width)

        def pad_q(a):
            a = jnp.pad(a.reshape(batch, t, a.shape[-1]), ((0, 0), (0, tq_pad - t), (0, 0)))
            return a.reshape(batch * tq_pad, a.shape[-1])

        yb = _dsa(pad_q(pr["qb"]), pad_q(pr["qi"]), pad_q(pr["dtwi"]), with_cache(cache["idx_k"], pr["ki"]),
                  with_cache(cache["dsa_k"], pr["kb"]), with_cache(cache["dsa_v"], pr["vb"]),
                  0, batch, tq_pad, s_keys, n_valid, past)
        yb = yb.reshape(batch, tq_pad, D_MODEL)[:, :t].reshape(batch * t, D_MODEL)

        nrows = cache["band_k"].shape[1]
        k_rows = -(-(nrows + t) // LANES) * LANES
        lead = k_rows - nrows - t

        def with_band(c, new):
            a = jnp.concatenate([c.reshape(batch, nrows, D_MODEL).astype(BF16), new.reshape(batch, t, D_MODEL)],
                                axis=1)
            return jnp.pad(a, ((0, 0), (lead, 0), (0, 0))).reshape(batch * k_rows, D_MODEL)

        yc = _band(pr["qc"], with_band(cache["band_k"], pr["kc_bf"]), with_band(cache["band_v"], pr["vc_bf"]),
                   w["rel_bias"], batch, t, t, t, k_rows, 1, past, past - nrows, False)
        conv_src = jnp.concatenate([cache["conv"], pr["xbc"].reshape(batch, t, CONV_CH)], axis=1)

    x = _merge(x, ya, yb, yc, pw, layer)
    x = _ffn(x, pw, layer, 1, 4, 5)
    x = _ple(x, p.reshape(batch * t, PLE_DIM), pw, layer)

    state = (h_new.reshape(batch, SSM_HEADS, SSM_HEAD_DIM, D_STATE), conv_src[:, -(CONV_W - 1):])
    return x, state, shared


def _cache_outputs(shared, states, batch, t):
    band_rows = min(BAND, t)
    return (shared["kb"].reshape(DEPTH, batch, t, KV_HEADS, DSA_HEAD_DIM),
            shared["vb"].reshape(DEPTH, batch, t, KV_HEADS, DSA_HEAD_DIM),
            shared["ki"].reshape(DEPTH, batch, t, IDX_DIM),
            shared["kc"].reshape(DEPTH, batch, band_rows, BAND_HEADS, BAND_HEAD_DIM),
            shared["vc"].reshape(DEPTH, batch, band_rows, BAND_HEADS, BAND_HEAD_DIM),
            jnp.stack([s[0] for s in states]), jnp.stack([s[1] for s in states]))


def kernel(x_prompt, x_sample, p_prompt, p_sample, cache_dsa_k, cache_dsa_v, cache_idx_k, cache_band_k,
           cache_band_v, state_ssm, state_conv, norm_g, ffn_w13, ffn_w2, w_in, conv_w, conv_b, dt_bias,
           a_log, d_skip, ssm_norm_g, rel_bias, w_gate, b_gate, w_branch, w_out, w_ple, w_ple_gate):
    bp, tp, _ = x_prompt.shape
    bs, ts, _ = x_sample.shape
    yp = x_prompt.reshape(bp * tp, D_MODEL)
    ys = x_sample.reshape(bs * ts, D_MODEL)
    st_p, st_s, shared_p, shared_s = [], [], None, None
    pw = _prep_weights(norm_g, ffn_w13, ffn_w2, w_in, w_gate, b_gate, w_branch, w_out, w_ple, w_ple_gate)
    for i in range(DEPTH):
        w = {"conv_w": conv_w[i], "conv_b": conv_b[i], "dt_bias": dt_bias[i], "a_log": a_log[i],
             "d_skip": d_skip[i], "ssm_norm_g": ssm_norm_g[i], "rel_bias": rel_bias[i]}
        yp, sp, shared_p = _trunk_layer(yp, p_prompt[i], w, pw, i, None, bp, tp, shared_p)
        st_p.append(sp)
        cache = {"dsa_k": cache_dsa_k[i], "dsa_v": cache_dsa_v[i], "idx_k": cache_idx_k[i],
                 "band_k": cache_band_k[i], "band_v": cache_band_v[i], "ssm": state_ssm[i], "conv": state_conv[i]}
        ys, ss, shared_s = _trunk_layer(ys, p_sample[i], w, pw, i, cache, bs, ts, shared_s)
        st_s.append(ss)
    return (yp.reshape(bp, tp, D_MODEL), ys.reshape(bs, ts, D_MODEL),
            *_cache_outputs(shared_p, st_p, bp, tp), *_cache_outputs(shared_s, st_s, bs, ts))
```

```python
import functools
import math

import numpy as np
import jax
import jax.numpy as jnp
from jax import lax
from jax.experimental import pallas as pl
from jax.experimental.pallas import tpu as pltpu

F32 = jnp.float32
BF16 = jnp.bfloat16

D_MODEL = 1024
DEPTH = 2
CHUNK = 64
EPS = 1e-6
HALF = 0.5
D_FF = 2816
PLE_DIM = 256
SSM_HEAD_DIM = 64
D_INNER = D_MODEL
SSM_HEADS = D_INNER // SSM_HEAD_DIM
N_GROUPS = 4
HEADS_PER_GROUP = SSM_HEADS // N_GROUPS
D_STATE = 128
CONV_W = 4
CONV_CH = D_INNER + 2 * N_GROUPS * D_STATE
DSA_HEAD_DIM = 128
DSA_HEADS = D_MODEL // DSA_HEAD_DIM
KV_HEADS = 2
KV_REP = DSA_HEADS // KV_HEADS
IDX_HEADS = 4
IDX_DIM = 64
TOPK_MAX = 256
BAND_HEAD_DIM = 64
BAND_HEADS = D_MODEL // BAND_HEAD_DIM
LEFT_CHUNKS = 8
BAND = LEFT_CHUNKS * CHUNK
REL_CLIP = 256
N_BRANCH = 3
IN_WIDTHS = (D_INNER, CONV_CH, SSM_HEADS,
             DSA_HEADS * DSA_HEAD_DIM, KV_HEADS * DSA_HEAD_DIM, KV_HEADS * DSA_HEAD_DIM,
             IDX_HEADS * IDX_DIM, IDX_DIM, IDX_HEADS,
             BAND_HEADS * BAND_HEAD_DIM, BAND_HEADS * BAND_HEAD_DIM, BAND_HEADS * BAND_HEAD_DIM)
IN_SPLITS = tuple(int(s) for s in np.cumsum(IN_WIDTHS)[:-1])

LANES = 128
SUBLANES = 8
KV_DIM = KV_HEADS * DSA_HEAD_DIM
GROUP_CH = D_INNER // N_GROUPS
NEG_BIG = -1e30
INT_MIN = -2 ** 31
LOG2E = math.log2(math.e)
VMEM_LIMIT = 56 * 1024 * 1024


def _mm(a, b):
    return jnp.dot(a, b, preferred_element_type=F32)


def _mm_nt(a, b):
    return lax.dot_general(a, b, (((1,), (1,)), ((), ())), preferred_element_type=F32)


def _mm_tn(a, b):
    return lax.dot_general(a, b, (((0,), (0,)), ((), ())), preferred_element_type=F32)


def _rms(x, g):
    return x * lax.rsqrt(jnp.mean(x * x, axis=-1, keepdims=True) + EPS) * g


def _sigmoid(x):
    return 1.0 / (1.0 + jnp.exp(-x))


def _silu(x):
    return x * _sigmoid(x)


def _resident(shape):
    return pl.BlockSpec(shape, lambda *_: (0,) * len(shape), pipeline_mode=pl.Buffered(1))


def _resident_slice(arr, lead, block=None, at=None):
    tail = tuple(arr.shape[len(lead):]) if block is None else tuple(block)
    idx = tuple(lead) + ((0,) * len(tail) if at is None else tuple(at))
    return pl.BlockSpec((None,) * len(lead) + tail, lambda *_: idx, pipeline_mode=pl.Buffered(1))


def _params(n_grid_dims):
    return pltpu.CompilerParams(dimension_semantics=("arbitrary",) * n_grid_dims,
                                vmem_limit_bytes=VMEM_LIMIT)


def _row_tile(n_rows, want):
    t = min(want, n_rows)
    assert n_rows % t == 0
    return t


def _fold_rows(x, op):
    parts = [x[i * SUBLANES:(i + 1) * SUBLANES] for i in range(x.shape[0] // SUBLANES)]
    while len(parts) > 1:
        parts = [op(parts[i], parts[i + 1]) for i in range(0, len(parts) - 1, 2)] + parts[len(parts) & ~1:]
    return parts[0]


FF_CHUNK = 256


def _ffn_body(x_ref, g_ref, wa_ref, wb_ref, w2_ref, o_ref, *, g_pre, g_post):
    x = x_ref[...]
    u = _rms(x, g_ref[g_pre:g_pre + 1, :]).astype(BF16)
    acc = jnp.zeros(x.shape, F32)
    for c in range(D_FF // FF_CHUNK):
        sl = slice(c * FF_CHUNK, (c + 1) * FF_CHUNK)
        a = _mm(u, wa_ref[:, sl])
        b = _mm(u, wb_ref[:, sl])
        acc = acc + _mm((_silu(a) * b).astype(BF16), w2_ref[sl, :])
    o_ref[...] = x + HALF * _rms(acc, g_ref[g_post:g_post + 1, :])


def _ffn(x, pw, layer, j, g_pre, g_post):
    n = x.shape[0]
    tm = _row_tile(n, 512)
    row = lambda i: (i, 0)
    half = (D_MODEL, D_FF)
    return pl.pallas_call(
        functools.partial(_ffn_body, g_pre=g_pre, g_post=g_post),
        grid=(n // tm,),
        in_specs=[pl.BlockSpec((tm, D_MODEL), row), _resident_slice(pw["g"], (layer,)),
                  _resident_slice(pw["w13"], (layer, j), half, (0, 0)),
                  _resident_slice(pw["w13"], (layer, j), half, (0, 1)),
                  _resident_slice(pw["w2"], (layer, j))],
        out_specs=pl.BlockSpec((tm, D_MODEL), row),
        out_shape=jax.ShapeDtypeStruct((n, D_MODEL), F32),
        compiler_params=_params(1),
        name="ffn",
    )(x, pw["g"], pw["w13"], pw["w13"], pw["w2"])


_INPROJ_GROUPS = (D_INNER, CONV_CH, D_MODEL, KV_DIM, KV_DIM, IDX_HEADS * IDX_DIM, D_MODEL, D_MODEL, D_MODEL,
                  LANES, LANES)
_INPROJ_OUT = (
    ("z", 0, D_INNER, F32, None, "layer"),
    ("xbc", 1, CONV_CH, F32, None, "layer"),
    ("qb", 2, D_MODEL, BF16, DSA_HEAD_DIM ** -0.5 * LOG2E, "layer"),
    ("kb", 3, KV_DIM, F32, None, "stack"),
    ("vb", 4, KV_DIM, F32, None, "stack"),
    ("qi", 5, IDX_HEADS * IDX_DIM, F32, None, "layer"),
    ("qc", 6, D_MODEL, BF16, BAND_HEAD_DIM ** -0.5 * LOG2E, "layer"),
    ("kc", 7, D_MODEL, F32, None, "tail"),
    ("kc_bf", 7, D_MODEL, BF16, None, "layer"),
    ("vc", 8, D_MODEL, F32, None, "tail"),
    ("vc_bf", 8, D_MODEL, BF16, None, "layer"),
    ("ki", 9, IDX_DIM, F32, None, "stack"),
    ("dtwi", 10, LANES, F32, None, "layer"),
)
WI_LANE = SSM_HEADS


def _pack_w_in(w_in):
    z, xbc, dt, qb, kb, vb, qi, ki, wi, qc, kc, vc = jnp.split(w_in, IN_SPLITS, axis=-1)
    pad = lambda w: jnp.pad(w, ((0, 0), (0, 0), (0, LANES - w.shape[-1])))
    cols = [z, xbc, qb, kb, vb, qi, qc, kc, vc, pad(ki), pad(jnp.concatenate([dt, wi], axis=-1))]
    return jnp.concatenate(cols, axis=-1).astype(BF16)


def _inproj_body(x_ref, g_ref, w_ref, *refs):
    out_refs = refs[len(refs) - len(_INPROJ_OUT):]
    u = _rms(x_ref[...], g_ref[2:3, :]).astype(BF16)
    starts = np.concatenate([[0], np.cumsum(_INPROJ_GROUPS)])
    for grp, width in enumerate(_INPROJ_GROUPS):
        r = _mm(u, w_ref[:, int(starts[grp]):int(starts[grp]) + width])
        for (_, og, stored, dtype, scale, _), o_ref in zip(_INPROJ_OUT, out_refs):
            if og == grp:
                v = r if scale is None else r * scale
                o_ref[...] = (v if stored == width else v[:, :stored]).astype(dtype)


def _inproj(x, pw, layer, batch, t, shared):
    n = x.shape[0]
    tm = _row_tile(n, 256)
    tail = min(BAND, t)
    row = lambda i: (i, 0)
    stack_row = lambda i: (layer, i, 0)
    if tail == t:
        tail_row = stack_row
    else:
        assert t % tm == 0 and tail % tm == 0
        per_seq, per_tail = t // tm, tail // tm
        tail_row = lambda i: (layer, (i // per_seq) * per_tail + jnp.maximum(i % per_seq - (per_seq - per_tail), 0), 0)
    out_specs, out_shape, stacked = [], [], []
    for k, (name, _, width, dtype, _, kind) in enumerate(_INPROJ_OUT):
        if kind == "layer":
            out_specs.append(pl.BlockSpec((tm, width), row))
            out_shape.append(jax.ShapeDtypeStruct((n, width), dtype))
        else:
            rows = n if kind == "stack" else batch * tail
            out_specs.append(pl.BlockSpec((None, tm, width), stack_row if kind == "stack" else tail_row))
            out_shape.append(jax.ShapeDtypeStruct((DEPTH, rows, width), dtype))
            stacked.append((name, k))
    carried = [] if shared is None else [shared[name] for name, _ in stacked]
    aliases = {} if shared is None else {3 + a: k for a, (_, k) in enumerate(stacked)}
    outs = pl.pallas_call(
        _inproj_body,
        grid=(n // tm,),
        in_specs=[pl.BlockSpec((tm, D_MODEL), row), _resident_slice(pw["g"], (layer,)),
                  _resident_slice(pw["w_in"], (layer,))] + [pl.BlockSpec(memory_space=pl.ANY)] * len(carried),
        out_specs=out_specs,
        out_shape=out_shape,
        input_output_aliases=aliases,
        compiler_params=_params(1),
        name="inproj",
    )(x, pw["g"], pw["w_in"], *carried)
    pr = {o[0]: a for o, a in zip(_INPROJ_OUT, outs)}
    return pr, {name: pr[name] for name, _ in stacked}


CONV_PAD = 16
CONV_ROWS = CONV_PAD + CHUNK


def _split3(x):
    hi = x.astype(BF16)
    r = x - hi.astype(F32)
    mid = r.astype(BF16)
    lo = (r - mid.astype(F32)).astype(BF16)
    return hi, mid, lo


def _expand_heads(x, e):
    hi, mid, lo = _split3(x)
    return _mm(hi, e) + _mm(mid, e) + _mm(lo, e)


def _cumsum_rows(x):
    n = x.shape[0]
    row = lax.broadcasted_iota(jnp.int32, x.shape, 0)
    d = 1
    while d < n:
        x = x + jnp.where(row >= d, pltpu.roll(x, d, 0), 0.0)
        d *= 2
    return x


def _conv_shift_matrix():
    s = np.zeros(((CONV_W - 1) * CHUNK, 3 * CONV_ROWS), np.float32)
    for k in range(CONV_W - 1):
        for l in range(CHUNK):
            for part in range(3):
                s[k * CHUNK + l, part * CONV_ROWS + CONV_PAD - (CONV_W - 1) + k + l] = 1.0
    return jnp.asarray(s, BF16)


def _mamba_body(*refs, has_state, n_chunks):
    if has_state:
        (z_ref, xbc_ref, dtwi_ref, conv0_ref, h0_ref, cw_ref, cb_ref, dtb_ref, alog_ref, dskip_ref,
         ng_ref, e_ref, shift_ref, y_ref, hout_ref, tail_scr, ht_scr) = refs
    else:
        (z_ref, xbc_ref, dtwi_ref, cw_ref, cb_ref, dtb_ref, alog_ref, dskip_ref,
         ng_ref, e_ref, shift_ref, y_ref, hout_ref, tail_scr, ht_scr) = refs
    c = pl.program_id(1)
    q = CHUNK

    @pl.when(c == 0)
    def _():
        tail_scr[...] = jnp.zeros(tail_scr.shape, F32)
        if has_state:
            tail_scr[CONV_PAD - SUBLANES:CONV_PAD, :] = conv0_ref[...]
            ht_scr[...] = h0_ref[...].T
        else:
            ht_scr[...] = jnp.zeros(ht_scr.shape, F32)

    x = xbc_ref[...]
    hi, mid, lo = _split3(jnp.concatenate([tail_scr[...], x], axis=0))
    delayed = _mm(shift_ref[...], jnp.concatenate([hi, mid, lo], axis=0))
    tail_scr[CONV_PAD - SUBLANES:CONV_PAD, :] = x[q - SUBLANES:q, :]
    acc = cb_ref[...] + x * cw_ref[CONV_W - 1:CONV_W, :]
    for k in range(CONV_W - 1):
        acc = acc + delayed[k * q:(k + 1) * q, :] * cw_ref[k:k + 1, :]
    xc = _silu(acc)
    xs = xc[:, :D_INNER]
    bm = xc[:, D_INNER:D_INNER + N_GROUPS * D_STATE]
    cm = xc[:, D_INNER + N_GROUPS * D_STATE:]

    lane = lax.broadcasted_iota(jnp.int32, (q, LANES), 1)
    pre = dtwi_ref[...] + dtb_ref[...]
    dt = jnp.maximum(pre, 0.0) + jnp.log1p(jnp.exp(-jnp.abs(pre)))
    dt = jnp.where(lane < SSM_HEADS, dt, 0.0)
    dta = dt * (-jnp.exp(alog_ref[...]))
    cum = _cumsum_rows(dta)

    e = e_ref[...]
    ecol = _expand_heads(cum, e)
    dtx = _expand_heads(dt, e)
    li = lax.broadcasted_iota(jnp.int32, (q, D_INNER), 0)
    si = lax.broadcasted_iota(jnp.int32, (q, D_INNER), 1) & (q - 1)
    erow = jnp.sum(jnp.where(li == si, ecol, 0.0), axis=0, keepdims=True)
    elast = ecol[q - 1:q, :]
    causal = (li >= si)[:, :GROUP_CH]
    xdt = xs * dtx
    xdec = (xdt * jnp.exp(elast - ecol)).astype(BF16)
    exp_e = jnp.exp(ecol)
    chunk_decay = jnp.exp(elast)
    bdr = lax.broadcasted_iota(jnp.int32, (GROUP_CH, GROUP_CH), 0) // SSM_HEAD_DIM
    bdc = lax.broadcasted_iota(jnp.int32, (GROUP_CH, GROUP_CH), 1) // SSM_HEAD_DIM
    block_diag = bdr == bdc

    ys = []
    for g in range(N_GROUPS):
        sl = slice(g * GROUP_CH, (g + 1) * GROUP_CH)
        nl = slice(g * D_STATE, (g + 1) * D_STATE)
        bg = bm[:, nl].astype(BF16)
        cg = cm[:, nl].astype(BF16)
        cb = _mm_nt(cg, bg)
        cbt = jnp.concatenate([cb] * HEADS_PER_GROUP, axis=1)
        decay = jnp.exp(jnp.where(causal, ecol[:, sl] - erow[:, sl], NEG_BIG))
        m = (cbt * decay).astype(BF16)
        xg = xdt[:, sl]
        bd = jnp.where(block_diag, jnp.concatenate([xg] * HEADS_PER_GROUP, axis=0), 0.0).astype(BF16)
        y_diag = _mm(m, bd)
        ht_g = ht_scr[:, sl]
        y_off = _mm(cg, ht_g.astype(BF16)) * exp_e[:, sl]
        ys.append(y_diag + y_off)
        ht_scr[:, sl] = ht_g * chunk_decay[:, sl] + _mm_tn(bg, xdec[:, sl])
    y = jnp.concatenate(ys, axis=1) + dskip_ref[...] * xs
    y = y * _silu(z_ref[...])
    outs = []
    for g in range(N_GROUPS):
        yg = y[:, g * GROUP_CH:(g + 1) * GROUP_CH]
        outs.append(yg * lax.rsqrt(jnp.mean(yg * yg, axis=-1, keepdims=True) + EPS))
    y_ref[...] = (jnp.concatenate(outs, axis=1) * ng_ref[...]).astype(y_ref.dtype)

    @pl.when(c == n_chunks - 1)
    def _():
        hout_ref[...] = ht_scr[...].T


def _head_expand_matrix():
    e = np.zeros((LANES, D_INNER), np.float32)
    for h in range(SSM_HEADS):
        e[h, h * SSM_HEAD_DIM:(h + 1) * SSM_HEAD_DIM] = 1.0
    return jnp.asarray(e, BF16)


def _mamba(z, xbc, dtwi, conv0, h0, lw, batch, t):
    nc = t // CHUNK
    has_state = h0 is not None
    row = lambda b, c: (b * nc + c, 0)
    per_b = lambda b, c: (b, 0, 0)
    pad16 = lambda v: jnp.pad(v.reshape(1, SSM_HEADS), ((0, 0), (0, LANES - SSM_HEADS)))
    small = [lw["conv_w"], lw["conv_b"].reshape(1, CONV_CH), pad16(lw["dt_bias"]), pad16(lw["a_log"]),
             jnp.repeat(lw["d_skip"], SSM_HEAD_DIM).reshape(1, D_INNER),
             lw["ssm_norm_g"].reshape(1, D_INNER), _head_expand_matrix(), _conv_shift_matrix()]
    ins = [z, xbc, dtwi]
    in_specs = [pl.BlockSpec((CHUNK, D_INNER), row), pl.BlockSpec((CHUNK, CONV_CH), row),
                pl.BlockSpec((CHUNK, LANES), row)]
    if has_state:
        ins += [conv0, h0]
        in_specs += [pl.BlockSpec((None, SUBLANES, CONV_CH), per_b),
                     pl.BlockSpec((None, D_INNER, D_STATE), per_b)]
    ins += small
    in_specs += [_resident(a.shape) for a in small]
    y, h_out = pl.pallas_call(
        functools.partial(_mamba_body, has_state=has_state, n_chunks=nc),
        grid=(batch, nc),
        in_specs=in_specs,
        out_specs=[pl.BlockSpec((CHUNK, D_INNER), row), pl.BlockSpec((None, D_INNER, D_STATE), per_b)],
        out_shape=[jax.ShapeDtypeStruct((batch * t, D_INNER), BF16),
                   jax.ShapeDtypeStruct((batch, D_INNER, D_STATE), F32)],
        scratch_shapes=[pltpu.VMEM((CONV_PAD, CONV_CH), F32), pltpu.VMEM((D_STATE, D_INNER), F32)],
        compiler_params=_params(2),
        name="mamba",
    )(*ins)
    return y, h_out


DSA_TQ = LANES
DSA_KB = 512
IDX_CAT = 4 * IDX_DIM
DSA_OT_ROWS = 2 * DSA_HEAD_DIM


def _hi_lo(x):
    hi = x.astype(BF16).astype(F32)
    return hi, (x - hi).astype(BF16).astype(F32)


def _dsa_body(qb_ref, qi_ref, dtwi_ref, ki_ref, k_ref, v_ref, o_ref,
              kcat_scr, kbf_scr, vt_scr, tri_scr, key_scr, mask_scr, s_scr,
              *, n_kb, n_valid, q_pos0, topk, visible):
    tq, kb_rows = DSA_TQ, DSA_KB
    j = pl.program_id(1)

    @pl.when(j == 0)
    def _():
        hi, lo = _hi_lo(ki_ref[...])
        kcat_scr[...] = jnp.concatenate([hi, lo, hi, jnp.zeros_like(hi)], axis=1).astype(BF16)
        kbf_scr[...] = k_ref[...].astype(BF16)
        ones = jnp.ones((DSA_HEAD_DIM, kb_rows), F32)
        for kb in range(n_kb):
            vt = v_ref[kb * kb_rows:(kb + 1) * kb_rows, :].T
            for g in range(KV_HEADS):
                vt_g = vt[g * DSA_HEAD_DIM:(g + 1) * DSA_HEAD_DIM, :]
                vt_scr[g, kb] = jnp.concatenate([vt_g, ones], axis=0).astype(BF16)
        tr = lax.broadcasted_iota(jnp.int32, (kb_rows, kb_rows), 0)
        tc = lax.broadcasted_iota(jnp.int32, (kb_rows, kb_rows), 1)
        tri_scr[...] = jnp.where(tc < tr, 1.0, 0.0).astype(BF16)

    qpos = q_pos0 + j * tq + lax.broadcasted_iota(jnp.int32, (1, tq), 1)
    q_end = jnp.minimum(((qpos >> 6) + 1) << 6, n_valid)
    k_eff = jnp.minimum(q_end, topk).astype(F32)
    last_end = jnp.minimum((((q_pos0 + (j + 1) * tq - 1) >> 6) + 1) << 6, n_valid)
    nkb = (last_end + (kb_rows - 1)) // kb_rows

    wit = (dtwi_ref[...] * (IDX_DIM ** -0.5 * IDX_HEADS ** -0.5)).T
    qi = qi_ref[...]
    qparts = []
    for h in range(IDX_HEADS):
        hi, lo = _hi_lo(qi[:, h * IDX_DIM:(h + 1) * IDX_DIM])
        qparts.append(jnp.concatenate([hi, hi, lo, jnp.zeros_like(hi)], axis=1))
    qcat = jnp.concatenate(qparts, axis=0).astype(BF16)
    krow = lax.broadcasted_iota(jnp.int32, (kb_rows, 1), 0)
    qb = qb_ref[...]
    q4 = [jnp.concatenate([qb[:, (g * KV_REP + r) * DSA_HEAD_DIM:(g * KV_REP + r + 1) * DSA_HEAD_DIM]
                           for r in range(KV_REP)], axis=0) for g in range(KV_HEADS)]

    def run(n_vis):
        for kb in range(n_vis):
            logit = _mm_nt(kcat_scr[kb * kb_rows:(kb + 1) * kb_rows, :], qcat)
            sc = jnp.zeros((kb_rows, tq), F32)
            for h in range(IDX_HEADS):
                sc = sc + jnp.maximum(logit[:, h * tq:(h + 1) * tq], 0.0) * wit[WI_LANE + h:WI_LANE + h + 1, :]
            sc = jnp.where(kb * kb_rows + krow < q_end, sc, -jnp.inf)
            bits = lax.bitcast_convert_type(sc, jnp.int32)
            key_scr[kb] = jnp.where(bits < 0, bits ^ jnp.int32(0x7FFFFFFF), bits)

        def count(pred):
            acc = _fold_rows(jnp.where(pred(key_scr[0]), 1.0, 0.0), jnp.add)
            for kb in range(1, n_vis):
                acc = acc + _fold_rows(jnp.where(pred(key_scr[kb]), 1.0, 0.0), jnp.add)
            return jnp.sum(acc, axis=0, keepdims=True)

        def radix_step(i, tu):
            cand = tu | lax.shift_left(jnp.int32(1), 31 - i)
            thr_c = cand ^ jnp.int32(INT_MIN)
            return jnp.where(count(lambda k: k >= thr_c) >= k_eff, cand, tu)

        tu = lax.fori_loop(0, 32, radix_step, jnp.zeros((1, tq), jnp.int32))
        thr = tu ^ jnp.int32(INT_MIN)

        need = k_eff - count(lambda k: k > thr)
        before = jnp.zeros((1, tq), F32)
        for kb in range(n_vis):
            key = key_scr[kb]
            eq = jnp.where(key == thr, 1.0, 0.0)
            rank = _mm(tri_scr[...], eq.astype(BF16)) + before
            take = jnp.where(key > thr, 1.0, jnp.where(rank < need, eq, 0.0))
            mask_scr[kb] = jnp.where(take > 0.0, 0.0, NEG_BIG)
            before = before + jnp.sum(_fold_rows(eq, jnp.add), axis=0, keepdims=True)

        m8 = [jnp.full((SUBLANES, KV_REP * tq), NEG_BIG, F32) for _ in range(KV_HEADS)]
        for kb in range(n_vis):
            mask4 = jnp.concatenate([mask_scr[kb]] * KV_REP, axis=1)
            for g in range(KV_HEADS):
                kg = kbf_scr[kb * kb_rows:(kb + 1) * kb_rows, g * DSA_HEAD_DIM:(g + 1) * DSA_HEAD_DIM]
                st = _mm_nt(kg, q4[g]) + mask4
                s_scr[kb, g] = st
                m8[g] = jnp.maximum(m8[g], _fold_rows(st, jnp.maximum))
        m = [jnp.max(m8[g], axis=0, keepdims=True) for g in range(KV_HEADS)]
        acc = [None] * KV_HEADS
        for kb in range(n_vis):
            for g in range(KV_HEADS):
                p = jnp.exp2(s_scr[kb, g] - m[g]).astype(BF16)
                pv = _mm(vt_scr[g, kb], p)
                acc[g] = pv if acc[g] is None else acc[g] + pv
        outs = []
        for g in range(KV_HEADS):
            ot = acc[g][0:DSA_HEAD_DIM, :] / acc[g][DSA_HEAD_DIM:DSA_HEAD_DIM + 1, :]
            outs += [ot[:, r * tq:(r + 1) * tq].T for r in range(KV_REP)]
        o_ref[...] = jnp.concatenate(outs, axis=1).astype(o_ref.dtype)

    if len(visible) == 1:
        run(visible[0])
    else:
        for n_vis in visible:
            pl.when(nkb == n_vis)(functools.partial(run, n_vis))


def _dsa(qb, qi, dtwi, ki_all, k_all, v_all, key_layer, batch, t, s_keys, n_valid, q_pos0):
    tq, kb_rows = DSA_TQ, DSA_KB
    assert t % tq == 0 and s_keys % kb_rows == 0
    nq = t // tq
    n_kb = s_keys // kb_rows
    topk = min(TOPK_MAX, n_valid // 4)
    qrow = lambda b, j: (b * nq + j, 0)
    krow = lambda b, j: (key_layer, b, 0)

    def visible_blocks(j):
        last_end = min(((((q_pos0 + (j + 1) * tq - 1) >> 6) + 1) << 6), n_valid)
        return (last_end + kb_rows - 1) // kb_rows

    visible = tuple(sorted({visible_blocks(j) for j in range(nq)}))
    return pl.pallas_call(
        functools.partial(_dsa_body, n_kb=n_kb, n_valid=n_valid, q_pos0=q_pos0, topk=topk, visible=visible),
        grid=(batch, nq),
        in_specs=[pl.BlockSpec((tq, D_MODEL), qrow), pl.BlockSpec((tq, IDX_HEADS * IDX_DIM), qrow),
                  pl.BlockSpec((tq, LANES), qrow), pl.BlockSpec((None, s_keys, IDX_DIM), krow),
                  pl.BlockSpec((None, s_keys, KV_DIM), krow), pl.BlockSpec((None, s_keys, KV_DIM), krow)],
        out_specs=pl.BlockSpec((tq, D_MODEL), qrow),
        out_shape=jax.ShapeDtypeStruct((batch * t, D_MODEL), BF16),
        scratch_shapes=[pltpu.VMEM((s_keys, IDX_CAT), BF16), pltpu.VMEM((s_keys, KV_DIM), BF16),
                        pltpu.VMEM((KV_HEADS, n_kb, DSA_OT_ROWS, kb_rows), BF16),
                        pltpu.VMEM((kb_rows, kb_rows), BF16),
                        pltpu.VMEM((n_kb, kb_rows, tq), jnp.int32), pltpu.VMEM((n_kb, kb_rows, tq), F32),
                        pltpu.VMEM((n_kb, KV_HEADS, kb_rows, KV_REP * tq), F32)],
        compiler_params=_params(2),
        name="dsa",
    )(qb, qi, dtwi, ki_all, k_all, v_all)


def _band_body(*refs, tq, sub, n_kblk, q_pos0, k_min, clamped):
    q_ref = refs[0]
    k_refs = refs[1:1 + n_kblk]
    v_refs = refs[1 + n_kblk:1 + 2 * n_kblk]
    vec_ref, o_ref, bias_scr = refs[1 + 2 * n_kblk:]
    w = sum(r.shape[0] for r in k_refs)
    wsub = w - tq + sub
    i = pl.program_id(1)

    @pl.when((pl.program_id(0) == 0) & (i == 0))
    def _():
        r = lax.broadcasted_iota(jnp.int32, (sub, wsub), 0)
        c = lax.broadcasted_iota(jnp.int32, (sub, wsub), 1)
        dchunk = (r >> 6) + (wsub - sub) // CHUNK - (c >> 6)
        band_mask = jnp.where((dchunk >= 0) & (dchunk <= LEFT_CHUNKS), 0.0, NEG_BIG)
        for h in range(BAND_HEADS):
            rows = jnp.broadcast_to(vec_ref[h:h + 1, :], (sub, vec_ref.shape[1]))
            toeplitz = pltpu.roll(rows, 0, 1, stride=1, stride_axis=0)[:, :wsub]
            bias_scr[h // 2, (h % 2) * sub:(h % 2 + 1) * sub, :] = toeplitz * LOG2E + band_mask

    def window(blocks, lo, lanes):
        parts, r0 = [], 0
        for blk in blocks:
            a, b = max(lo, r0), min(lo + wsub, r0 + blk.shape[0])
            if a < b:
                parts.append(blk[a - r0:b - r0, lanes])
            r0 += blk.shape[0]
        return parts[0] if len(parts) == 1 else jnp.concatenate(parts, axis=0)

    lane = lax.broadcasted_iota(jnp.int32, (sub, LANES), 1)
    first_head = lane < BAND_HEAD_DIM
    keep_a = jnp.where(first_head, 1.0, 0.0).astype(BF16)
    keep_b = jnp.where(first_head, 0.0, 1.0).astype(BF16)
    ones = jnp.ones((wsub, LANES), BF16)
    chains = [(hp, c2) for hp in range(BAND_HEADS // 2) for c2 in range(tq // sub)]

    def scores(hp, c2, mask_missing_keys):
        lanes = slice(hp * LANES, (hp + 1) * LANES)
        qp = q_ref[c2 * sub:(c2 + 1) * sub, lanes]
        q2 = jnp.concatenate([qp * keep_a, qp * keep_b], axis=0)
        s = _mm_nt(q2, window(k_refs, c2 * sub, lanes)) + bias_scr[hp]
        if mask_missing_keys:
            kpos = q_pos0 + i * tq + c2 * sub + (sub - wsub) + lax.broadcasted_iota(jnp.int32, (1, wsub), 1)
            s = s + jnp.where(kpos >= k_min, 0.0, NEG_BIG)
        return s

    def attend(hp, c2, s):
        lanes = slice(hp * LANES, (hp + 1) * LANES)
        p = jnp.exp2(s - jnp.max(s, axis=1, keepdims=True)).astype(BF16)
        o = _mm(p, jnp.concatenate([window(v_refs, c2 * sub, lanes), ones], axis=1))
        oa = o[:sub, :LANES] / o[:sub, LANES:LANES + 1]
        ob = o[sub:, :LANES] / o[sub:, LANES:LANES + 1]
        o_ref[c2 * sub:(c2 + 1) * sub, lanes] = jnp.where(first_head, oa, ob).astype(o_ref.dtype)

    def heads(mask_missing_keys):
        ahead, pending = 2, {}
        for n in range(len(chains) + ahead):
            if n < len(chains):
                pending[n] = scores(*chains[n], mask_missing_keys)
            if n >= ahead:
                attend(*chains[n - ahead], pending.pop(n - ahead))

    if clamped:
        first_full = -(-(w - tq) // tq)
        pl.when(i < first_full)(functools.partial(heads, True))
        pl.when(i >= first_full)(functools.partial(heads, False))
    else:
        heads(False)


def _band_bias_vec(rel_bias, sub, wsub):
    l = -(-(wsub + sub) // LANES) * LANES
    m = np.arange(l)
    d = np.where(m < wsub, m, m - l)
    rel = np.clip(wsub - sub - d, -REL_CLIP, REL_CLIP) + REL_CLIP
    return rel_bias[jnp.asarray(rel)].T


def _band(q, k, v, rel_bias, batch, t, tq, sub, k_block_rows, n_kblk, q_pos0, k_min, clamped):
    nq = t // tq
    w = n_kblk * k_block_rows
    wsub = w - tq + sub
    assert wsub % LANES == 0 and (wsub - sub) % CHUNK == 0
    qrow = lambda b, i: (b * nq + i, 0)

    def krow(off):
        if clamped:
            return lambda b, i: (b * nq + jnp.maximum(i - (n_kblk - 1) + off, 0), 0)
        return lambda b, i: (b * n_kblk + off, 0)

    vec = _band_bias_vec(rel_bias, sub, wsub)
    kspecs = [pl.BlockSpec((k_block_rows, D_MODEL), krow(o)) for o in range(n_kblk)]
    return pl.pallas_call(
        functools.partial(_band_body, tq=tq, sub=sub, n_kblk=n_kblk, q_pos0=q_pos0, k_min=k_min, clamped=clamped),
        grid=(batch, nq),
        in_specs=[pl.BlockSpec((tq, D_MODEL), qrow)] + kspecs + kspecs + [_resident(vec.shape)],
        out_specs=pl.BlockSpec((tq, D_MODEL), qrow),
        out_shape=jax.ShapeDtypeStruct((batch * t, D_MODEL), BF16),
        scratch_shapes=[pltpu.VMEM((BAND_HEADS // 2, 2 * sub, wsub), F32)],
        compiler_params=_params(2),
        name="band",
    )(q, *([k] * n_kblk), *([v] * n_kblk), vec)


def _merge_body(x_ref, ya_ref, yb_ref, yc_ref, g_ref, wg_ref, bg_ref, wbr_ref, wo_ref, o_ref):
    x = x_ref[...]
    u = _rms(x, g_ref[2:3, :]).astype(BF16)
    mix = jnp.zeros(x.shape, F32)
    for k, y_ref in enumerate((ya_ref, yb_ref, yc_ref)):
        sl = slice(k * D_MODEL, (k + 1) * D_MODEL)
        gate = _sigmoid(_mm(u, wg_ref[:, sl]) + bg_ref[:, sl])
        mix = mix + gate * _mm(y_ref[...], wbr_ref[k])
    o_ref[...] = x + _rms(_mm(mix.astype(BF16), wo_ref[...]), g_ref[3:4, :])


def _merge(x, ya, yb, yc, pw, layer):
    n = x.shape[0]
    tm = _row_tile(n, 256)
    row = lambda i: (i, 0)
    tile = pl.BlockSpec((tm, D_MODEL), row)
    return pl.pallas_call(
        _merge_body,
        grid=(n // tm,),
        in_specs=[tile, tile, tile, tile]
        + [_resident_slice(pw[k], (layer,)) for k in ("g", "wg", "bg", "wbr", "wo")],
        out_specs=tile,
        out_shape=jax.ShapeDtypeStruct((n, D_MODEL), F32),
        compiler_params=_params(1),
        name="merge",
    )(x, ya, yb, yc, pw["g"], pw["wg"], pw["bg"], pw["wbr"], pw["wo"])


def _ple_body(x_ref, p_ref, g_ref, wp_ref, wpg_ref, o_ref):
    x = x_ref[...]
    e = _mm(p_ref[...].astype(BF16), wp_ref[...])
    pg = _sigmoid(_mm(_rms(x, g_ref[6:7, :]).astype(BF16), wpg_ref[...]))
    o_ref[...] = x + _rms(pg * e, g_ref[7:8, :])


def _ple(x, p, pw, layer):
    n = x.shape[0]
    tm = _row_tile(n, 512)
    row = lambda i: (i, 0)
    return pl.pallas_call(
        _ple_body,
        grid=(n // tm,),
        in_specs=[pl.BlockSpec((tm, D_MODEL), row), pl.BlockSpec((tm, PLE_DIM), row)]
        + [_resident_slice(pw[k], (layer,)) for k in ("g", "wp", "wpg")],
        out_specs=pl.BlockSpec((tm, D_MODEL), row),
        out_shape=jax.ShapeDtypeStruct((n, D_MODEL), F32),
        compiler_params=_params(1),
        name="ple",
    )(x, p, pw["g"], pw["wp"], pw["wpg"])


BAND_TQ = 256
BAND_SUB = 128
BAND_KBLK = 1 + -(-BAND // BAND_TQ)


def _prep_weights(norm_g, ffn_w13, ffn_w2, w_in, w_gate, b_gate, w_branch, w_out, w_ple, w_ple_gate):
    bf = lambda a: a.astype(BF16)
    return {"g": norm_g, "w13": bf(ffn_w13), "w2": bf(ffn_w2), "w_in": _pack_w_in(w_in), "wg": bf(w_gate),
            "bg": b_gate.reshape(DEPTH, 1, N_BRANCH * D_MODEL), "wbr": bf(w_branch), "wo": bf(w_out),
            "wp": bf(w_ple), "wpg": bf(w_ple_gate)}


def _trunk_layer(x, p, w, pw, layer, cache, batch, t, shared):
    x = _ffn(x, pw, layer, 0, 0, 1)
    pr, shared = _inproj(x, pw, layer, batch, t, shared)
    if cache is None:
        ya, h_new = _mamba(pr["z"], pr["xbc"], pr["dtwi"], None, None, w, batch, t)
        yb = _dsa(pr["qb"], pr["qi"], pr["dtwi"], pr["ki"], pr["kb"], pr["vb"], layer, batch, t, t, t, 0)
        yc = _band(pr["qc"], pr["kc_bf"], pr["vc_bf"], w["rel_bias"], batch, t, BAND_TQ, BAND_SUB, BAND_TQ,
                   BAND_KBLK, 0, 0, True)
        conv_src = pr["xbc"].reshape(batch, t, CONV_CH)
    else:
        past = cache["dsa_k"].shape[1]
        conv0 = jnp.pad(cache["conv"], ((0, 0), (SUBLANES - (CONV_W - 1), 0), (0, 0)))
        h0 = cache["ssm"].reshape(batch, D_INNER, D_STATE)
        ya, h_new = _mamba(pr["z"], pr["xbc"], pr["dtwi"], conv0, h0, w, batch, t)

        n_valid = past + t
        s_keys = -(-n_valid // DSA_KB) * DSA_KB
        tq_pad = -(-t // DSA_TQ) * DSA_TQ

        def with_cache(c, new):
            width = new.shape[-1]
            a = jnp.concatenate([c.reshape(batch, past, width), new[layer].reshape(batch, t, width)], axis=1)
            return jnp.pad(a, ((0, 0), (0, s_keys - n_valid), (0, 0))).reshape(1, batch * s_keys, width)

        def pad_q(a):
            a = jnp.pad(a.reshape(batch, t, a.shape[-1]), ((0, 0), (0, tq_pad - t), (0, 0)))
            return a.reshape(batch * tq_pad, a.shape[-1])

        yb = _dsa(pad_q(pr["qb"]), pad_q(pr["qi"]), pad_q(pr["dtwi"]), with_cache(cache["idx_k"], pr["ki"]),
                  with_cache(cache["dsa_k"], pr["kb"]), with_cache(cache["dsa_v"], pr["vb"]),
                  0, batch, tq_pad, s_keys, n_valid, past)
        yb = yb.reshape(batch, tq_pad, D_MODEL)[:, :t].reshape(batch * t, D_MODEL)

        nrows = cache["band_k"].shape[1]
        k_rows = -(-(nrows + t) // LANES) * LANES
        lead = k_rows - nrows - t

        def with_band(c, new):
            a = jnp.concatenate([c.reshape(batch, nrows, D_MODEL).astype(BF16), new.reshape(batch, t, D_MODEL)],
                                axis=1)
            return jnp.pad(a, ((0, 0), (lead, 0), (0, 0))).reshape(batch * k_rows, D_MODEL)

        yc = _band(pr["qc"], with_band(cache["band_k"], pr["kc_bf"]), with_band(cache["band_v"], pr["vc_bf"]),
                   w["rel_bias"], batch, t, t, t, k_rows, 1, past, past - nrows, False)
        conv_src = jnp.concatenate([cache["conv"], pr["xbc"].reshape(batch, t, CONV_CH)], axis=1)

    x = _merge(x, ya, yb, yc, pw, layer)
    x = _ffn(x, pw, layer, 1, 4, 5)
    x = _ple(x, p.reshape(batch * t, PLE_DIM), pw, layer)

    state = (h_new.reshape(batch, SSM_HEADS, SSM_HEAD_DIM, D_STATE), conv_src[:, -(CONV_W - 1):])
    return x, state, shared


def _cache_outputs(shared, states, batch, t):
    band_rows = min(BAND, t)
    return (shared["kb"].reshape(DEPTH, batch, t, KV_HEADS, DSA_HEAD_DIM),
            shared["vb"].reshape(DEPTH, batch, t, KV_HEADS, DSA_HEAD_DIM),
            shared["ki"].reshape(DEPTH, batch, t, IDX_DIM),
            shared["kc"].reshape(DEPTH, batch, band_rows, BAND_HEADS, BAND_HEAD_DIM),
            shared["vc"].reshape(DEPTH, batch, band_rows, BAND_HEADS, BAND_HEAD_DIM),
            jnp.stack([s[0] for s in states]), jnp.stack([s[1] for s in states]))


def kernel(x_prompt, x_sample, p_prompt, p_sample, cache_dsa_k, cache_dsa_v, cache_idx_k, cache_band_k,
           cache_band_v, state_ssm, state_conv, norm_g, ffn_w13, ffn_w2, w_in, conv_w, conv_b, dt_bias,
           a_log, d_skip, ssm_norm_g, rel_bias, w_gate, b_gate, w_branch, w_out, w_ple, w_ple_gate):
    bp, tp, _ = x_prompt.shape
    bs, ts, _ = x_sample.shape
    yp = x_prompt.reshape(bp * tp, D_MODEL)
    ys = x_sample.reshape(bs * ts, D_MODEL)
    st_p, st_s, shared_p, shared_s = [], [], None, None
    pw = _prep_weights(norm_g, ffn_w13, ffn_w2, w_in, w_gate, b_gate, w_branch, w_out, w_ple, w_ple_gate)
    for i in range(DEPTH):
        w = {"conv_w": conv_w[i], "conv_b": conv_b[i], "dt_bias": dt_bias[i], "a_log": a_log[i],
             "d_skip": d_skip[i], "ssm_norm_g": ssm_norm_g[i], "rel_bias": rel_bias[i]}
        yp, sp, shared_p = _trunk_layer(yp, p_prompt[i], w, pw, i, None, bp, tp, shared_p)
        st_p.append(sp)
        cache = {"dsa_k": cache_dsa_k[i], "dsa_v": cache_dsa_v[i], "idx_k": cache_idx_k[i],
                 "band_k": cache_band_k[i], "band_v": cache_band_v[i], "ssm": state_ssm[i], "conv": state_conv[i]}
        ys, ss, shared_s = _trunk_layer(ys, p_sample[i], w, pw, i, cache, bs, ts, shared_s)
        st_s.append(ss)
    return (yp.reshape(bp, tp, D_MODEL), ys.reshape(bs, ts, D_MODEL),
            *_cache_outputs(shared_p, st_p, bp, tp), *_cache_outputs(shared_s, st_s, bs, ts))
```

```python
import functools
import math

import numpy as np
import jax
import jax.numpy as jnp
from jax import lax
from jax.experimental import pallas as pl
from jax.experimental.pallas import tpu as pltpu

F32 = jnp.float32
BF16 = jnp.bfloat16

D_MODEL = 1024
DEPTH = 2
CHUNK = 64
EPS = 1e-6
HALF = 0.5
D_FF = 2816
PLE_DIM = 256
SSM_HEAD_DIM = 64
D_INNER = D_MODEL
SSM_HEADS = D_INNER // SSM_HEAD_DIM
N_GROUPS = 4
HEADS_PER_GROUP = SSM_HEADS // N_GROUPS
D_STATE = 128
CONV_W = 4
CONV_CH = D_INNER + 2 * N_GROUPS * D_STATE
DSA_HEAD_DIM = 128
DSA_HEADS = D_MODEL // DSA_HEAD_DIM
KV_HEADS = 2
KV_REP = DSA_HEADS // KV_HEADS
IDX_HEADS = 4
IDX_DIM = 64
TOPK_MAX = 256
BAND_HEAD_DIM = 64
BAND_HEADS = D_MODEL // BAND_HEAD_DIM
LEFT_CHUNKS = 8
BAND = LEFT_CHUNKS * CHUNK
REL_CLIP = 256
N_BRANCH = 3
IN_WIDTHS = (D_INNER, CONV_CH, SSM_HEADS,
             DSA_HEADS * DSA_HEAD_DIM, KV_HEADS * DSA_HEAD_DIM, KV_HEADS * DSA_HEAD_DIM,
             IDX_HEADS * IDX_DIM, IDX_DIM, IDX_HEADS,
             BAND_HEADS * BAND_HEAD_DIM, BAND_HEADS * BAND_HEAD_DIM, BAND_HEADS * BAND_HEAD_DIM)
IN_SPLITS = tuple(int(s) for s in np.cumsum(IN_WIDTHS)[:-1])

LANES = 128
SUBLANES = 8
KV_DIM = KV_HEADS * DSA_HEAD_DIM
GROUP_CH = D_INNER // N_GROUPS
NEG_BIG = -1e30
INT_MIN = -2 ** 31
LOG2E = math.log2(math.e)
VMEM_LIMIT = 56 * 1024 * 1024


def _mm(a, b):
    return jnp.dot(a, b, preferred_element_type=F32)


def _mm_nt(a, b):
    return lax.dot_general(a, b, (((1,), (1,)), ((), ())), preferred_element_type=F32)


def _mm_tn(a, b):
    return lax.dot_general(a, b, (((0,), (0,)), ((), ())), preferred_element_type=F32)


def _rms(x, g):
    return x * lax.rsqrt(jnp.mean(x * x, axis=-1, keepdims=True) + EPS) * g


def _sigmoid(x):
    return 1.0 / (1.0 + jnp.exp(-x))


def _silu(x):
    return x * _sigmoid(x)


def _resident(shape):
    return pl.BlockSpec(shape, lambda *_: (0,) * len(shape), pipeline_mode=pl.Buffered(1))


def _resident_slice(arr, lead, block=None, at=None):
    tail = tuple(arr.shape[len(lead):]) if block is None else tuple(block)
    idx = tuple(lead) + ((0,) * len(tail) if at is None else tuple(at))
    return pl.BlockSpec((None,) * len(lead) + tail, lambda *_: idx, pipeline_mode=pl.Buffered(1))


def _params(n_grid_dims):
    return pltpu.CompilerParams(dimension_semantics=("arbitrary",) * n_grid_dims,
                                vmem_limit_bytes=VMEM_LIMIT)


def _row_tile(n_rows, want):
    t = min(want, n_rows)
    assert n_rows % t == 0
    return t


def _fold_rows(x, op):
    parts = [x[i * SUBLANES:(i + 1) * SUBLANES] for i in range(x.shape[0] // SUBLANES)]
    while len(parts) > 1:
        parts = [op(parts[i], parts[i + 1]) for i in range(0, len(parts) - 1, 2)] + parts[len(parts) & ~1:]
    return parts[0]


FF_CHUNK = 256


def _ffn_body(x_ref, g_ref, wa_ref, wb_ref, w2_ref, o_ref, *, g_pre, g_post):
    x = x_ref[...]
    u = _rms(x, g_ref[g_pre:g_pre + 1, :]).astype(BF16)
    acc = jnp.zeros(x.shape, F32)
    for c in range(D_FF // FF_CHUNK):
        sl = slice(c * FF_CHUNK, (c + 1) * FF_CHUNK)
        a = _mm(u, wa_ref[:, sl])
        b = _mm(u, wb_ref[:, sl])
        acc = acc + _mm((_silu(a) * b).astype(BF16), w2_ref[sl, :])
    o_ref[...] = x + HALF * _rms(acc, g_ref[g_post:g_post + 1, :])


def _ffn(x, pw, layer, j, g_pre, g_post):
    n = x.shape[0]
    tm = _row_tile(n, 512)
    row = lambda i: (i, 0)
    half = (D_MODEL, D_FF)
    return pl.pallas_call(
        functools.partial(_ffn_body, g_pre=g_pre, g_post=g_post),
        grid=(n // tm,),
        in_specs=[pl.BlockSpec((tm, D_MODEL), row), _resident_slice(pw["g"], (layer,)),
                  _resident_slice(pw["w13"], (layer, j), half, (0, 0)),
                  _resident_slice(pw["w13"], (layer, j), half, (0, 1)),
                  _resident_slice(pw["w2"], (layer, j))],
        out_specs=pl.BlockSpec((tm, D_MODEL), row),
        out_shape=jax.ShapeDtypeStruct((n, D_MODEL), F32),
        compiler_params=_params(1),
        name="ffn",
    )(x, pw["g"], pw["w13"], pw["w13"], pw["w2"])


_INPROJ_GROUPS = (D_INNER, CONV_CH, D_MODEL, KV_DIM, KV_DIM, IDX_HEADS * IDX_DIM, D_MODEL, D_MODEL, D_MODEL,
                  LANES, LANES)
_INPROJ_OUT = (
    ("z", 0, D_INNER, F32, None, "layer"),
    ("xbc", 1, CONV_CH, F32, None, "layer"),
    ("qb", 2, D_MODEL, BF16, DSA_HEAD_DIM ** -0.5 * LOG2E, "layer"),
    ("kb", 3, KV_DIM, F32, None, "stack"),
    ("vb", 4, KV_DIM, F32, None, "stack"),
    ("qi", 5, IDX_HEADS * IDX_DIM, F32, None, "layer"),
    ("qc", 6, D_MODEL, BF16, BAND_HEAD_DIM ** -0.5 * LOG2E, "layer"),
    ("kc", 7, D_MODEL, F32, None, "tail"),
    ("kc_bf", 7, D_MODEL, BF16, None, "layer"),
    ("vc", 8, D_MODEL, F32, None, "tail"),
    ("vc_bf", 8, D_MODEL, BF16, None, "layer"),
    ("ki", 9, IDX_DIM, F32, None, "stack"),
    ("dtwi", 10, LANES, F32, None, "layer"),
)
WI_LANE = SSM_HEADS


def _pack_w_in(w_in):
    z, xbc, dt, qb, kb, vb, qi, ki, wi, qc, kc, vc = jnp.split(w_in, IN_SPLITS, axis=-1)
    pad = lambda w: jnp.pad(w, ((0, 0), (0, 0), (0, LANES - w.shape[-1])))
    cols = [z, xbc, qb, kb, vb, qi, qc, kc, vc, pad(ki), pad(jnp.concatenate([dt, wi], axis=-1))]
    return jnp.concatenate(cols, axis=-1).astype(BF16)


def _inproj_body(x_ref, g_ref, w_ref, *refs):
    out_refs = refs[len(refs) - len(_INPROJ_OUT):]
    u = _rms(x_ref[...], g_ref[2:3, :]).astype(BF16)
    starts = np.concatenate([[0], np.cumsum(_INPROJ_GROUPS)])
    for grp, width in enumerate(_INPROJ_GROUPS):
        r = _mm(u, w_ref[:, int(starts[grp]):int(starts[grp]) + width])
        for (_, og, stored, dtype, scale, _), o_ref in zip(_INPROJ_OUT, out_refs):
            if og == grp:
                v = r if scale is None else r * scale
                o_ref[...] = (v if stored == width else v[:, :stored]).astype(dtype)


def _inproj(x, pw, layer, batch, t, shared):
    n = x.shape[0]
    tm = _row_tile(n, 256)
    tail = min(BAND, t)
    row = lambda i: (i, 0)
    stack_row = lambda i: (layer, i, 0)
    if tail == t:
        tail_row = stack_row
    else:
        assert t % tm == 0 and tail % tm == 0
        per_seq, per_tail = t // tm, tail // tm
        tail_row = lambda i: (layer, (i // per_seq) * per_tail + jnp.maximum(i % per_seq - (per_seq - per_tail), 0), 0)
    out_specs, out_shape, stacked = [], [], []
    for k, (name, _, width, dtype, _, kind) in enumerate(_INPROJ_OUT):
        if kind == "layer":
            out_specs.append(pl.BlockSpec((tm, width), row))
            out_shape.append(jax.ShapeDtypeStruct((n, width), dtype))
        else:
            rows = n if kind == "stack" else batch * tail
            out_specs.append(pl.BlockSpec((None, tm, width), stack_row if kind == "stack" else tail_row))
            out_shape.append(jax.ShapeDtypeStruct((DEPTH, rows, width), dtype))
            stacked.append((name, k))
    carried = [] if shared is None else [shared[name] for name, _ in stacked]
    aliases = {} if shared is None else {3 + a: k for a, (_, k) in enumerate(stacked)}
    outs = pl.pallas_call(
        _inproj_body,
        grid=(n // tm,),
        in_specs=[pl.BlockSpec((tm, D_MODEL), row), _resident_slice(pw["g"], (layer,)),
                  _resident_slice(pw["w_in"], (layer,))] + [pl.BlockSpec(memory_space=pl.ANY)] * len(carried),
        out_specs=out_specs,
        out_shape=out_shape,
        input_output_aliases=aliases,
        compiler_params=_params(1),
        name="inproj",
    )(x, pw["g"], pw["w_in"], *carried)
    pr = {o[0]: a for o, a in zip(_INPROJ_OUT, outs)}
    return pr, {name: pr[name] for name, _ in stacked}


CONV_PAD = 16
CONV_ROWS = CONV_PAD + CHUNK
MAMBA_PAIR = 2


def _split3(x):
    hi = x.astype(BF16)
    r = x - hi.astype(F32)
    mid = r.astype(BF16)
    lo = (r - mid.astype(F32)).astype(BF16)
    return hi, mid, lo


def _expand_heads(x, e):
    hi, mid, lo = _split3(x)
    return _mm(hi, e) + _mm(mid, e) + _mm(lo, e)


def _cumsum_rows(x):
    n = x.shape[0]
    row = lax.broadcasted_iota(jnp.int32, x.shape, 0)
    d = 1
    while d < n:
        x = x + jnp.where(row >= d, pltpu.roll(x, d, 0), 0.0)
        d *= 2
    return x


def _conv_shift_matrix():
    s = np.zeros(((CONV_W - 1) * CHUNK, 3 * CONV_ROWS), np.float32)
    for k in range(CONV_W - 1):
        for l in range(CHUNK):
            for part in range(3):
                s[k * CHUNK + l, part * CONV_ROWS + CONV_PAD - (CONV_W - 1) + k + l] = 1.0
    return jnp.asarray(s, BF16)


def _mamba_body(*refs, has_state, n_chunks):
    if has_state:
        (z_ref, xbc_ref, dtwi_ref, conv0_ref, h0_ref, cw_ref, cb_ref, dtb_ref, alog_ref, dskip_ref,
         ng_ref, e_ref, shift_ref, y_ref, hout_ref, tail_scr, ht_scr) = refs
    else:
        (z_ref, xbc_ref, dtwi_ref, cw_ref, cb_ref, dtb_ref, alog_ref, dskip_ref,
         ng_ref, e_ref, shift_ref, y_ref, hout_ref, tail_scr, ht_scr) = refs
    c = pl.program_id(1)
    q = CHUNK
    tail_rows = slice(CONV_PAD - SUBLANES, CONV_PAD)

    @pl.when(c == 0)
    def _():
        tail_scr[...] = jnp.zeros(tail_scr.shape, F32)
        if has_state:
            for b in range(MAMBA_PAIR):
                tail_scr[b, tail_rows, :] = conv0_ref[b]
                ht_scr[b] = h0_ref[b].T
        else:
            ht_scr[...] = jnp.zeros(ht_scr.shape, F32)

    lane = lax.broadcasted_iota(jnp.int32, (q, LANES), 1)
    li = lax.broadcasted_iota(jnp.int32, (q, D_INNER), 0)
    si = lax.broadcasted_iota(jnp.int32, (q, D_INNER), 1) & (q - 1)
    diag = li == si
    causal = (li >= si)[:, :GROUP_CH]
    bdr = lax.broadcasted_iota(jnp.int32, (GROUP_CH, GROUP_CH), 0) // SSM_HEAD_DIM
    bdc = lax.broadcasted_iota(jnp.int32, (GROUP_CH, GROUP_CH), 1) // SSM_HEAD_DIM
    block_diag = bdr == bdc
    neg_a = -jnp.exp(alog_ref[...])

    def sequence(b):
        x = xbc_ref[b]
        hi, mid, lo = _split3(jnp.concatenate([tail_scr[b], x], axis=0))
        delayed = _mm(shift_ref[...], jnp.concatenate([hi, mid, lo], axis=0))
        tail_scr[b, tail_rows, :] = x[q - SUBLANES:q, :]
        pre = dtwi_ref[b] + dtb_ref[...]
        dt = jnp.maximum(pre, 0.0) + jnp.log1p(jnp.exp(-jnp.abs(pre)))
        dt = jnp.where(lane < SSM_HEADS, dt, 0.0)
        cum = _cumsum_rows(dt * neg_a)
        e = e_ref[...]
        ecol = _expand_heads(cum, e)
        dtx = _expand_heads(dt, e)
        yield

        acc = cb_ref[...] + x * cw_ref[CONV_W - 1:CONV_W, :]
        for k in range(CONV_W - 1):
            acc = acc + delayed[k * q:(k + 1) * q, :] * cw_ref[k:k + 1, :]
        xc = _silu(acc)
        xs = xc[:, :D_INNER]
        bm = xc[:, D_INNER:D_INNER + N_GROUPS * D_STATE].astype(BF16)
        cm = xc[:, D_INNER + N_GROUPS * D_STATE:].astype(BF16)
        cbs = [_mm_nt(cm[:, g * D_STATE:(g + 1) * D_STATE], bm[:, g * D_STATE:(g + 1) * D_STATE])
               for g in range(N_GROUPS)]
        yield

        erow = jnp.sum(jnp.where(diag, ecol, 0.0), axis=0, keepdims=True)
        elast = ecol[q - 1:q, :]
        xdt = xs * dtx
        xdec = (xdt * jnp.exp(elast - ecol)).astype(BF16)
        exp_e = jnp.exp(ecol)
        chunk_decay = jnp.exp(elast)
        yield

        ys = []
        for g in range(N_GROUPS):
            sl = slice(g * GROUP_CH, (g + 1) * GROUP_CH)
            nl = slice(g * D_STATE, (g + 1) * D_STATE)
            cbt = jnp.concatenate([cbs[g]] * HEADS_PER_GROUP, axis=1)
            decay = jnp.exp(jnp.where(causal, ecol[:, sl] - erow[:, sl], NEG_BIG))
            m = (cbt * decay).astype(BF16)
            xg = xdt[:, sl]
            bd = jnp.where(block_diag, jnp.concatenate([xg] * HEADS_PER_GROUP, axis=0), 0.0).astype(BF16)
            y_diag = _mm(m, bd)
            ht_g = ht_scr[b, :, sl]
            y_off = _mm(cm[:, nl], ht_g.astype(BF16)) * exp_e[:, sl]
            ys.append(y_diag + y_off)
            ht_scr[b, :, sl] = ht_g * chunk_decay[:, sl] + _mm_tn(bm[:, nl], xdec[:, sl])
            yield

        y = jnp.concatenate(ys, axis=1) + dskip_ref[...] * xs
        y = y * _silu(z_ref[b])
        outs = []
        for g in range(N_GROUPS):
            yg = y[:, g * GROUP_CH:(g + 1) * GROUP_CH]
            outs.append(yg * lax.rsqrt(jnp.mean(yg * yg, axis=-1, keepdims=True) + EPS))
        y_ref[b] = (jnp.concatenate(outs, axis=1) * ng_ref[...]).astype(y_ref.dtype)
        yield

    for _ in zip(*[sequence(b) for b in range(MAMBA_PAIR)]):
        pass

    @pl.when(c == n_chunks - 1)
    def _():
        for b in range(MAMBA_PAIR):
            hout_ref[b] = ht_scr[b].T


def _head_expand_matrix():
    e = np.zeros((LANES, D_INNER), np.float32)
    for h in range(SSM_HEADS):
        e[h, h * SSM_HEAD_DIM:(h + 1) * SSM_HEAD_DIM] = 1.0
    return jnp.asarray(e, BF16)


def _mamba(z, xbc, dtwi, conv0, h0, lw, batch, t):
    nc = t // CHUNK
    assert batch % MAMBA_PAIR == 0
    has_state = h0 is not None
    chunk = lambda b, c: (b, c, 0)
    per_b = lambda b, c: (b, 0, 0)
    pad16 = lambda v: jnp.pad(v.reshape(1, SSM_HEADS), ((0, 0), (0, LANES - SSM_HEADS)))
    small = [lw["conv_w"], lw["conv_b"].reshape(1, CONV_CH), pad16(lw["dt_bias"]), pad16(lw["a_log"]),
             jnp.repeat(lw["d_skip"], SSM_HEAD_DIM).reshape(1, D_INNER),
             lw["ssm_norm_g"].reshape(1, D_INNER), _head_expand_matrix(), _conv_shift_matrix()]
    ins = [a.reshape(batch, t, a.shape[-1]) for a in (z, xbc, dtwi)]
    in_specs = [pl.BlockSpec((MAMBA_PAIR, CHUNK, a.shape[-1]), chunk) for a in ins]
    if has_state:
        ins += [conv0, h0]
        in_specs += [pl.BlockSpec((MAMBA_PAIR, SUBLANES, CONV_CH), per_b),
                     pl.BlockSpec((MAMBA_PAIR, D_INNER, D_STATE), per_b)]
    ins += small
    in_specs += [_resident(a.shape) for a in small]
    y, h_out = pl.pallas_call(
        functools.partial(_mamba_body, has_state=has_state, n_chunks=nc),
        grid=(batch // MAMBA_PAIR, nc),
        in_specs=in_specs,
        out_specs=[pl.BlockSpec((MAMBA_PAIR, CHUNK, D_INNER), chunk),
                   pl.BlockSpec((MAMBA_PAIR, D_INNER, D_STATE), per_b)],
        out_shape=[jax.ShapeDtypeStruct((batch, t, D_INNER), BF16),
                   jax.ShapeDtypeStruct((batch, D_INNER, D_STATE), F32)],
        scratch_shapes=[pltpu.VMEM((MAMBA_PAIR, CONV_PAD, CONV_CH), F32),
                        pltpu.VMEM((MAMBA_PAIR, D_STATE, D_INNER), F32)],
        compiler_params=_params(2),
        name="mamba",
    )(*ins)
    return y.reshape(batch * t, D_INNER), h_out


DSA_TQ = LANES
DSA_KB = 512
IDX_CAT = 4 * IDX_DIM
DSA_OT_ROWS = 2 * DSA_HEAD_DIM


def _hi_lo(x):
    hi = x.astype(BF16).astype(F32)
    return hi, (x - hi).astype(BF16).astype(F32)


def _dsa_body(qb_ref, qi_ref, dtwi_ref, ki_ref, k_ref, v_ref, o_ref,
              kcat_scr, kbf_scr, vt_scr, tri_scr, key_scr, mask_scr, s_scr,
              *, n_kb, n_valid, q_pos0, topk, visible):
    tq, kb_rows = DSA_TQ, DSA_KB
    j = pl.program_id(1)

    @pl.when(j == 0)
    def _():
        hi, lo = _hi_lo(ki_ref[...])
        kcat_scr[...] = jnp.concatenate([hi, lo, hi, jnp.zeros_like(hi)], axis=1).astype(BF16)
        kbf_scr[...] = k_ref[...].astype(BF16)
        ones = jnp.ones((DSA_HEAD_DIM, kb_rows), F32)
        for kb in range(n_kb):
            vt = v_ref[kb * kb_rows:(kb + 1) * kb_rows, :].T
            for g in range(KV_HEADS):
                vt_g = vt[g * DSA_HEAD_DIM:(g + 1) * DSA_HEAD_DIM, :]
                vt_scr[g, kb] = jnp.concatenate([vt_g, ones], axis=0).astype(BF16)
        tr = lax.broadcasted_iota(jnp.int32, (kb_rows, kb_rows), 0)
        tc = lax.broadcasted_iota(jnp.int32, (kb_rows, kb_rows), 1)
        tri_scr[...] = jnp.where(tc < tr, 1.0, 0.0).astype(BF16)

    qpos = q_pos0 + j * tq + lax.broadcasted_iota(jnp.int32, (1, tq), 1)
    q_end = jnp.minimum(((qpos >> 6) + 1) << 6, n_valid)
    k_eff = jnp.minimum(q_end, topk).astype(F32)
    last_end = jnp.minimum((((q_pos0 + (j + 1) * tq - 1) >> 6) + 1) << 6, n_valid)
    nkb = (last_end + (kb_rows - 1)) // kb_rows

    wit = (dtwi_ref[...] * (IDX_DIM ** -0.5 * IDX_HEADS ** -0.5)).T
    qi = qi_ref[...]
    qparts = []
    for h in range(IDX_HEADS):
        hi, lo = _hi_lo(qi[:, h * IDX_DIM:(h + 1) * IDX_DIM])
        qparts.append(jnp.concatenate([hi, hi, lo, jnp.zeros_like(hi)], axis=1))
    qcat = jnp.concatenate(qparts, axis=0).astype(BF16)
    krow = lax.broadcasted_iota(jnp.int32, (kb_rows, 1), 0)
    qb = qb_ref[...]
    q4 = [jnp.concatenate([qb[:, (g * KV_REP + r) * DSA_HEAD_DIM:(g * KV_REP + r + 1) * DSA_HEAD_DIM]
                           for r in range(KV_REP)], axis=0) for g in range(KV_HEADS)]

    def run(n_vis):
        for kb in range(n_vis):
            logit = _mm_nt(kcat_scr[kb * kb_rows:(kb + 1) * kb_rows, :], qcat)
            sc = jnp.zeros((kb_rows, tq), F32)
            for h in range(IDX_HEADS):
                sc = sc + jnp.maximum(logit[:, h * tq:(h + 1) * tq], 0.0) * wit[WI_LANE + h:WI_LANE + h + 1, :]
            sc = jnp.where(kb * kb_rows + krow < q_end, sc, -jnp.inf)
            bits = lax.bitcast_convert_type(sc, jnp.int32)
            key_scr[kb] = jnp.where(bits < 0, bits ^ jnp.int32(0x7FFFFFFF), bits)

        def count(pred):
            acc = _fold_rows(jnp.where(pred(key_scr[0]), 1.0, 0.0), jnp.add)
            for kb in range(1, n_vis):
                acc = acc + _fold_rows(jnp.where(pred(key_scr[kb]), 1.0, 0.0), jnp.add)
            return jnp.sum(acc, axis=0, keepdims=True)

        def radix_step(i, tu):
            cand = tu | lax.shift_left(jnp.int32(1), 31 - i)
            thr_c = cand ^ jnp.int32(INT_MIN)
            return jnp.where(count(lambda k: k >= thr_c) >= k_eff, cand, tu)

        tu = lax.fori_loop(0, 32, radix_step, jnp.zeros((1, tq), jnp.int32))
        thr = tu ^ jnp.int32(INT_MIN)

        surplus = jnp.max(count(lambda k: k >= thr) - k_eff)

        @pl.when(surplus <= 0.0)
        def _():
            for kb in range(n_vis):
                mask_scr[kb] = jnp.where(key_scr[kb] >= thr, 0.0, NEG_BIG)

        @pl.when(surplus > 0.0)
        def _():
            need = k_eff - count(lambda k: k > thr)
            before = jnp.zeros((1, tq), F32)
            for kb in range(n_vis):
                key = key_scr[kb]
                eq = jnp.where(key == thr, 1.0, 0.0)
                rank = _mm(tri_scr[...], eq.astype(BF16)) + before
                take = jnp.where(key > thr, 1.0, jnp.where(rank < need, eq, 0.0))
                mask_scr[kb] = jnp.where(take > 0.0, 0.0, NEG_BIG)
                before = before + jnp.sum(_fold_rows(eq, jnp.add), axis=0, keepdims=True)

        m8 = [jnp.full((SUBLANES, KV_REP * tq), NEG_BIG, F32) for _ in range(KV_HEADS)]
        for kb in range(n_vis):
            mask4 = jnp.concatenate([mask_scr[kb]] * KV_REP, axis=1)
            for g in range(KV_HEADS):
                kg = kbf_scr[kb * kb_rows:(kb + 1) * kb_rows, g * DSA_HEAD_DIM:(g + 1) * DSA_HEAD_DIM]
                st = _mm_nt(kg, q4[g]) + mask4
                s_scr[kb, g] = st
                m8[g] = jnp.maximum(m8[g], _fold_rows(st, jnp.maximum))
        m = [jnp.max(m8[g], axis=0, keepdims=True) for g in range(KV_HEADS)]
        acc = [None] * KV_HEADS
        for kb in range(n_vis):
            for g in range(KV_HEADS):
                p = jnp.exp2(s_scr[kb, g] - m[g]).astype(BF16)
                pv = _mm(vt_scr[g, kb], p)
                acc[g] = pv if acc[g] is None else acc[g] + pv
        outs = []
        for g in range(KV_HEADS):
            ot = acc[g][0:DSA_HEAD_DIM, :] / acc[g][DSA_HEAD_DIM:DSA_HEAD_DIM + 1, :]
            outs += [ot[:, r * tq:(r + 1) * tq].T for r in range(KV_REP)]
        o_ref[...] = jnp.concatenate(outs, axis=1).astype(o_ref.dtype)

    if len(visible) == 1:
        run(visible[0])
    else:
        for n_vis in visible:
            pl.when(nkb == n_vis)(functools.partial(run, n_vis))


def _dsa(qb, qi, dtwi, ki_all, k_all, v_all, key_layer, batch, t, s_keys, n_valid, q_pos0):
    tq, kb_rows = DSA_TQ, DSA_KB
    assert t % tq == 0 and s_keys % kb_rows == 0
    nq = t // tq
    n_kb = s_keys // kb_rows
    topk = min(TOPK_MAX, n_valid // 4)
    qrow = lambda b, j: (b * nq + j, 0)
    krow = lambda b, j: (key_layer, b, 0)

    def visible_blocks(j):
        last_end = min(((((q_pos0 + (j + 1) * tq - 1) >> 6) + 1) << 6), n_valid)
        return (last_end + kb_rows - 1) // kb_rows

    visible = tuple(sorted({visible_blocks(j) for j in range(nq)}))
    return pl.pallas_call(
        functools.partial(_dsa_body, n_kb=n_kb, n_valid=n_valid, q_pos0=q_pos0, topk=topk, visible=visible),
        grid=(batch, nq),
        in_specs=[pl.BlockSpec((tq, D_MODEL), qrow), pl.BlockSpec((tq, IDX_HEADS * IDX_DIM), qrow),
                  pl.BlockSpec((tq, LANES), qrow), pl.BlockSpec((None, s_keys, IDX_DIM), krow),
                  pl.BlockSpec((None, s_keys, KV_DIM), krow), pl.BlockSpec((None, s_keys, KV_DIM), krow)],
        out_specs=pl.BlockSpec((tq, D_MODEL), qrow),
        out_shape=jax.ShapeDtypeStruct((batch * t, D_MODEL), BF16),
        scratch_shapes=[pltpu.VMEM((s_keys, IDX_CAT), BF16), pltpu.VMEM((s_keys, KV_DIM), BF16),
                        pltpu.VMEM((KV_HEADS, n_kb, DSA_OT_ROWS, kb_rows), BF16),
                        pltpu.VMEM((kb_rows, kb_rows), BF16),
                        pltpu.VMEM((n_kb, kb_rows, tq), jnp.int32), pltpu.VMEM((n_kb, kb_rows, tq), F32),
                        pltpu.VMEM((n_kb, KV_HEADS, kb_rows, KV_REP * tq), F32)],
        compiler_params=_params(2),
        name="dsa",
    )(qb, qi, dtwi, ki_all, k_all, v_all)


def _band_body(*refs, tq, sub, n_kblk, q_pos0, k_min, clamped):
    q_ref = refs[0]
    k_refs = refs[1:1 + n_kblk]
    v_refs = refs[1 + n_kblk:1 + 2 * n_kblk]
    vec_ref, o_ref, bias_scr = refs[1 + 2 * n_kblk:]
    w = sum(r.shape[0] for r in k_refs)
    wsub = w - tq + sub
    i = pl.program_id(1)

    @pl.when((pl.program_id(0) == 0) & (i == 0))
    def _():
        r = lax.broadcasted_iota(jnp.int32, (sub, wsub), 0)
        c = lax.broadcasted_iota(jnp.int32, (sub, wsub), 1)
        dchunk = (r >> 6) + (wsub - sub) // CHUNK - (c >> 6)
        band_mask = jnp.where((dchunk >= 0) & (dchunk <= LEFT_CHUNKS), 0.0, NEG_BIG)
        for h in range(BAND_HEADS):
            rows = jnp.broadcast_to(vec_ref[h:h + 1, :], (sub, vec_ref.shape[1]))
            toeplitz = pltpu.roll(rows, 0, 1, stride=1, stride_axis=0)[:, :wsub]
            bias_scr[h // 2, (h % 2) * sub:(h % 2 + 1) * sub, :] = toeplitz * LOG2E + band_mask

    lane = lax.broadcasted_iota(jnp.int32, (sub, LANES), 1)
    first_head = lane < BAND_HEAD_DIM
    keep_a = jnp.where(first_head, 1.0, 0.0).astype(BF16)
    keep_b = jnp.where(first_head, 0.0, 1.0).astype(BF16)
    ones = jnp.ones((w, LANES), BF16)
    n_sub = tq // sub

    def all_rows(blocks, lanes):
        parts = [blk[:, lanes] for blk in blocks]
        return parts[0] if len(parts) == 1 else jnp.concatenate(parts, axis=0)

    def scores(hp):
        lanes = slice(hp * LANES, (hp + 1) * LANES)
        q2 = []
        for c2 in range(n_sub):
            qp = q_ref[c2 * sub:(c2 + 1) * sub, lanes]
            q2 += [qp * keep_a, qp * keep_b]
        return _mm_nt(jnp.concatenate(q2, axis=0), all_rows(k_refs, lanes))

    def attend(hp, s_full, mask_missing_keys):
        lanes = slice(hp * LANES, (hp + 1) * LANES)
        ps = []
        for c2 in range(n_sub):
            s = s_full[c2 * 2 * sub:(c2 + 1) * 2 * sub, c2 * sub:c2 * sub + wsub] + bias_scr[hp]
            if mask_missing_keys:
                kpos = q_pos0 + i * tq + c2 * sub + (sub - wsub) + lax.broadcasted_iota(jnp.int32, (1, wsub), 1)
                s = s + jnp.where(kpos >= k_min, 0.0, NEG_BIG)
            p = jnp.exp2(s - jnp.max(s, axis=1, keepdims=True)).astype(BF16)
            pad = [jnp.zeros((2 * sub, c2 * sub), BF16)] if c2 else []
            pad_r = [jnp.zeros((2 * sub, w - wsub - c2 * sub), BF16)] if w - wsub - c2 * sub else []
            ps.append(jnp.concatenate(pad + [p] + pad_r, axis=1) if pad or pad_r else p)
        p_full = ps[0] if n_sub == 1 else jnp.concatenate(ps, axis=0)
        o = _mm(p_full, jnp.concatenate([all_rows(v_refs, lanes), ones], axis=1))
        for c2 in range(n_sub):
            ra, rb = c2 * 2 * sub, c2 * 2 * sub + sub
            oa = o[ra:ra + sub, :LANES] / o[ra:ra + sub, LANES:LANES + 1]
            ob = o[rb:rb + sub, :LANES] / o[rb:rb + sub, LANES:LANES + 1]
            o_ref[c2 * sub:(c2 + 1) * sub, lanes] = jnp.where(first_head, oa, ob).astype(o_ref.dtype)

    def heads(mask_missing_keys):
        n_pairs, ahead, pending = BAND_HEADS // 2, 2, {}
        for n in range(n_pairs + ahead):
            if n < n_pairs:
                pending[n] = scores(n)
            if n >= ahead:
                attend(n - ahead, pending.pop(n - ahead), mask_missing_keys)

    if clamped:
        first_full = -(-(w - tq) // tq)
        pl.when(i < first_full)(functools.partial(heads, True))
        pl.when(i >= first_full)(functools.partial(heads, False))
    else:
        heads(False)


def _band_bias_vec(rel_bias, sub, wsub):
    l = -(-(wsub + sub) // LANES) * LANES
    m = np.arange(l)
    d = np.where(m < wsub, m, m - l)
    rel = np.clip(wsub - sub - d, -REL_CLIP, REL_CLIP) + REL_CLIP
    return rel_bias[jnp.asarray(rel)].T


def _band(q, k, v, rel_bias, batch, t, tq, sub, k_block_rows, n_kblk, q_pos0, k_min, clamped):
    nq = t // tq
    w = n_kblk * k_block_rows
    wsub = w - tq + sub
    assert wsub % LANES == 0 and (wsub - sub) % CHUNK == 0
    qrow = lambda b, i: (b * nq + i, 0)

    def krow(off):
        if clamped:
            return lambda b, i: (b * nq + jnp.maximum(i - (n_kblk - 1) + off, 0), 0)
        return lambda b, i: (b * n_kblk + off, 0)

    vec = _band_bias_vec(rel_bias, sub, wsub)
    kspecs = [pl.BlockSpec((k_block_rows, D_MODEL), krow(o)) for o in range(n_kblk)]
    return pl.pallas_call(
        functools.partial(_band_body, tq=tq, sub=sub, n_kblk=n_kblk, q_pos0=q_pos0, k_min=k_min, clamped=clamped),
        grid=(batch, nq),
        in_specs=[pl.BlockSpec((tq, D_MODEL), qrow)] + kspecs + kspecs + [_resident(vec.shape)],
        out_specs=pl.BlockSpec((tq, D_MODEL), qrow),
        out_shape=jax.ShapeDtypeStruct((batch * t, D_MODEL), BF16),
        scratch_shapes=[pltpu.VMEM((BAND_HEADS // 2, 2 * sub, wsub), F32)],
        compiler_params=_params(2),
        name="band",
    )(q, *([k] * n_kblk), *([v] * n_kblk), vec)


def _merge_body(x_ref, ya_ref, yb_ref, yc_ref, g_ref, wg_ref, bg_ref, wbr_ref, wo_ref, o_ref):
    x = x_ref[...]
    u = _rms(x, g_ref[2:3, :]).astype(BF16)
    mix = jnp.zeros(x.shape, F32)
    for k, y_ref in enumerate((ya_ref, yb_ref, yc_ref)):
        sl = slice(k * D_MODEL, (k + 1) * D_MODEL)
        gate = _sigmoid(_mm(u, wg_ref[:, sl]) + bg_ref[:, sl])
        mix = mix + gate * _mm(y_ref[...], wbr_ref[k])
    o_ref[...] = x + _rms(_mm(mix.astype(BF16), wo_ref[...]), g_ref[3:4, :])


def _merge(x, ya, yb, yc, pw, layer):
    n = x.shape[0]
    tm = _row_tile(n, 256)
    row = lambda i: (i, 0)
    tile = pl.BlockSpec((tm, D_MODEL), row)
    return pl.pallas_call(
        _merge_body,
        grid=(n // tm,),
        in_specs=[tile, tile, tile, tile]
        + [_resident_slice(pw[k], (layer,)) for k in ("g", "wg", "bg", "wbr", "wo")],
        out_specs=tile,
        out_shape=jax.ShapeDtypeStruct((n, D_MODEL), F32),
        compiler_params=_params(1),
        name="merge",
    )(x, ya, yb, yc, pw["g"], pw["wg"], pw["bg"], pw["wbr"], pw["wo"])


def _ple_body(x_ref, p_ref, g_ref, wp_ref, wpg_ref, o_ref):
    x = x_ref[...]
    e = _mm(p_ref[...].astype(BF16), wp_ref[...])
    pg = _sigmoid(_mm(_rms(x, g_ref[6:7, :]).astype(BF16), wpg_ref[...]))
    o_ref[...] = x + _rms(pg * e, g_ref[7:8, :])


def _ple(x, p, pw, layer):
    n = x.shape[0]
    tm = _row_tile(n, 512)
    row = lambda i: (i, 0)
    return pl.pallas_call(
        _ple_body,
        grid=(n // tm,),
        in_specs=[pl.BlockSpec((tm, D_MODEL), row), pl.BlockSpec((tm, PLE_DIM), row)]
        + [_resident_slice(pw[k], (layer,)) for k in ("g", "wp", "wpg")],
        out_specs=pl.BlockSpec((tm, D_MODEL), row),
        out_shape=jax.ShapeDtypeStruct((n, D_MODEL), F32),
        compiler_params=_params(1),
        name="ple",
    )(x, p, pw["g"], pw["wp"], pw["wpg"])


BAND_TQ = 256
BAND_SUB = 128
BAND_KBLK = 1 + -(-BAND // BAND_TQ)


def _prep_weights(norm_g, ffn_w13, ffn_w2, w_in, w_gate, b_gate, w_branch, w_out, w_ple, w_ple_gate):
    bf = lambda a: a.astype(BF16)
    return {"g": norm_g, "w13": bf(ffn_w13), "w2": bf(ffn_w2), "w_in": _pack_w_in(w_in), "wg": bf(w_gate),
            "bg": b_gate.reshape(DEPTH, 1, N_BRANCH * D_MODEL), "wbr": bf(w_branch), "wo": bf(w_out),
            "wp": bf(w_ple), "wpg": bf(w_ple_gate)}


def _trunk_layer(x, p, w, pw, layer, cache, batch, t, shared):
    x = _ffn(x, pw, layer, 0, 0, 1)
    pr, shared = _inproj(x, pw, layer, batch, t, shared)
    if cache is None:
        ya, h_new = _mamba(pr["z"], pr["xbc"], pr["dtwi"], None, None, w, batch, t)
        yb = _dsa(pr["qb"], pr["qi"], pr["dtwi"], pr["ki"], pr["kb"], pr["vb"], layer, batch, t, t, t, 0)
        yc = _band(pr["qc"], pr["kc_bf"], pr["vc_bf"], w["rel_bias"], batch, t, BAND_TQ, BAND_SUB, BAND_TQ,
                   BAND_KBLK, 0, 0, True)
        conv_src = pr["xbc"].reshape(batch, t, CONV_CH)
    else:
        past = cache["dsa_k"].shape[1]
        conv0 = jnp.pad(cache["conv"], ((0, 0), (SUBLANES - (CONV_W - 1), 0), (0, 0)))
        h0 = cache["ssm"].reshape(batch, D_INNER, D_STATE)
        ya, h_new = _mamba(pr["z"], pr["xbc"], pr["dtwi"], conv0, h0, w, batch, t)

        n_valid = past + t
        s_keys = -(-n_valid // DSA_KB) * DSA_KB
        tq_pad = -(-t // DSA_TQ) * DSA_TQ

        def with_cache(c, new):
            width = new.shape[-1]
            a = jnp.concatenate([c.reshape(batch, past, width), new[layer].reshape(batch, t, width)], axis=1)
            return jnp.pad(a, ((0, 0), (0, s_keys - n_valid), (0, 0))).reshape(1, batch * s_keys, width)

        def pad_q(a):
            a = jnp.pad(a.reshape(batch, t, a.shape[-1]), ((0, 0), (0, tq_pad - t), (0, 0)))
            return a.reshape(batch * tq_pad, a.shape[-1])

        yb = _dsa(pad_q(pr["qb"]), pad_q(pr["qi"]), pad_q(pr["dtwi"]), with_cache(cache["idx_k"], pr["ki"]),
                  with_cache(cache["dsa_k"], pr["kb"]), with_cache(cache["dsa_v"], pr["vb"]),
                  0, batch, tq_pad, s_keys, n_valid, past)
        yb = yb.reshape(batch, tq_pad, D_MODEL)[:, :t].reshape(batch * t, D_MODEL)

        nrows = cache["band_k"].shape[1]
        k_rows = -(-(nrows + t) // LANES) * LANES
        lead = k_rows - nrows - t

        def with_band(c, new):
            a = jnp.concatenate([c.reshape(batch, nrows, D_MODEL).astype(BF16), new.reshape(batch, t, D_MODEL)],
                                axis=1)
            return jnp.pad(a, ((0, 0), (lead, 0), (0, 0))).reshape(batch * k_rows, D_MODEL)

        yc = _band(pr["qc"], with_band(cache["band_k"], pr["kc_bf"]), with_band(cache["band_v"], pr["vc_bf"]),
                   w["rel_bias"], batch, t, t, t, k_rows, 1, past, past - nrows, False)
        conv_src = jnp.concatenate([cache["conv"], pr["xbc"].reshape(batch, t, CONV_CH)], axis=1)

    x = _merge(x, ya, yb, yc, pw, layer)
    x = _ffn(x, pw, layer, 1, 4, 5)
    x = _ple(x, p.reshape(batch * t, PLE_DIM), pw, layer)

    state = (h_new.reshape(batch, SSM_HEADS, SSM_HEAD_DIM, D_STATE), conv_src[:, -(CONV_W - 1):])
    return x, state, shared


def _cache_outputs(shared, states, batch, t):
    band_rows = min(BAND, t)
    return (shared["kb"].reshape(DEPTH, batch, t, KV_HEADS, DSA_HEAD_DIM),
            shared["vb"].reshape(DEPTH, batch, t, KV_HEADS, DSA_HEAD_DIM),
            shared["ki"].reshape(DEPTH, batch, t, IDX_DIM),
            shared["kc"].reshape(DEPTH, batch, band_rows, BAND_HEADS, BAND_HEAD_DIM),
            shared["vc"].reshape(DEPTH, batch, band_rows, BAND_HEADS, BAND_HEAD_DIM),
            jnp.stack([s[0] for s in states]), jnp.stack([s[1] for s in states]))


def kernel(x_prompt, x_sample, p_prompt, p_sample, cache_dsa_k, cache_dsa_v, cache_idx_k, cache_band_k,
           cache_band_v, state_ssm, state_conv, norm_g, ffn_w13, ffn_w2, w_in, conv_w, conv_b, dt_bias,
           a_log, d_skip, ssm_norm_g, rel_bias, w_gate, b_gate, w_branch, w_out, w_ple, w_ple_gate):
    bp, tp, _ = x_prompt.shape
    bs, ts, _ = x_sample.shape
    yp = x_prompt.reshape(bp * tp, D_MODEL)
    ys = x_sample.reshape(bs * ts, D_MODEL)
    st_p, st_s, shared_p, shared_s = [], [], None, None
    pw = _prep_weights(norm_g, ffn_w13, ffn_w2, w_in, w_gate, b_gate, w_branch, w_out, w_ple, w_ple_gate)
    for i in range(DEPTH):
        w = {"conv_w": conv_w[i], "conv_b": conv_b[i], "dt_bias": dt_bias[i], "a_log": a_log[i],
             "d_skip": d_skip[i], "ssm_norm_g": ssm_norm_g[i], "rel_bias": rel_bias[i]}
        yp, sp, shared_p = _trunk_layer(yp, p_prompt[i], w, pw, i, None, bp, tp, shared_p)
        st_p.append(sp)
        cache = {"dsa_k": cache_dsa_k[i], "dsa_v": cache_dsa_v[i], "idx_k": cache_idx_k[i],
                 "band_k": cache_band_k[i], "band_v": cache_band_v[i], "ssm": state_ssm[i], "conv": state_conv[i]}
        ys, ss, shared_s = _trunk_layer(ys, p_sample[i], w, pw, i, cache, bs, ts, shared_s)
        st_s.append(ss)
    return (yp.reshape(bp, tp, D_MODEL), ys.reshape(bs, ts, D_MODEL),
            *_cache_outputs(shared_p, st_p, bp, tp), *_cache_outputs(shared_s, st_s, bs, ts))
```

```python
import functools
import math

import numpy as np
import jax
import jax.numpy as jnp
from jax import lax
from jax.experimental import pallas as pl
from jax.experimental.pallas import tpu as pltpu

F32 = jnp.float32
BF16 = jnp.bfloat16

D_MODEL = 1024
DEPTH = 2
CHUNK = 64
EPS = 1e-6
HALF = 0.5
D_FF = 2816
PLE_DIM = 256
SSM_HEAD_DIM = 64
D_INNER = D_MODEL
SSM_HEADS = D_INNER // SSM_HEAD_DIM
N_GROUPS = 4
HEADS_PER_GROUP = SSM_HEADS // N_GROUPS
D_STATE = 128
CONV_W = 4
CONV_CH = D_INNER + 2 * N_GROUPS * D_STATE
DSA_HEAD_DIM = 128
DSA_HEADS = D_MODEL // DSA_HEAD_DIM
KV_HEADS = 2
KV_REP = DSA_HEADS // KV_HEADS
IDX_HEADS = 4
IDX_DIM = 64
TOPK_MAX = 256
BAND_HEAD_DIM = 64
BAND_HEADS = D_MODEL // BAND_HEAD_DIM
LEFT_CHUNKS = 8
BAND = LEFT_CHUNKS * CHUNK
REL_CLIP = 256
N_BRANCH = 3
IN_WIDTHS = (D_INNER, CONV_CH, SSM_HEADS,
             DSA_HEADS * DSA_HEAD_DIM, KV_HEADS * DSA_HEAD_DIM, KV_HEADS * DSA_HEAD_DIM,
             IDX_HEADS * IDX_DIM, IDX_DIM, IDX_HEADS,
             BAND_HEADS * BAND_HEAD_DIM, BAND_HEADS * BAND_HEAD_DIM, BAND_HEADS * BAND_HEAD_DIM)
IN_SPLITS = tuple(int(s) for s in np.cumsum(IN_WIDTHS)[:-1])

LANES = 128
SUBLANES = 8
KV_DIM = KV_HEADS * DSA_HEAD_DIM
GROUP_CH = D_INNER // N_GROUPS
NEG_BIG = -1e30
INT_MIN = -2 ** 31
LOG2E = math.log2(math.e)
VMEM_LIMIT = 56 * 1024 * 1024


def _mm(a, b):
    return jnp.dot(a, b, preferred_element_type=F32)


def _mm_nt(a, b):
    return lax.dot_general(a, b, (((1,), (1,)), ((), ())), preferred_element_type=F32)


def _mm_tn(a, b):
    return lax.dot_general(a, b, (((0,), (0,)), ((), ())), preferred_element_type=F32)


def _rms(x, g):
    return x * lax.rsqrt(jnp.mean(x * x, axis=-1, keepdims=True) + EPS) * g


def _sigmoid(x):
    return 1.0 / (1.0 + jnp.exp(-x))


def _silu(x):
    return x * _sigmoid(x)


def _resident(shape):
    return pl.BlockSpec(shape, lambda *_: (0,) * len(shape), pipeline_mode=pl.Buffered(1))


def _resident_slice(arr, lead, block=None, at=None):
    tail = tuple(arr.shape[len(lead):]) if block is None else tuple(block)
    idx = tuple(lead) + ((0,) * len(tail) if at is None else tuple(at))
    return pl.BlockSpec((None,) * len(lead) + tail, lambda *_: idx, pipeline_mode=pl.Buffered(1))


def _params(n_grid_dims):
    return pltpu.CompilerParams(dimension_semantics=("arbitrary",) * n_grid_dims,
                                vmem_limit_bytes=VMEM_LIMIT)


def _row_tile(n_rows, want):
    t = min(want, n_rows)
    assert n_rows % t == 0
    return t


def _fold_rows(x, op):
    parts = [x[i * SUBLANES:(i + 1) * SUBLANES] for i in range(x.shape[0] // SUBLANES)]
    while len(parts) > 1:
        parts = [op(parts[i], parts[i + 1]) for i in range(0, len(parts) - 1, 2)] + parts[len(parts) & ~1:]
    return parts[0]


FF_CHUNK = 256


def _ffn_body(x_ref, g_ref, wa_ref, wb_ref, w2_ref, o_ref, *, g_pre, g_post):
    x = x_ref[...]
    u = _rms(x, g_ref[g_pre:g_pre + 1, :]).astype(BF16)
    acc = jnp.zeros(x.shape, F32)
    for c in range(D_FF // FF_CHUNK):
        sl = slice(c * FF_CHUNK, (c + 1) * FF_CHUNK)
        a = _mm(u, wa_ref[:, sl])
        b = _mm(u, wb_ref[:, sl])
        acc = acc + _mm((_silu(a) * b).astype(BF16), w2_ref[sl, :])
    o_ref[...] = x + HALF * _rms(acc, g_ref[g_post:g_post + 1, :])


def _ffn(x, pw, layer, j, g_pre, g_post):
    n = x.shape[0]
    tm = _row_tile(n, 512)
    row = lambda i: (i, 0)
    half = (D_MODEL, D_FF)
    return pl.pallas_call(
        functools.partial(_ffn_body, g_pre=g_pre, g_post=g_post),
        grid=(n // tm,),
        in_specs=[pl.BlockSpec((tm, D_MODEL), row), _resident_slice(pw["g"], (layer,)),
                  _resident_slice(pw["w13"], (layer, j), half, (0, 0)),
                  _resident_slice(pw["w13"], (layer, j), half, (0, 1)),
                  _resident_slice(pw["w2"], (layer, j))],
        out_specs=pl.BlockSpec((tm, D_MODEL), row),
        out_shape=jax.ShapeDtypeStruct((n, D_MODEL), F32),
        compiler_params=_params(1),
        name="ffn",
    )(x, pw["g"], pw["w13"], pw["w13"], pw["w2"])


_INPROJ_GROUPS = (D_INNER, CONV_CH, D_MODEL, KV_DIM, KV_DIM, IDX_HEADS * IDX_DIM, D_MODEL, D_MODEL, D_MODEL,
                  LANES, LANES)
_INPROJ_OUT = (
    ("z", 0, D_INNER, F32, None, "layer"),
    ("xbc", 1, CONV_CH, F32, None, "layer"),
    ("qb", 2, D_MODEL, BF16, DSA_HEAD_DIM ** -0.5 * LOG2E, "layer"),
    ("kb", 3, KV_DIM, F32, None, "stack"),
    ("vb", 4, KV_DIM, F32, None, "stack"),
    ("qi", 5, IDX_HEADS * IDX_DIM, F32, None, "layer"),
    ("qc", 6, D_MODEL, BF16, BAND_HEAD_DIM ** -0.5 * LOG2E, "layer"),
    ("kc", 7, D_MODEL, F32, None, "tail"),
    ("kc_bf", 7, D_MODEL, BF16, None, "layer"),
    ("vc", 8, D_MODEL, F32, None, "tail"),
    ("vc_bf", 8, D_MODEL, BF16, None, "layer"),
    ("ki", 9, IDX_DIM, F32, None, "stack"),
    ("dtwi", 10, LANES, F32, None, "layer"),
)
WI_LANE = SSM_HEADS


def _pack_w_in(w_in):
    z, xbc, dt, qb, kb, vb, qi, ki, wi, qc, kc, vc = jnp.split(w_in, IN_SPLITS, axis=-1)
    pad = lambda w: jnp.pad(w, ((0, 0), (0, 0), (0, LANES - w.shape[-1])))
    cols = [z, xbc, qb, kb, vb, qi, qc, kc, vc, pad(ki), pad(jnp.concatenate([dt, wi], axis=-1))]
    return jnp.concatenate(cols, axis=-1).astype(BF16)


def _inproj_body(x_ref, g_ref, w_ref, *refs):
    out_refs = refs[len(refs) - len(_INPROJ_OUT):]
    u = _rms(x_ref[...], g_ref[2:3, :]).astype(BF16)
    starts = np.concatenate([[0], np.cumsum(_INPROJ_GROUPS)])
    for grp, width in enumerate(_INPROJ_GROUPS):
        r = _mm(u, w_ref[:, int(starts[grp]):int(starts[grp]) + width])
        for (_, og, stored, dtype, scale, _), o_ref in zip(_INPROJ_OUT, out_refs):
            if og == grp:
                v = r if scale is None else r * scale
                o_ref[...] = (v if stored == width else v[:, :stored]).astype(dtype)


def _inproj(x, pw, layer, batch, t, shared):
    n = x.shape[0]
    tm = _row_tile(n, 256)
    tail = min(BAND, t)
    row = lambda i: (i, 0)
    stack_row = lambda i: (layer, i, 0)
    if tail == t:
        tail_row = stack_row
    else:
        assert t % tm == 0 and tail % tm == 0
        per_seq, per_tail = t // tm, tail // tm
        tail_row = lambda i: (layer, (i // per_seq) * per_tail + jnp.maximum(i % per_seq - (per_seq - per_tail), 0), 0)
    out_specs, out_shape, stacked = [], [], []
    for k, (name, _, width, dtype, _, kind) in enumerate(_INPROJ_OUT):
        if kind == "layer":
            out_specs.append(pl.BlockSpec((tm, width), row))
            out_shape.append(jax.ShapeDtypeStruct((n, width), dtype))
        else:
            rows = n if kind == "stack" else batch * tail
            out_specs.append(pl.BlockSpec((None, tm, width), stack_row if kind == "stack" else tail_row))
            out_shape.append(jax.ShapeDtypeStruct((DEPTH, rows, width), dtype))
            stacked.append((name, k))
    carried = [] if shared is None else [shared[name] for name, _ in stacked]
    aliases = {} if shared is None else {3 + a: k for a, (_, k) in enumerate(stacked)}
    outs = pl.pallas_call(
        _inproj_body,
        grid=(n // tm,),
        in_specs=[pl.BlockSpec((tm, D_MODEL), row), _resident_slice(pw["g"], (layer,)),
                  _resident_slice(pw["w_in"], (layer,))] + [pl.BlockSpec(memory_space=pl.ANY)] * len(carried),
        out_specs=out_specs,
        out_shape=out_shape,
        input_output_aliases=aliases,
        compiler_params=_params(1),
        name="inproj",
    )(x, pw["g"], pw["w_in"], *carried)
    pr = {o[0]: a for o, a in zip(_INPROJ_OUT, outs)}
    return pr, {name: pr[name] for name, _ in stacked}


CONV_PAD = 16
CONV_ROWS = CONV_PAD + CHUNK
MAMBA_PAIR = 4


def _split3(x):
    hi = x.astype(BF16)
    r = x - hi.astype(F32)
    mid = r.astype(BF16)
    lo = (r - mid.astype(F32)).astype(BF16)
    return hi, mid, lo


def _expand_heads(x, e):
    hi, mid, lo = _split3(x)
    return _mm(hi, e) + _mm(mid, e) + _mm(lo, e)


def _cumsum_rows(x):
    n = x.shape[0]
    row = lax.broadcasted_iota(jnp.int32, x.shape, 0)
    d = 1
    while d < n:
        x = x + jnp.where(row >= d, pltpu.roll(x, d, 0), 0.0)
        d *= 2
    return x


def _conv_shift_matrix():
    s = np.zeros(((CONV_W - 1) * CHUNK, 3 * CONV_ROWS), np.float32)
    for k in range(CONV_W - 1):
        for l in range(CHUNK):
            for part in range(3):
                s[k * CHUNK + l, part * CONV_ROWS + CONV_PAD - (CONV_W - 1) + k + l] = 1.0
    return jnp.asarray(s, BF16)


def _mamba_body(*refs, has_state, n_chunks):
    if has_state:
        (z_ref, xbc_ref, dtwi_ref, conv0_ref, h0_ref, cw_ref, cb_ref, dtb_ref, alog_ref, dskip_ref,
         ng_ref, e_ref, shift_ref, y_ref, hout_ref, tail_scr, ht_scr) = refs
    else:
        (z_ref, xbc_ref, dtwi_ref, cw_ref, cb_ref, dtb_ref, alog_ref, dskip_ref,
         ng_ref, e_ref, shift_ref, y_ref, hout_ref, tail_scr, ht_scr) = refs
    c = pl.program_id(1)
    q = CHUNK
    tail_rows = slice(CONV_PAD - SUBLANES, CONV_PAD)

    @pl.when(c == 0)
    def _():
        tail_scr[...] = jnp.zeros(tail_scr.shape, F32)
        if has_state:
            for b in range(MAMBA_PAIR):
                tail_scr[b, tail_rows, :] = conv0_ref[b]
                ht_scr[b] = h0_ref[b].T
        else:
            ht_scr[...] = jnp.zeros(ht_scr.shape, F32)

    lane = lax.broadcasted_iota(jnp.int32, (q, LANES), 1)
    li = lax.broadcasted_iota(jnp.int32, (q, D_INNER), 0)
    si = lax.broadcasted_iota(jnp.int32, (q, D_INNER), 1) & (q - 1)
    diag = li == si
    causal = (li >= si)[:, :GROUP_CH]
    bdr = lax.broadcasted_iota(jnp.int32, (GROUP_CH, GROUP_CH), 0) // SSM_HEAD_DIM
    bdc = lax.broadcasted_iota(jnp.int32, (GROUP_CH, GROUP_CH), 1) // SSM_HEAD_DIM
    block_diag = bdr == bdc
    neg_a = -jnp.exp(alog_ref[...])

    def sequence(b):
        x = xbc_ref[b]
        hi, mid, lo = _split3(jnp.concatenate([tail_scr[b], x], axis=0))
        delayed = _mm(shift_ref[...], jnp.concatenate([hi, mid, lo], axis=0))
        tail_scr[b, tail_rows, :] = x[q - SUBLANES:q, :]
        pre = dtwi_ref[b] + dtb_ref[...]
        dt = jnp.maximum(pre, 0.0) + jnp.log1p(jnp.exp(-jnp.abs(pre)))
        dt = jnp.where(lane < SSM_HEADS, dt, 0.0)
        cum = _cumsum_rows(dt * neg_a)
        e = e_ref[...]
        ecol = _expand_heads(cum, e)
        dtx = _expand_heads(dt, e)
        yield

        acc = cb_ref[...] + x * cw_ref[CONV_W - 1:CONV_W, :]
        for k in range(CONV_W - 1):
            acc = acc + delayed[k * q:(k + 1) * q, :] * cw_ref[k:k + 1, :]
        xc = _silu(acc)
        xs = xc[:, :D_INNER]
        bm = xc[:, D_INNER:D_INNER + N_GROUPS * D_STATE].astype(BF16)
        cm = xc[:, D_INNER + N_GROUPS * D_STATE:].astype(BF16)
        cbs = [_mm_nt(cm[:, g * D_STATE:(g + 1) * D_STATE], bm[:, g * D_STATE:(g + 1) * D_STATE])
               for g in range(N_GROUPS)]
        yield

        erow = jnp.sum(jnp.where(diag, ecol, 0.0), axis=0, keepdims=True)
        elast = ecol[q - 1:q, :]
        xdt = xs * dtx
        xdec = (xdt * jnp.exp(elast - ecol)).astype(BF16)
        exp_e = jnp.exp(ecol)
        chunk_decay = jnp.exp(elast)
        yield

        ys = []
        for g in range(N_GROUPS):
            sl = slice(g * GROUP_CH, (g + 1) * GROUP_CH)
            nl = slice(g * D_STATE, (g + 1) * D_STATE)
            cbt = jnp.concatenate([cbs[g]] * HEADS_PER_GROUP, axis=1)
            decay = jnp.exp(jnp.where(causal, ecol[:, sl] - erow[:, sl], NEG_BIG))
            m = (cbt * decay).astype(BF16)
            xg = xdt[:, sl]
            bd = jnp.where(block_diag, jnp.concatenate([xg] * HEADS_PER_GROUP, axis=0), 0.0).astype(BF16)
            y_diag = _mm(m, bd)
            ht_g = ht_scr[b, :, sl]
            y_off = _mm(cm[:, nl], ht_g.astype(BF16)) * exp_e[:, sl]
            ys.append(y_diag + y_off)
            ht_scr[b, :, sl] = ht_g * chunk_decay[:, sl] + _mm_tn(bm[:, nl], xdec[:, sl])
            yield

        y = jnp.concatenate(ys, axis=1) + dskip_ref[...] * xs
        y = y * _silu(z_ref[b])
        outs = []
        for g in range(N_GROUPS):
            yg = y[:, g * GROUP_CH:(g + 1) * GROUP_CH]
            outs.append(yg * lax.rsqrt(jnp.mean(yg * yg, axis=-1, keepdims=True) + EPS))
        y_ref[b] = (jnp.concatenate(outs, axis=1) * ng_ref[...]).astype(y_ref.dtype)
        yield

    for _ in zip(*[sequence(b) for b in range(MAMBA_PAIR)]):
        pass

    @pl.when(c == n_chunks - 1)
    def _():
        for b in range(MAMBA_PAIR):
            hout_ref[b] = ht_scr[b].T


def _head_expand_matrix():
    e = np.zeros((LANES, D_INNER), np.float32)
    for h in range(SSM_HEADS):
        e[h, h * SSM_HEAD_DIM:(h + 1) * SSM_HEAD_DIM] = 1.0
    return jnp.asarray(e, BF16)


def _mamba(z, xbc, dtwi, conv0, h0, lw, batch, t):
    nc = t // CHUNK
    assert batch % MAMBA_PAIR == 0
    has_state = h0 is not None
    chunk = lambda b, c: (b, c, 0)
    per_b = lambda b, c: (b, 0, 0)
    pad16 = lambda v: jnp.pad(v.reshape(1, SSM_HEADS), ((0, 0), (0, LANES - SSM_HEADS)))
    small = [lw["conv_w"], lw["conv_b"].reshape(1, CONV_CH), pad16(lw["dt_bias"]), pad16(lw["a_log"]),
             jnp.repeat(lw["d_skip"], SSM_HEAD_DIM).reshape(1, D_INNER),
             lw["ssm_norm_g"].reshape(1, D_INNER), _head_expand_matrix(), _conv_shift_matrix()]
    ins = [a.reshape(batch, t, a.shape[-1]) for a in (z, xbc, dtwi)]
    in_specs = [pl.BlockSpec((MAMBA_PAIR, CHUNK, a.shape[-1]), chunk) for a in ins]
    if has_state:
        ins += [conv0, h0]
        in_specs += [pl.BlockSpec((MAMBA_PAIR, SUBLANES, CONV_CH), per_b),
                     pl.BlockSpec((MAMBA_PAIR, D_INNER, D_STATE), per_b)]
    ins += small
    in_specs += [_resident(a.shape) for a in small]
    y, h_out = pl.pallas_call(
        functools.partial(_mamba_body, has_state=has_state, n_chunks=nc),
        grid=(batch // MAMBA_PAIR, nc),
        in_specs=in_specs,
        out_specs=[pl.BlockSpec((MAMBA_PAIR, CHUNK, D_INNER), chunk),
                   pl.BlockSpec((MAMBA_PAIR, D_INNER, D_STATE), per_b)],
        out_shape=[jax.ShapeDtypeStruct((batch, t, D_INNER), BF16),
                   jax.ShapeDtypeStruct((batch, D_INNER, D_STATE), F32)],
        scratch_shapes=[pltpu.VMEM((MAMBA_PAIR, CONV_PAD, CONV_CH), F32),
                        pltpu.VMEM((MAMBA_PAIR, D_STATE, D_INNER), F32)],
        compiler_params=_params(2),
        name="mamba",
    )(*ins)
    return y.reshape(batch * t, D_INNER), h_out


DSA_TQ = LANES
DSA_KB = 512
IDX_CAT = 4 * IDX_DIM
DSA_RANK_ROWS = 128


def _hi_lo(x):
    hi = x.astype(BF16).astype(F32)
    return hi, (x - hi).astype(BF16).astype(F32)


def _dsa_body(qb_ref, qi_ref, dtwi_ref, ki_ref, k_ref, v_ref, o_ref,
              kcat_scr, kbf_scr, vt_scr, tri_scr, key_scr, mask_scr, s_scr,
              *, n_kb, n_valid, q_pos0, topk, visible):
    tq, kb_rows = DSA_TQ, DSA_KB
    j = pl.program_id(1)

    @pl.when(j == 0)
    def _():
        hi, lo = _hi_lo(ki_ref[...])
        kcat_scr[...] = jnp.concatenate([hi, lo, hi, jnp.zeros_like(hi)], axis=1).astype(BF16)
        kbf_scr[...] = k_ref[...].astype(BF16)
        for kb in range(n_kb):
            vt = v_ref[kb * kb_rows:(kb + 1) * kb_rows, :].T
            for g in range(KV_HEADS):
                vt_scr[g, kb] = vt[g * DSA_HEAD_DIM:(g + 1) * DSA_HEAD_DIM, :].astype(BF16)
        tr = lax.broadcasted_iota(jnp.int32, (DSA_RANK_ROWS, DSA_RANK_ROWS), 0)
        tc = lax.broadcasted_iota(jnp.int32, (DSA_RANK_ROWS, DSA_RANK_ROWS), 1)
        tri_scr[...] = jnp.where(tc < tr, 1.0, 0.0).astype(BF16)

    qpos = q_pos0 + j * tq + lax.broadcasted_iota(jnp.int32, (1, tq), 1)
    q_end = jnp.minimum(((qpos >> 6) + 1) << 6, n_valid)
    k_eff = jnp.minimum(q_end, topk).astype(F32)
    last_end = jnp.minimum((((q_pos0 + (j + 1) * tq - 1) >> 6) + 1) << 6, n_valid)
    nkb = (last_end + (kb_rows - 1)) // kb_rows

    wit = (dtwi_ref[...] * (IDX_DIM ** -0.5 * IDX_HEADS ** -0.5)).T
    qi = qi_ref[...]
    qparts = []
    for h in range(IDX_HEADS):
        hi, lo = _hi_lo(qi[:, h * IDX_DIM:(h + 1) * IDX_DIM])
        qparts.append(jnp.concatenate([hi, hi, lo, jnp.zeros_like(hi)], axis=1))
    qcat = jnp.concatenate(qparts, axis=0).astype(BF16)
    krow = lax.broadcasted_iota(jnp.int32, (kb_rows, 1), 0)
    qb = qb_ref[...]
    q4 = [jnp.concatenate([qb[:, (g * KV_REP + r) * DSA_HEAD_DIM:(g * KV_REP + r + 1) * DSA_HEAD_DIM]
                           for r in range(KV_REP)], axis=0) for g in range(KV_HEADS)]

    def run(n_vis):
        for kb in range(n_vis):
            logit = _mm_nt(kcat_scr[kb * kb_rows:(kb + 1) * kb_rows, :], qcat)
            sc = jnp.zeros((kb_rows, tq), F32)
            for h in range(IDX_HEADS):
                sc = sc + jnp.maximum(logit[:, h * tq:(h + 1) * tq], 0.0) * wit[WI_LANE + h:WI_LANE + h + 1, :]
            sc = jnp.where(kb * kb_rows + krow < q_end, sc, -jnp.inf)
            bits = lax.bitcast_convert_type(sc, jnp.int32)
            key_scr[kb] = jnp.where(bits < 0, bits ^ jnp.int32(0x7FFFFFFF), bits)

        def count(pred):
            acc = _fold_rows(jnp.where(pred(key_scr[0]), 1.0, 0.0), jnp.add)
            for kb in range(1, n_vis):
                acc = acc + _fold_rows(jnp.where(pred(key_scr[kb]), 1.0, 0.0), jnp.add)
            return jnp.sum(acc, axis=0, keepdims=True)

        def radix_step(i, tu):
            cand = tu | lax.shift_left(jnp.int32(1), 31 - i)
            thr_c = cand ^ jnp.int32(INT_MIN)
            return jnp.where(count(lambda k: k >= thr_c) >= k_eff, cand, tu)

        tu = lax.fori_loop(0, 32, radix_step, jnp.zeros((1, tq), jnp.int32))
        thr = tu ^ jnp.int32(INT_MIN)

        need = k_eff - count(lambda k: k > thr)
        before = jnp.zeros((1, tq), F32)
        for kb in range(n_vis):
            for r0 in range(0, kb_rows, DSA_RANK_ROWS):
                key = key_scr[kb, r0:r0 + DSA_RANK_ROWS, :]
                eq = jnp.where(key == thr, 1.0, 0.0)
                rank = _mm(tri_scr[...], eq.astype(BF16)) + before
                take = jnp.where(key > thr, 1.0, jnp.where(rank < need, eq, 0.0))
                mask_scr[kb, r0:r0 + DSA_RANK_ROWS, :] = jnp.where(take > 0.0, 0.0, NEG_BIG)
                before = before + jnp.sum(_fold_rows(eq, jnp.add), axis=0, keepdims=True)

        m8 = [jnp.full((SUBLANES, KV_REP * tq), NEG_BIG, F32) for _ in range(KV_HEADS)]
        for kb in range(n_vis):
            mask4 = jnp.concatenate([mask_scr[kb]] * KV_REP, axis=1)
            for g in range(KV_HEADS):
                kg = kbf_scr[kb * kb_rows:(kb + 1) * kb_rows, g * DSA_HEAD_DIM:(g + 1) * DSA_HEAD_DIM]
                st = _mm_nt(kg, q4[g]) + mask4
                s_scr[kb, g] = st
                m8[g] = jnp.maximum(m8[g], _fold_rows(st, jnp.maximum))
        m = [jnp.max(m8[g], axis=0, keepdims=True) for g in range(KV_HEADS)]
        acc, den = [None] * KV_HEADS, [None] * KV_HEADS
        for kb in range(n_vis):
            for g in range(KV_HEADS):
                p = jnp.exp2(s_scr[kb, g] - m[g])
                pv = _mm(vt_scr[g, kb], p.astype(BF16))
                psum = _fold_rows(p, jnp.add)
                acc[g] = pv if acc[g] is None else acc[g] + pv
                den[g] = psum if den[g] is None else den[g] + psum
        outs = []
        for g in range(KV_HEADS):
            ot = acc[g] / jnp.sum(den[g], axis=0, keepdims=True)
            outs += [ot[:, r * tq:(r + 1) * tq].T for r in range(KV_REP)]
        o_ref[...] = jnp.concatenate(outs, axis=1).astype(o_ref.dtype)

    if len(visible) == 1:
        run(visible[0])
    else:
        for n_vis in visible:
            pl.when(nkb == n_vis)(functools.partial(run, n_vis))


def _dsa(qb, qi, dtwi, ki_all, k_all, v_all, key_layer, batch, t, s_keys, n_valid, q_pos0):
    tq, kb_rows = DSA_TQ, DSA_KB
    assert t % tq == 0 and s_keys % kb_rows == 0
    nq = t // tq
    n_kb = s_keys // kb_rows
    topk = min(TOPK_MAX, n_valid // 4)
    qrow = lambda b, j: (b * nq + j, 0)
    krow = lambda b, j: (key_layer, b, 0)

    def visible_blocks(j):
        last_end = min(((((q_pos0 + (j + 1) * tq - 1) >> 6) + 1) << 6), n_valid)
        return (last_end + kb_rows - 1) // kb_rows

    visible = tuple(sorted({visible_blocks(j) for j in range(nq)}))
    return pl.pallas_call(
        functools.partial(_dsa_body, n_kb=n_kb, n_valid=n_valid, q_pos0=q_pos0, topk=topk, visible=visible),
        grid=(batch, nq),
        in_specs=[pl.BlockSpec((tq, D_MODEL), qrow), pl.BlockSpec((tq, IDX_HEADS * IDX_DIM), qrow),
                  pl.BlockSpec((tq, LANES), qrow), pl.BlockSpec((None, s_keys, IDX_DIM), krow),
                  pl.BlockSpec((None, s_keys, KV_DIM), krow), pl.BlockSpec((None, s_keys, KV_DIM), krow)],
        out_specs=pl.BlockSpec((tq, D_MODEL), qrow),
        out_shape=jax.ShapeDtypeStruct((batch * t, D_MODEL), BF16),
        scratch_shapes=[pltpu.VMEM((s_keys, IDX_CAT), BF16), pltpu.VMEM((s_keys, KV_DIM), BF16),
                        pltpu.VMEM((KV_HEADS, n_kb, DSA_HEAD_DIM, kb_rows), BF16),
                        pltpu.VMEM((DSA_RANK_ROWS, DSA_RANK_ROWS), BF16),
                        pltpu.VMEM((n_kb, kb_rows, tq), jnp.int32), pltpu.VMEM((n_kb, kb_rows, tq), F32),
                        pltpu.VMEM((n_kb, KV_HEADS, kb_rows, KV_REP * tq), F32)],
        compiler_params=_params(2),
        name="dsa",
    )(qb, qi, dtwi, ki_all, k_all, v_all)


def _band_body(*refs, tq, sub, n_kblk, q_pos0, k_min, clamped):
    q_ref = refs[0]
    k_refs = refs[1:1 + n_kblk]
    v_refs = refs[1 + n_kblk:1 + 2 * n_kblk]
    vec_ref, o_ref, bias_scr = refs[1 + 2 * n_kblk:]
    w = sum(r.shape[0] for r in k_refs)
    wsub = w - tq + sub
    i = pl.program_id(1)

    @pl.when((pl.program_id(0) == 0) & (i == 0))
    def _():
        r = lax.broadcasted_iota(jnp.int32, (sub, wsub), 0)
        c = lax.broadcasted_iota(jnp.int32, (sub, wsub), 1)
        dchunk = (r >> 6) + (wsub - sub) // CHUNK - (c >> 6)
        band_mask = jnp.where((dchunk >= 0) & (dchunk <= LEFT_CHUNKS), 0.0, NEG_BIG)
        for h in range(BAND_HEADS):
            rows = jnp.broadcast_to(vec_ref[h:h + 1, :], (sub, vec_ref.shape[1]))
            toeplitz = pltpu.roll(rows, 0, 1, stride=1, stride_axis=0)[:, :wsub]
            bias_scr[h // 2, (h % 2) * sub:(h % 2 + 1) * sub, :] = toeplitz * LOG2E + band_mask

    lane = lax.broadcasted_iota(jnp.int32, (sub, LANES), 1)
    first_head = lane < BAND_HEAD_DIM
    keep_a = jnp.where(first_head, 1.0, 0.0).astype(BF16)
    keep_b = jnp.where(first_head, 0.0, 1.0).astype(BF16)
    ones = jnp.ones((w, LANES), BF16)
    n_sub = tq // sub

    def all_rows(blocks, lanes):
        parts = [blk[:, lanes] for blk in blocks]
        return parts[0] if len(parts) == 1 else jnp.concatenate(parts, axis=0)

    def scores(hp):
        lanes = slice(hp * LANES, (hp + 1) * LANES)
        q2 = []
        for c2 in range(n_sub):
            qp = q_ref[c2 * sub:(c2 + 1) * sub, lanes]
            q2 += [qp * keep_a, qp * keep_b]
        return _mm_nt(jnp.concatenate(q2, axis=0), all_rows(k_refs, lanes))

    def attend(hp, s_full, mask_missing_keys):
        lanes = slice(hp * LANES, (hp + 1) * LANES)
        ps = []
        for c2 in range(n_sub):
            s = s_full[c2 * 2 * sub:(c2 + 1) * 2 * sub, c2 * sub:c2 * sub + wsub] + bias_scr[hp]
            if mask_missing_keys:
                kpos = q_pos0 + i * tq + c2 * sub + (sub - wsub) + lax.broadcasted_iota(jnp.int32, (1, wsub), 1)
                s = s + jnp.where(kpos >= k_min, 0.0, NEG_BIG)
            p = jnp.exp2(s - jnp.max(s, axis=1, keepdims=True)).astype(BF16)
            pad = [jnp.zeros((2 * sub, c2 * sub), BF16)] if c2 else []
            pad_r = [jnp.zeros((2 * sub, w - wsub - c2 * sub), BF16)] if w - wsub - c2 * sub else []
            ps.append(jnp.concatenate(pad + [p] + pad_r, axis=1) if pad or pad_r else p)
        p_full = ps[0] if n_sub == 1 else jnp.concatenate(ps, axis=0)
        o = _mm(p_full, jnp.concatenate([all_rows(v_refs, lanes), ones], axis=1))
        for c2 in range(n_sub):
            ra, rb = c2 * 2 * sub, c2 * 2 * sub + sub
            oa = o[ra:ra + sub, :LANES] / o[ra:ra + sub, LANES:LANES + 1]
            ob = o[rb:rb + sub, :LANES] / o[rb:rb + sub, LANES:LANES + 1]
            o_ref[c2 * sub:(c2 + 1) * sub, lanes] = jnp.where(first_head, oa, ob).astype(o_ref.dtype)

    def heads(mask_missing_keys):
        n_pairs, ahead, pending = BAND_HEADS // 2, 2, {}
        for n in range(n_pairs + ahead):
            if n < n_pairs:
                pending[n] = scores(n)
            if n >= ahead:
                attend(n - ahead, pending.pop(n - ahead), mask_missing_keys)

    if clamped:
        first_full = -(-(w - tq) // tq)
        pl.when(i < first_full)(functools.partial(heads, True))
        pl.when(i >= first_full)(functools.partial(heads, False))
    else:
        heads(False)


def _band_bias_vec(rel_bias, sub, wsub):
    l = -(-(wsub + sub) // LANES) * LANES
    m = np.arange(l)
    d = np.where(m < wsub, m, m - l)
    rel = np.clip(wsub - sub - d, -REL_CLIP, REL_CLIP) + REL_CLIP
    return rel_bias[jnp.asarray(rel)].T


def _band(q, k, v, rel_bias, batch, t, tq, sub, k_block_rows, n_kblk, q_pos0, k_min, clamped):
    nq = t // tq
    w = n_kblk * k_block_rows
    wsub = w - tq + sub
    assert wsub % LANES == 0 and (wsub - sub) % CHUNK == 0
    qrow = lambda b, i: (b * nq + i, 0)

    def krow(off):
        if clamped:
            return lambda b, i: (b * nq + jnp.maximum(i - (n_kblk - 1) + off, 0), 0)
        return lambda b, i: (b * n_kblk + off, 0)

    vec = _band_bias_vec(rel_bias, sub, wsub)
    kspecs = [pl.BlockSpec((k_block_rows, D_MODEL), krow(o)) for o in range(n_kblk)]
    return pl.pallas_call(
        functools.partial(_band_body, tq=tq, sub=sub, n_kblk=n_kblk, q_pos0=q_pos0, k_min=k_min, clamped=clamped),
        grid=(batch, nq),
        in_specs=[pl.BlockSpec((tq, D_MODEL), qrow)] + kspecs + kspecs + [_resident(vec.shape)],
        out_specs=pl.BlockSpec((tq, D_MODEL), qrow),
        out_shape=jax.ShapeDtypeStruct((batch * t, D_MODEL), BF16),
        scratch_shapes=[pltpu.VMEM((BAND_HEADS // 2, 2 * sub, wsub), F32)],
        compiler_params=_params(2),
        name="band",
    )(q, *([k] * n_kblk), *([v] * n_kblk), vec)


def _merge_body(x_ref, ya_ref, yb_ref, yc_ref, g_ref, wg_ref, bg_ref, wbr_ref, wo_ref, o_ref):
    x = x_ref[...]
    u = _rms(x, g_ref[2:3, :]).astype(BF16)
    mix = jnp.zeros(x.shape, F32)
    for k, y_ref in enumerate((ya_ref, yb_ref, yc_ref)):
        sl = slice(k * D_MODEL, (k + 1) * D_MODEL)
        gate = _sigmoid(_mm(u, wg_ref[:, sl]) + bg_ref[:, sl])
        mix = mix + gate * _mm(y_ref[...], wbr_ref[k])
    o_ref[...] = x + _rms(_mm(mix.astype(BF16), wo_ref[...]), g_ref[3:4, :])


def _merge(x, ya, yb, yc, pw, layer):
    n = x.shape[0]
    tm = _row_tile(n, 256)
    row = lambda i: (i, 0)
    tile = pl.BlockSpec((tm, D_MODEL), row)
    return pl.pallas_call(
        _merge_body,
        grid=(n // tm,),
        in_specs=[tile, tile, tile, tile]
        + [_resident_slice(pw[k], (layer,)) for k in ("g", "wg", "bg", "wbr", "wo")],
        out_specs=tile,
        out_shape=jax.ShapeDtypeStruct((n, D_MODEL), F32),
        compiler_params=_params(1),
        name="merge",
    )(x, ya, yb, yc, pw["g"], pw["wg"], pw["bg"], pw["wbr"], pw["wo"])


def _ple_body(x_ref, p_ref, g_ref, wp_ref, wpg_ref, o_ref):
    x = x_ref[...]
    e = _mm(p_ref[...].astype(BF16), wp_ref[...])
    pg = _sigmoid(_mm(_rms(x, g_ref[6:7, :]).astype(BF16), wpg_ref[...]))
    o_ref[...] = x + _rms(pg * e, g_ref[7:8, :])


def _ple(x, p, pw, layer):
    n = x.shape[0]
    tm = _row_tile(n, 512)
    row = lambda i: (i, 0)
    return pl.pallas_call(
        _ple_body,
        grid=(n // tm,),
        in_specs=[pl.BlockSpec((tm, D_MODEL), row), pl.BlockSpec((tm, PLE_DIM), row)]
        + [_resident_slice(pw[k], (layer,)) for k in ("g", "wp", "wpg")],
        out_specs=pl.BlockSpec((tm, D_MODEL), row),
        out_shape=jax.ShapeDtypeStruct((n, D_MODEL), F32),
        compiler_params=_params(1),
        name="ple",
    )(x, p, pw["g"], pw["wp"], pw["wpg"])


BAND_TQ = 256
BAND_SUB = 128
BAND_KBLK = 1 + -(-BAND // BAND_TQ)


def _prep_weights(norm_g, ffn_w13, ffn_w2, w_in, w_gate, b_gate, w_branch, w_out, w_ple, w_ple_gate):
    bf = lambda a: a.astype(BF16)
    return {"g": norm_g, "w13": bf(ffn_w13), "w2": bf(ffn_w2), "w_in": _pack_w_in(w_in), "wg": bf(w_gate),
            "bg": b_gate.reshape(DEPTH, 1, N_BRANCH * D_MODEL), "wbr": bf(w_branch), "wo": bf(w_out),
            "wp": bf(w_ple), "wpg": bf(w_ple_gate)}


def _trunk_layer(x, p, w, pw, layer, cache, batch, t, shared):
    x = _ffn(x, pw, layer, 0, 0, 1)
    pr, shared = _inproj(x, pw, layer, batch, t, shared)
    if cache is None:
        ya, h_new = _mamba(pr["z"], pr["xbc"], pr["dtwi"], None, None, w, batch, t)
        yb = _dsa(pr["qb"], pr["qi"], pr["dtwi"], pr["ki"], pr["kb"], pr["vb"], layer, batch, t, t, t, 0)
        yc = _band(pr["qc"], pr["kc_bf"], pr["vc_bf"], w["rel_bias"], batch, t, BAND_TQ, BAND_SUB, BAND_TQ,
                   BAND_KBLK, 0, 0, True)
        conv_src = pr["xbc"].reshape(batch, t, CONV_CH)
    else:
        past = cache["dsa_k"].shape[1]
        conv0 = jnp.pad(cache["conv"], ((0, 0), (SUBLANES - (CONV_W - 1), 0), (0, 0)))
        h0 = cache["ssm"].reshape(batch, D_INNER, D_STATE)
        ya, h_new = _mamba(pr["z"], pr["xbc"], pr["dtwi"], conv0, h0, w, batch, t)

        n_valid = past + t
        s_keys = -(-n_valid // DSA_KB) * DSA_KB
        tq_pad = -(-t // DSA_TQ) * DSA_TQ

        def with_cache(c, new):
            width = new.shape[-1]
            a = jnp.concatenate([c.reshape(batch, past, width), new[layer].reshape(batch, t, width)], axis=1)
            return jnp.pad(a, ((0, 0), (0, s_keys - n_valid), (0, 0))).reshape(1, batch * s_keys, width)

        def pad_q(a):
            a = jnp.pad(a.reshape(batch, t, a.shape[-1]), ((0, 0), (0, tq_pad - t), (0, 0)))
            return a.reshape(batch * tq_pad, a.shape[-1])

        yb = _dsa(pad_q(pr["qb"]), pad_q(pr["qi"]), pad_q(pr["dtwi"]), with_cache(cache["idx_k"], pr["ki"]),
                  with_cache(cache["dsa_k"], pr["kb"]), with_cache(cache["dsa_v"], pr["vb"]),
                  0, batch, tq_pad, s_keys, n_valid, past)
        yb = yb.reshape(batch, tq_pad, D_MODEL)[:, :t].reshape(batch * t, D_MODEL)

        nrows = cache["band_k"].shape[1]
        k_rows = -(-(nrows + t) // LANES) * LANES
        lead = k_rows - nrows - t

        def with_band(c, new):
            a = jnp.concatenate([c.reshape(batch, nrows, D_MODEL).astype(BF16), new.reshape(batch, t, D_MODEL)],
                                axis=1)
            return jnp.pad(a, ((0, 0), (lead, 0), (0, 0))).reshape(batch * k_rows, D_MODEL)

        yc = _band(pr["qc"], with_band(cache["band_k"], pr["kc_bf"]), with_band(cache["band_v"], pr["vc_bf"]),
                   w["rel_bias"], batch, t, t, t, k_rows, 1, past, past - nrows, False)
        conv_src = jnp.concatenate([cache["conv"], pr["xbc"].reshape(batch, t, CONV_CH)], axis=1)

    x = _merge(x, ya, yb, yc, pw, layer)
    x = _ffn(x, pw, layer, 1, 4, 5)
    x = _ple(x, p.reshape(batch * t, PLE_DIM), pw, layer)

    state = (h_new.reshape(batch, SSM_HEADS, SSM_HEAD_DIM, D_STATE), conv_src[:, -(CONV_W - 1):])
    return x, state, shared


def _cache_outputs(shared, states, batch, t):
    band_rows = min(BAND, t)
    return (shared["kb"].reshape(DEPTH, batch, t, KV_HEADS, DSA_HEAD_DIM),
            shared["vb"].reshape(DEPTH, batch, t, KV_HEADS, DSA_HEAD_DIM),
            shared["ki"].reshape(DEPTH, batch, t, IDX_DIM),
            shared["kc"].reshape(DEPTH, batch, band_rows, BAND_HEADS, BAND_HEAD_DIM),
            shared["vc"].reshape(DEPTH, batch, band_rows, BAND_HEADS, BAND_HEAD_DIM),
            jnp.stack([s[0] for s in states]), jnp.stack([s[1] for s in states]))


def kernel(x_prompt, x_sample, p_prompt, p_sample, cache_dsa_k, cache_dsa_v, cache_idx_k, cache_band_k,
           cache_band_v, state_ssm, state_conv, norm_g, ffn_w13, ffn_w2, w_in, conv_w, conv_b, dt_bias,
           a_log, d_skip, ssm_norm_g, rel_bias, w_gate, b_gate, w_branch, w_out, w_ple, w_ple_gate):
    bp, tp, _ = x_prompt.shape
    bs, ts, _ = x_sample.shape
    yp = x_prompt.reshape(bp * tp, D_MODEL)
    ys = x_sample.reshape(bs * ts, D_MODEL)
    st_p, st_s, shared_p, shared_s = [], [], None, None
    pw = _prep_weights(norm_g, ffn_w13, ffn_w2, w_in, w_gate, b_gate, w_branch, w_out, w_ple, w_ple_gate)
    for i in range(DEPTH):
        w = {"conv_w": conv_w[i], "conv_b": conv_b[i], "dt_bias": dt_bias[i], "a_log": a_log[i],
             "d_skip": d_skip[i], "ssm_norm_g": ssm_norm_g[i], "rel_bias": rel_bias[i]}
        yp, sp, shared_p = _trunk_layer(yp, p_prompt[i], w, pw, i, None, bp, tp, shared_p)
        st_p.append(sp)
        cache = {"dsa_k": cache_dsa_k[i], "dsa_v": cache_dsa_v[i], "idx_k": cache_idx_k[i],
                 "band_k": cache_band_k[i], "band_v": cache_band_v[i], "ssm": state_ssm[i], "conv": state_conv[i]}
        ys, ss, shared_s = _trunk_layer(ys, p_sample[i], w, pw, i, cache, bs, ts, shared_s)
        st_s.append(ss)
    return (yp.reshape(bp, tp, D_MODEL), ys.reshape(bs, ts, D_MODEL),
            *_cache_outputs(shared_p, st_p, bp, tp), *_cache_outputs(shared_s, st_s, bs, ts))
```

```python
import functools
import math

import numpy as np
import jax
import jax.numpy as jnp
from jax import lax
from jax.experimental import pallas as pl
from jax.experimental.pallas import tpu as pltpu

F32 = jnp.float32
BF16 = jnp.bfloat16

D_MODEL = 1024
DEPTH = 2
CHUNK = 64
EPS = 1e-6
HALF = 0.5
D_FF = 2816
PLE_DIM = 256
SSM_HEAD_DIM = 64
D_INNER = D_MODEL
SSM_HEADS = D_INNER // SSM_HEAD_DIM
N_GROUPS = 4
HEADS_PER_GROUP = SSM_HEADS // N_GROUPS
D_STATE = 128
CONV_W = 4
CONV_CH = D_INNER + 2 * N_GROUPS * D_STATE
DSA_HEAD_DIM = 128
DSA_HEADS = D_MODEL // DSA_HEAD_DIM
KV_HEADS = 2
KV_REP = DSA_HEADS // KV_HEADS
IDX_HEADS = 4
IDX_DIM = 64
TOPK_MAX = 256
BAND_HEAD_DIM = 64
BAND_HEADS = D_MODEL // BAND_HEAD_DIM
LEFT_CHUNKS = 8
BAND = LEFT_CHUNKS * CHUNK
REL_CLIP = 256
N_BRANCH = 3
IN_WIDTHS = (D_INNER, CONV_CH, SSM_HEADS,
             DSA_HEADS * DSA_HEAD_DIM, KV_HEADS * DSA_HEAD_DIM, KV_HEADS * DSA_HEAD_DIM,
             IDX_HEADS * IDX_DIM, IDX_DIM, IDX_HEADS,
             BAND_HEADS * BAND_HEAD_DIM, BAND_HEADS * BAND_HEAD_DIM, BAND_HEADS * BAND_HEAD_DIM)
IN_SPLITS = tuple(int(s) for s in np.cumsum(IN_WIDTHS)[:-1])

LANES = 128
SUBLANES = 8
KV_DIM = KV_HEADS * DSA_HEAD_DIM
GROUP_CH = D_INNER // N_GROUPS
NEG_BIG = -1e30
INT_MIN = -2 ** 31
LOG2E = math.log2(math.e)
VMEM_LIMIT = 56 * 1024 * 1024


def _mm(a, b):
    return jnp.dot(a, b, preferred_element_type=F32)


def _mm_nt(a, b):
    return lax.dot_general(a, b, (((1,), (1,)), ((), ())), preferred_element_type=F32)


def _mm_tn(a, b):
    return lax.dot_general(a, b, (((0,), (0,)), ((), ())), preferred_element_type=F32)


def _rms(x, g):
    return x * lax.rsqrt(jnp.mean(x * x, axis=-1, keepdims=True) + EPS) * g


def _sigmoid(x):
    return 1.0 / (1.0 + jnp.exp(-x))


def _silu(x):
    return x * _sigmoid(x)


def _resident(shape):
    return pl.BlockSpec(shape, lambda *_: (0,) * len(shape), pipeline_mode=pl.Buffered(1))


def _resident_slice(arr, lead, block=None, at=None):
    tail = tuple(arr.shape[len(lead):]) if block is None else tuple(block)
    idx = tuple(lead) + ((0,) * len(tail) if at is None else tuple(at))
    return pl.BlockSpec((None,) * len(lead) + tail, lambda *_: idx, pipeline_mode=pl.Buffered(1))


def _params(n_grid_dims):
    return pltpu.CompilerParams(dimension_semantics=("arbitrary",) * n_grid_dims,
                                vmem_limit_bytes=VMEM_LIMIT)


def _row_tile(n_rows, want):
    t = min(want, n_rows)
    assert n_rows % t == 0
    return t


def _fold_rows(x, op):
    parts = [x[i * SUBLANES:(i + 1) * SUBLANES] for i in range(x.shape[0] // SUBLANES)]
    while len(parts) > 1:
        parts = [op(parts[i], parts[i + 1]) for i in range(0, len(parts) - 1, 2)] + parts[len(parts) & ~1:]
    return parts[0]


FF_CHUNK = 256


def _ffn_body(x_ref, g_ref, wa_ref, wb_ref, w2_ref, o_ref, *, g_pre, g_post):
    x = x_ref[...]
    u = _rms(x, g_ref[g_pre:g_pre + 1, :]).astype(BF16)
    acc = jnp.zeros(x.shape, F32)
    for c in range(D_FF // FF_CHUNK):
        sl = slice(c * FF_CHUNK, (c + 1) * FF_CHUNK)
        a = _mm(u, wa_ref[:, sl])
        b = _mm(u, wb_ref[:, sl])
        acc = acc + _mm((_silu(a) * b).astype(BF16), w2_ref[sl, :])
    o_ref[...] = x + HALF * _rms(acc, g_ref[g_post:g_post + 1, :])


def _ffn(x, pw, layer, j, g_pre, g_post):
    n = x.shape[0]
    tm = _row_tile(n, 1024)
    row = lambda i: (i, 0)
    half = (D_MODEL, D_FF)
    return pl.pallas_call(
        functools.partial(_ffn_body, g_pre=g_pre, g_post=g_post),
        grid=(n // tm,),
        in_specs=[pl.BlockSpec((tm, D_MODEL), row), _resident_slice(pw["g"], (layer,)),
                  _resident_slice(pw["w13"], (layer, j), half, (0, 0)),
                  _resident_slice(pw["w13"], (layer, j), half, (0, 1)),
                  _resident_slice(pw["w2"], (layer, j))],
        out_specs=pl.BlockSpec((tm, D_MODEL), row),
        out_shape=jax.ShapeDtypeStruct((n, D_MODEL), F32),
        compiler_params=_params(1),
        name="ffn",
    )(x, pw["g"], pw["w13"], pw["w13"], pw["w2"])


_INPROJ_GROUPS = (D_INNER, CONV_CH, D_MODEL, KV_DIM, KV_DIM, IDX_HEADS * IDX_DIM, D_MODEL, D_MODEL, D_MODEL,
                  LANES, LANES)
_INPROJ_OUT = (
    ("z", 0, D_INNER, F32, None, "layer"),
    ("xbc", 1, CONV_CH, F32, None, "layer"),
    ("qb", 2, D_MODEL, BF16, DSA_HEAD_DIM ** -0.5 * LOG2E, "layer"),
    ("kb", 3, KV_DIM, F32, None, "stack"),
    ("vb", 4, KV_DIM, F32, None, "stack"),
    ("qi", 5, IDX_HEADS * IDX_DIM, F32, None, "layer"),
    ("qc", 6, D_MODEL, BF16, BAND_HEAD_DIM ** -0.5 * LOG2E, "layer"),
    ("kc", 7, D_MODEL, F32, None, "tail"),
    ("kc_bf", 7, D_MODEL, BF16, None, "layer"),
    ("vc", 8, D_MODEL, F32, None, "tail"),
    ("vc_bf", 8, D_MODEL, BF16, None, "layer"),
    ("ki", 9, IDX_DIM, F32, None, "stack"),
    ("dtwi", 10, LANES, F32, None, "layer"),
)
WI_LANE = SSM_HEADS


def _pack_w_in(w_in):
    z, xbc, dt, qb, kb, vb, qi, ki, wi, qc, kc, vc = jnp.split(w_in, IN_SPLITS, axis=-1)
    pad = lambda w: jnp.pad(w, ((0, 0), (0, 0), (0, LANES - w.shape[-1])))
    cols = [z, xbc, qb, kb, vb, qi, qc, kc, vc, pad(ki), pad(jnp.concatenate([dt, wi], axis=-1))]
    return jnp.concatenate(cols, axis=-1).astype(BF16)


def _inproj_body(x_ref, g_ref, w_ref, *refs):
    out_refs = refs[len(refs) - len(_INPROJ_OUT):]
    u = _rms(x_ref[...], g_ref[2:3, :]).astype(BF16)
    starts = np.concatenate([[0], np.cumsum(_INPROJ_GROUPS)])
    for grp, width in enumerate(_INPROJ_GROUPS):
        r = _mm(u, w_ref[:, int(starts[grp]):int(starts[grp]) + width])
        for (_, og, stored, dtype, scale, _), o_ref in zip(_INPROJ_OUT, out_refs):
            if og == grp:
                v = r if scale is None else r * scale
                o_ref[...] = (v if stored == width else v[:, :stored]).astype(dtype)


def _inproj(x, pw, layer, batch, t, shared):
    n = x.shape[0]
    tm = _row_tile(n, 256)
    tail = min(BAND, t)
    row = lambda i: (i, 0)
    stack_row = lambda i: (layer, i, 0)
    if tail == t:
        tail_row = stack_row
    else:
        assert t % tm == 0 and tail % tm == 0
        per_seq, per_tail = t // tm, tail // tm
        tail_row = lambda i: (layer, (i // per_seq) * per_tail + jnp.maximum(i % per_seq - (per_seq - per_tail), 0), 0)
    out_specs, out_shape, stacked = [], [], []
    for k, (name, _, width, dtype, _, kind) in enumerate(_INPROJ_OUT):
        if kind == "layer":
            out_specs.append(pl.BlockSpec((tm, width), row))
            out_shape.append(jax.ShapeDtypeStruct((n, width), dtype))
        else:
            rows = n if kind == "stack" else batch * tail
            out_specs.append(pl.BlockSpec((None, tm, width), stack_row if kind == "stack" else tail_row))
            out_shape.append(jax.ShapeDtypeStruct((DEPTH, rows, width), dtype))
            stacked.append((name, k))
    carried = [] if shared is None else [shared[name] for name, _ in stacked]
    aliases = {} if shared is None else {3 + a: k for a, (_, k) in enumerate(stacked)}
    outs = pl.pallas_call(
        _inproj_body,
        grid=(n // tm,),
        in_specs=[pl.BlockSpec((tm, D_MODEL), row), _resident_slice(pw["g"], (layer,)),
                  _resident_slice(pw["w_in"], (layer,))] + [pl.BlockSpec(memory_space=pl.ANY)] * len(carried),
        out_specs=out_specs,
        out_shape=out_shape,
        input_output_aliases=aliases,
        compiler_params=_params(1),
        name="inproj",
    )(x, pw["g"], pw["w_in"], *carried)
    pr = {o[0]: a for o, a in zip(_INPROJ_OUT, outs)}
    return pr, {name: pr[name] for name, _ in stacked}


CONV_PAD = 16
CONV_ROWS = CONV_PAD + CHUNK
MAMBA_PAIR = 4


def _split3(x):
    hi = x.astype(BF16)
    r = x - hi.astype(F32)
    mid = r.astype(BF16)
    lo = (r - mid.astype(F32)).astype(BF16)
    return hi, mid, lo


def _expand_heads(x, e):
    hi, mid, lo = _split3(x)
    return _mm(hi, e) + _mm(mid, e) + _mm(lo, e)


def _cumsum_rows(x):
    n = x.shape[0]
    row = lax.broadcasted_iota(jnp.int32, x.shape, 0)
    d = 1
    while d < n:
        x = x + jnp.where(row >= d, pltpu.roll(x, d, 0), 0.0)
        d *= 2
    return x


def _conv_shift_matrix():
    s = np.zeros(((CONV_W - 1) * CHUNK, 3 * CONV_ROWS), np.float32)
    for k in range(CONV_W - 1):
        for l in range(CHUNK):
            for part in range(3):
                s[k * CHUNK + l, part * CONV_ROWS + CONV_PAD - (CONV_W - 1) + k + l] = 1.0
    return jnp.asarray(s, BF16)


def _mamba_body(*refs, has_state, n_chunks):
    if has_state:
        (z_ref, xbc_ref, dtwi_ref, conv0_ref, h0_ref, cw_ref, cb_ref, dtb_ref, alog_ref, dskip_ref,
         ng_ref, e_ref, shift_ref, y_ref, hout_ref, tail_scr, ht_scr) = refs
    else:
        (z_ref, xbc_ref, dtwi_ref, cw_ref, cb_ref, dtb_ref, alog_ref, dskip_ref,
         ng_ref, e_ref, shift_ref, y_ref, hout_ref, tail_scr, ht_scr) = refs
    c = pl.program_id(1)
    q = CHUNK
    tail_rows = slice(CONV_PAD - SUBLANES, CONV_PAD)

    @pl.when(c == 0)
    def _():
        tail_scr[...] = jnp.zeros(tail_scr.shape, F32)
        if has_state:
            for b in range(MAMBA_PAIR):
                tail_scr[b, tail_rows, :] = conv0_ref[b]
                ht_scr[b] = h0_ref[b].T
        else:
            ht_scr[...] = jnp.zeros(ht_scr.shape, F32)

    lane = lax.broadcasted_iota(jnp.int32, (q, LANES), 1)
    li = lax.broadcasted_iota(jnp.int32, (q, D_INNER), 0)
    si = lax.broadcasted_iota(jnp.int32, (q, D_INNER), 1) & (q - 1)
    diag = li == si
    causal = (li >= si)[:, :GROUP_CH]
    bdr = lax.broadcasted_iota(jnp.int32, (GROUP_CH, GROUP_CH), 0) // SSM_HEAD_DIM
    bdc = lax.broadcasted_iota(jnp.int32, (GROUP_CH, GROUP_CH), 1) // SSM_HEAD_DIM
    block_diag = bdr == bdc
    neg_a = -jnp.exp(alog_ref[...])

    def sequence(b):
        x = xbc_ref[b]
        hi, mid, lo = _split3(jnp.concatenate([tail_scr[b], x], axis=0))
        delayed = _mm(shift_ref[...], jnp.concatenate([hi, mid, lo], axis=0))
        tail_scr[b, tail_rows, :] = x[q - SUBLANES:q, :]
        pre = dtwi_ref[b] + dtb_ref[...]
        dt = jnp.maximum(pre, 0.0) + jnp.log1p(jnp.exp(-jnp.abs(pre)))
        dt = jnp.where(lane < SSM_HEADS, dt, 0.0)
        cum = _cumsum_rows(dt * neg_a)
        e = e_ref[...]
        ecol = _expand_heads(cum, e)
        dtx = _expand_heads(dt, e)
        yield

        acc = cb_ref[...] + x * cw_ref[CONV_W - 1:CONV_W, :]
        for k in range(CONV_W - 1):
            acc = acc + delayed[k * q:(k + 1) * q, :] * cw_ref[k:k + 1, :]
        xc = _silu(acc)
        xs = xc[:, :D_INNER]
        bm = xc[:, D_INNER:D_INNER + N_GROUPS * D_STATE].astype(BF16)
        cm = xc[:, D_INNER + N_GROUPS * D_STATE:].astype(BF16)
        cbs = [_mm_nt(cm[:, g * D_STATE:(g + 1) * D_STATE], bm[:, g * D_STATE:(g + 1) * D_STATE])
               for g in range(N_GROUPS)]
        yield

        erow = jnp.sum(jnp.where(diag, ecol, 0.0), axis=0, keepdims=True)
        elast = ecol[q - 1:q, :]
        xdt = xs * dtx
        xdec = (xdt * jnp.exp(elast - ecol)).astype(BF16)
        exp_e = jnp.exp(ecol)
        chunk_decay = jnp.exp(elast)
        yield

        ys = []
        for g in range(N_GROUPS):
            sl = slice(g * GROUP_CH, (g + 1) * GROUP_CH)
            nl = slice(g * D_STATE, (g + 1) * D_STATE)
            cbt = jnp.concatenate([cbs[g]] * HEADS_PER_GROUP, axis=1)
            decay = jnp.exp(jnp.where(causal, ecol[:, sl] - erow[:, sl], NEG_BIG))
            m = (cbt * decay).astype(BF16)
            xg = xdt[:, sl]
            bd = jnp.where(block_diag, jnp.concatenate([xg] * HEADS_PER_GROUP, axis=0), 0.0).astype(BF16)
            y_diag = _mm(m, bd)
            ht_g = ht_scr[b, :, sl]
            y_off = _mm(cm[:, nl], ht_g.astype(BF16)) * exp_e[:, sl]
            ys.append(y_diag + y_off)
            ht_scr[b, :, sl] = ht_g * chunk_decay[:, sl] + _mm_tn(bm[:, nl], xdec[:, sl])
            yield

        y = jnp.concatenate(ys, axis=1) + dskip_ref[...] * xs
        y = y * _silu(z_ref[b])
        outs = []
        for g in range(N_GROUPS):
            yg = y[:, g * GROUP_CH:(g + 1) * GROUP_CH]
            outs.append(yg * lax.rsqrt(jnp.mean(yg * yg, axis=-1, keepdims=True) + EPS))
        y_ref[b] = (jnp.concatenate(outs, axis=1) * ng_ref[...]).astype(y_ref.dtype)
        yield

    for _ in zip(*[sequence(b) for b in range(MAMBA_PAIR)]):
        pass

    @pl.when(c == n_chunks - 1)
    def _():
        for b in range(MAMBA_PAIR):
            hout_ref[b] = ht_scr[b].T


def _head_expand_matrix():
    e = np.zeros((LANES, D_INNER), np.float32)
    for h in range(SSM_HEADS):
        e[h, h * SSM_HEAD_DIM:(h + 1) * SSM_HEAD_DIM] = 1.0
    return jnp.asarray(e, BF16)


def _mamba(z, xbc, dtwi, conv0, h0, lw, batch, t):
    nc = t // CHUNK
    assert batch % MAMBA_PAIR == 0
    has_state = h0 is not None
    chunk = lambda b, c: (b, c, 0)
    per_b = lambda b, c: (b, 0, 0)
    pad16 = lambda v: jnp.pad(v.reshape(1, SSM_HEADS), ((0, 0), (0, LANES - SSM_HEADS)))
    small = [lw["conv_w"], lw["conv_b"].reshape(1, CONV_CH), pad16(lw["dt_bias"]), pad16(lw["a_log"]),
             jnp.repeat(lw["d_skip"], SSM_HEAD_DIM).reshape(1, D_INNER),
             lw["ssm_norm_g"].reshape(1, D_INNER), _head_expand_matrix(), _conv_shift_matrix()]
    ins = [a.reshape(batch, t, a.shape[-1]) for a in (z, xbc, dtwi)]
    in_specs = [pl.BlockSpec((MAMBA_PAIR, CHUNK, a.shape[-1]), chunk) for a in ins]
    if has_state:
        ins += [conv0, h0]
        in_specs += [pl.BlockSpec((MAMBA_PAIR, SUBLANES, CONV_CH), per_b),
                     pl.BlockSpec((MAMBA_PAIR, D_INNER, D_STATE), per_b)]
    ins += small
    in_specs += [_resident(a.shape) for a in small]
    y, h_out = pl.pallas_call(
        functools.partial(_mamba_body, has_state=has_state, n_chunks=nc),
        grid=(batch // MAMBA_PAIR, nc),
        in_specs=in_specs,
        out_specs=[pl.BlockSpec((MAMBA_PAIR, CHUNK, D_INNER), chunk),
                   pl.BlockSpec((MAMBA_PAIR, D_INNER, D_STATE), per_b)],
        out_shape=[jax.ShapeDtypeStruct((batch, t, D_INNER), BF16),
                   jax.ShapeDtypeStruct((batch, D_INNER, D_STATE), F32)],
        scratch_shapes=[pltpu.VMEM((MAMBA_PAIR, CONV_PAD, CONV_CH), F32),
                        pltpu.VMEM((MAMBA_PAIR, D_STATE, D_INNER), F32)],
        compiler_params=_params(2),
        name="mamba",
    )(*ins)
    return y.reshape(batch * t, D_INNER), h_out


DSA_TQ = LANES
DSA_KB = 512
DSA_PAIR = 2
IDX_CAT = 4 * IDX_DIM
DSA_RANK_ROWS = 128


def _hi_lo(x):
    hi = x.astype(BF16).astype(F32)
    return hi, (x - hi).astype(BF16).astype(F32)


def _dsa_body(qb_ref, qi_ref, dtwi_ref, ki_ref, k_ref, v_ref, o_ref,
              kcat_scr, kbf_scr, vt_scr, tri_scr, key_scr, mask_scr, s_scr,
              *, n_kb, n_valid, q_pos0, topk, visible):
    tq, kb_rows = DSA_TQ, DSA_KB
    j = pl.program_id(1)

    @pl.when(j == 0)
    def _():
        hi, lo = _hi_lo(ki_ref[...])
        kcat_scr[...] = jnp.concatenate([hi, lo, hi, jnp.zeros_like(hi)], axis=1).astype(BF16)
        kbf_scr[...] = k_ref[...].astype(BF16)
        for kb in range(n_kb):
            vt = v_ref[kb * kb_rows:(kb + 1) * kb_rows, :].T
            for g in range(KV_HEADS):
                vt_scr[g, kb] = vt[g * DSA_HEAD_DIM:(g + 1) * DSA_HEAD_DIM, :].astype(BF16)
        tr = lax.broadcasted_iota(jnp.int32, (DSA_RANK_ROWS, DSA_RANK_ROWS), 0)
        tc = lax.broadcasted_iota(jnp.int32, (DSA_RANK_ROWS, DSA_RANK_ROWS), 1)
        tri_scr[...] = jnp.where(tc < tr, 1.0, 0.0).astype(BF16)

    n_q = qb_ref.shape[0] // tq
    last_end = jnp.minimum((((q_pos0 + (j + 1) * n_q * tq - 1) >> 6) + 1) << 6, n_valid)
    nkb = (last_end + (kb_rows - 1)) // kb_rows
    krow = lax.broadcasted_iota(jnp.int32, (kb_rows, 1), 0)

    def prepare(qn):
        rows = slice(qn * tq, (qn + 1) * tq)
        qpos = q_pos0 + (j * n_q + qn) * tq + lax.broadcasted_iota(jnp.int32, (1, tq), 1)
        q_end = jnp.minimum(((qpos >> 6) + 1) << 6, n_valid)
        wit = (dtwi_ref[rows, :] * (IDX_DIM ** -0.5 * IDX_HEADS ** -0.5)).T
        qi = qi_ref[rows, :]
        qparts = []
        for h in range(IDX_HEADS):
            hi, lo = _hi_lo(qi[:, h * IDX_DIM:(h + 1) * IDX_DIM])
            qparts.append(jnp.concatenate([hi, hi, lo, jnp.zeros_like(hi)], axis=1))
        qb = qb_ref[rows, :]
        q4 = [jnp.concatenate([qb[:, (g * KV_REP + r) * DSA_HEAD_DIM:(g * KV_REP + r + 1) * DSA_HEAD_DIM]
                               for r in range(KV_REP)], axis=0) for g in range(KV_HEADS)]
        return dict(rows=rows, q_end=q_end, k_eff=jnp.minimum(q_end, topk).astype(F32), wit=wit,
                    qcat=jnp.concatenate(qparts, axis=0).astype(BF16), q4=q4)

    qs = [prepare(qn) for qn in range(n_q)]

    def run(n_vis):
        for qn, q in enumerate(qs):
            for kb in range(n_vis):
                logit = _mm_nt(kcat_scr[kb * kb_rows:(kb + 1) * kb_rows, :], q["qcat"])
                sc = jnp.zeros((kb_rows, tq), F32)
                for h in range(IDX_HEADS):
                    sc = sc + (jnp.maximum(logit[:, h * tq:(h + 1) * tq], 0.0)
                               * q["wit"][WI_LANE + h:WI_LANE + h + 1, :])
                sc = jnp.where(kb * kb_rows + krow < q["q_end"], sc, -jnp.inf)
                bits = lax.bitcast_convert_type(sc, jnp.int32)
                key_scr[qn, kb] = jnp.where(bits < 0, bits ^ jnp.int32(0x7FFFFFFF), bits)

        def count(qn, pred):
            acc = _fold_rows(jnp.where(pred(key_scr[qn, 0]), 1.0, 0.0), jnp.add)
            for kb in range(1, n_vis):
                acc = acc + _fold_rows(jnp.where(pred(key_scr[qn, kb]), 1.0, 0.0), jnp.add)
            return jnp.sum(acc, axis=0, keepdims=True)

        def radix_step(i, tus):
            bit = lax.shift_left(jnp.int32(1), 31 - i)
            new = []
            for qn, q in enumerate(qs):
                cand = tus[qn] | bit
                thr_c = cand ^ jnp.int32(INT_MIN)
                new.append(jnp.where(count(qn, lambda k: k >= thr_c) >= q["k_eff"], cand, tus[qn]))
            return tuple(new)

        tus = lax.fori_loop(0, 32, radix_step, tuple(jnp.zeros((1, tq), jnp.int32) for _ in qs))

        def finish(qn, q, thr):
            need = q["k_eff"] - count(qn, lambda k: k > thr)
            before = jnp.zeros((1, tq), F32)
            for kb in range(n_vis):
                for r0 in range(0, kb_rows, DSA_RANK_ROWS):
                    key = key_scr[qn, kb, r0:r0 + DSA_RANK_ROWS, :]
                    eq = jnp.where(key == thr, 1.0, 0.0)
                    rank = _mm(tri_scr[...], eq.astype(BF16)) + before
                    take = jnp.where(key > thr, 1.0, jnp.where(rank < need, eq, 0.0))
                    mask_scr[qn, kb, r0:r0 + DSA_RANK_ROWS, :] = jnp.where(take > 0.0, 0.0, NEG_BIG)
                    before = before + jnp.sum(_fold_rows(eq, jnp.add), axis=0, keepdims=True)
            yield

            m8 = [jnp.full((SUBLANES, KV_REP * tq), NEG_BIG, F32) for _ in range(KV_HEADS)]
            for kb in range(n_vis):
                mask4 = jnp.concatenate([mask_scr[qn, kb]] * KV_REP, axis=1)
                for g in range(KV_HEADS):
                    kg = kbf_scr[kb * kb_rows:(kb + 1) * kb_rows, g * DSA_HEAD_DIM:(g + 1) * DSA_HEAD_DIM]
                    st = _mm_nt(kg, q["q4"][g]) + mask4
                    s_scr[qn, kb, g] = st
                    m8[g] = jnp.maximum(m8[g], _fold_rows(st, jnp.maximum))
            m = [jnp.max(m8[g], axis=0, keepdims=True) for g in range(KV_HEADS)]
            yield

            acc, den = [None] * KV_HEADS, [None] * KV_HEADS
            for kb in range(n_vis):
                for g in range(KV_HEADS):
                    p = jnp.exp2(s_scr[qn, kb, g] - m[g])
                    pv = _mm(vt_scr[g, kb], p.astype(BF16))
                    psum = _fold_rows(p, jnp.add)
                    acc[g] = pv if acc[g] is None else acc[g] + pv
                    den[g] = psum if den[g] is None else den[g] + psum
            outs = []
            for g in range(KV_HEADS):
                ot = acc[g] / jnp.sum(den[g], axis=0, keepdims=True)
                outs += [ot[:, r * tq:(r + 1) * tq].T for r in range(KV_REP)]
            o_ref[q["rows"], :] = jnp.concatenate(outs, axis=1).astype(o_ref.dtype)
            yield

        for _ in zip(*[finish(qn, q, tus[qn] ^ jnp.int32(INT_MIN)) for qn, q in enumerate(qs)]):
            pass

    if len(visible) == 1:
        run(visible[0])
    else:
        for n_vis in visible:
            pl.when(nkb == n_vis)(functools.partial(run, n_vis))


def _dsa(qb, qi, dtwi, ki_all, k_all, v_all, key_layer, batch, t, s_keys, n_valid, q_pos0):
    tq, kb_rows = DSA_TQ, DSA_KB
    assert t % tq == 0 and s_keys % kb_rows == 0
    n_q = DSA_PAIR if (t // tq) % DSA_PAIR == 0 else 1
    nq = t // (n_q * tq)
    n_kb = s_keys // kb_rows
    topk = min(TOPK_MAX, n_valid // 4)
    qrow = lambda b, j: (b * nq + j, 0)
    krow = lambda b, j: (key_layer, b, 0)

    def visible_blocks(j):
        last_end = min(((((q_pos0 + (j + 1) * n_q * tq - 1) >> 6) + 1) << 6), n_valid)
        return (last_end + kb_rows - 1) // kb_rows

    visible = tuple(sorted({visible_blocks(j) for j in range(nq)}))
    return pl.pallas_call(
        functools.partial(_dsa_body, n_kb=n_kb, n_valid=n_valid, q_pos0=q_pos0, topk=topk, visible=visible),
        grid=(batch, nq),
        in_specs=[pl.BlockSpec((n_q * tq, D_MODEL), qrow), pl.BlockSpec((n_q * tq, IDX_HEADS * IDX_DIM), qrow),
                  pl.BlockSpec((n_q * tq, LANES), qrow), pl.BlockSpec((None, s_keys, IDX_DIM), krow),
                  pl.BlockSpec((None, s_keys, KV_DIM), krow), pl.BlockSpec((None, s_keys, KV_DIM), krow)],
        out_specs=pl.BlockSpec((n_q * tq, D_MODEL), qrow),
        out_shape=jax.ShapeDtypeStruct((batch * t, D_MODEL), BF16),
        scratch_shapes=[pltpu.VMEM((s_keys, IDX_CAT), BF16), pltpu.VMEM((s_keys, KV_DIM), BF16),
                        pltpu.VMEM((KV_HEADS, n_kb, DSA_HEAD_DIM, kb_rows), BF16),
                        pltpu.VMEM((DSA_RANK_ROWS, DSA_RANK_ROWS), BF16),
                        pltpu.VMEM((n_q, n_kb, kb_rows, tq), jnp.int32), pltpu.VMEM((n_q, n_kb, kb_rows, tq), F32),
                        pltpu.VMEM((n_q, n_kb, KV_HEADS, kb_rows, KV_REP * tq), F32)],
        compiler_params=_params(2),
        name="dsa",
    )(qb, qi, dtwi, ki_all, k_all, v_all)


def _band_body(*refs, tq, sub, n_kblk, q_pos0, k_min, clamped):
    q_ref = refs[0]
    k_refs = refs[1:1 + n_kblk]
    v_refs = refs[1 + n_kblk:1 + 2 * n_kblk]
    vec_ref, o_ref, bias_scr = refs[1 + 2 * n_kblk:]
    w = sum(r.shape[0] for r in k_refs)
    wsub = w - tq + sub
    i = pl.program_id(1)

    @pl.when((pl.program_id(0) == 0) & (i == 0))
    def _():
        r = lax.broadcasted_iota(jnp.int32, (sub, wsub), 0)
        c = lax.broadcasted_iota(jnp.int32, (sub, wsub), 1)
        dchunk = (r >> 6) + (wsub - sub) // CHUNK - (c >> 6)
        band_mask = jnp.where((dchunk >= 0) & (dchunk <= LEFT_CHUNKS), 0.0, NEG_BIG)
        for h in range(BAND_HEADS):
            rows = jnp.broadcast_to(vec_ref[h:h + 1, :], (sub, vec_ref.shape[1]))
            toeplitz = pltpu.roll(rows, 0, 1, stride=1, stride_axis=0)[:, :wsub]
            bias_scr[h // 2, (h % 2) * sub:(h % 2 + 1) * sub, :] = toeplitz * LOG2E + band_mask

    lane = lax.broadcasted_iota(jnp.int32, (sub, LANES), 1)
    first_head = lane < BAND_HEAD_DIM
    keep_a = jnp.where(first_head, 1.0, 0.0).astype(BF16)
    keep_b = jnp.where(first_head, 0.0, 1.0).astype(BF16)
    ones = jnp.ones((w, LANES), BF16)
    n_sub = tq // sub

    def all_rows(blocks, lanes):
        parts = [blk[:, lanes] for blk in blocks]
        return parts[0] if len(parts) == 1 else jnp.concatenate(parts, axis=0)

    def scores(hp):
        lanes = slice(hp * LANES, (hp + 1) * LANES)
        q2 = []
        for c2 in range(n_sub):
            qp = q_ref[c2 * sub:(c2 + 1) * sub, lanes]
            q2 += [qp * keep_a, qp * keep_b]
        return _mm_nt(jnp.concatenate(q2, axis=0), all_rows(k_refs, lanes))

    def attend(hp, s_full, mask_missing_keys):
        lanes = slice(hp * LANES, (hp + 1) * LANES)
        ps = []
        for c2 in range(n_sub):
            s = s_full[c2 * 2 * sub:(c2 + 1) * 2 * sub, c2 * sub:c2 * sub + wsub] + bias_scr[hp]
            if mask_missing_keys:
                kpos = q_pos0 + i * tq + c2 * sub + (sub - wsub) + lax.broadcasted_iota(jnp.int32, (1, wsub), 1)
                s = s + jnp.where(kpos >= k_min, 0.0, NEG_BIG)
            p = jnp.exp2(s - jnp.max(s, axis=1, keepdims=True)).astype(BF16)
            pad = [jnp.zeros((2 * sub, c2 * sub), BF16)] if c2 else []
            pad_r = [jnp.zeros((2 * sub, w - wsub - c2 * sub), BF16)] if w - wsub - c2 * sub else []
            ps.append(jnp.concatenate(pad + [p] + pad_r, axis=1) if pad or pad_r else p)
        p_full = ps[0] if n_sub == 1 else jnp.concatenate(ps, axis=0)
        o = _mm(p_full, jnp.concatenate([all_rows(v_refs, lanes), ones], axis=1))
        for c2 in range(n_sub):
            ra, rb = c2 * 2 * sub, c2 * 2 * sub + sub
            oa = o[ra:ra + sub, :LANES] / o[ra:ra + sub, LANES:LANES + 1]
            ob = o[rb:rb + sub, :LANES] / o[rb:rb + sub, LANES:LANES + 1]
            o_ref[c2 * sub:(c2 + 1) * sub, lanes] = jnp.where(first_head, oa, ob).astype(o_ref.dtype)

    def heads(mask_missing_keys):
        n_pairs, ahead, pending = BAND_HEADS // 2, 2, {}
        for n in range(n_pairs + ahead):
            if n < n_pairs:
                pending[n] = scores(n)
            if n >= ahead:
                attend(n - ahead, pending.pop(n - ahead), mask_missing_keys)

    if clamped:
        first_full = -(-(w - tq) // tq)
        pl.when(i < first_full)(functools.partial(heads, True))
        pl.when(i >= first_full)(functools.partial(heads, False))
    else:
        heads(False)


def _band_bias_vec(rel_bias, sub, wsub):
    l = -(-(wsub + sub) // LANES) * LANES
    m = np.arange(l)
    d = np.where(m < wsub, m, m - l)
    rel = np.clip(wsub - sub - d, -REL_CLIP, REL_CLIP) + REL_CLIP
    return rel_bias[jnp.asarray(rel)].T


def _band(q, k, v, rel_bias, batch, t, tq, sub, k_block_rows, n_kblk, q_pos0, k_min, clamped):
    nq = t // tq
    w = n_kblk * k_block_rows
    wsub = w - tq + sub
    assert wsub % LANES == 0 and (wsub - sub) % CHUNK == 0
    qrow = lambda b, i: (b * nq + i, 0)

    def krow(off):
        if clamped:
            return lambda b, i: (b * nq + jnp.maximum(i - (n_kblk - 1) + off, 0), 0)
        return lambda b, i: (b * n_kblk + off, 0)

    vec = _band_bias_vec(rel_bias, sub, wsub)
    kspecs = [pl.BlockSpec((k_block_rows, D_MODEL), krow(o)) for o in range(n_kblk)]
    return pl.pallas_call(
        functools.partial(_band_body, tq=tq, sub=sub, n_kblk=n_kblk, q_pos0=q_pos0, k_min=k_min, clamped=clamped),
        grid=(batch, nq),
        in_specs=[pl.BlockSpec((tq, D_MODEL), qrow)] + kspecs + kspecs + [_resident(vec.shape)],
        out_specs=pl.BlockSpec((tq, D_MODEL), qrow),
        out_shape=jax.ShapeDtypeStruct((batch * t, D_MODEL), BF16),
        scratch_shapes=[pltpu.VMEM((BAND_HEADS // 2, 2 * sub, wsub), F32)],
        compiler_params=_params(2),
        name="band",
    )(q, *([k] * n_kblk), *([v] * n_kblk), vec)


def _merge_body(x_ref, ya_ref, yb_ref, yc_ref, g_ref, wg_ref, bg_ref, wbr_ref, wo_ref, o_ref):
    x = x_ref[...]
    u = _rms(x, g_ref[2:3, :]).astype(BF16)
    mix = jnp.zeros(x.shape, F32)
    for k, y_ref in enumerate((ya_ref, yb_ref, yc_ref)):
        sl = slice(k * D_MODEL, (k + 1) * D_MODEL)
        gate = _sigmoid(_mm(u, wg_ref[:, sl]) + bg_ref[:, sl])
        mix = mix + gate * _mm(y_ref[...], wbr_ref[k])
    o_ref[...] = x + _rms(_mm(mix.astype(BF16), wo_ref[...]), g_ref[3:4, :])


def _merge(x, ya, yb, yc, pw, layer):
    n = x.shape[0]
    tm = _row_tile(n, 512)
    row = lambda i: (i, 0)
    tile = pl.BlockSpec((tm, D_MODEL), row)
    return pl.pallas_call(
        _merge_body,
        grid=(n // tm,),
        in_specs=[tile, tile, tile, tile]
        + [_resident_slice(pw[k], (layer,)) for k in ("g", "wg", "bg", "wbr", "wo")],
        out_specs=tile,
        out_shape=jax.ShapeDtypeStruct((n, D_MODEL), F32),
        compiler_params=_params(1),
        name="merge",
    )(x, ya, yb, yc, pw["g"], pw["wg"], pw["bg"], pw["wbr"], pw["wo"])


def _ple_body(x_ref, p_ref, g_ref, wp_ref, wpg_ref, o_ref):
    x = x_ref[...]
    e = _mm(p_ref[...].astype(BF16), wp_ref[...])
    pg = _sigmoid(_mm(_rms(x, g_ref[6:7, :]).astype(BF16), wpg_ref[...]))
    o_ref[...] = x + _rms(pg * e, g_ref[7:8, :])


def _ple(x, p, pw, layer):
    n = x.shape[0]
    tm = _row_tile(n, 512)
    row = lambda i: (i, 0)
    return pl.pallas_call(
        _ple_body,
        grid=(n // tm,),
        in_specs=[pl.BlockSpec((tm, D_MODEL), row), pl.BlockSpec((tm, PLE_DIM), row)]
        + [_resident_slice(pw[k], (layer,)) for k in ("g", "wp", "wpg")],
        out_specs=pl.BlockSpec((tm, D_MODEL), row),
        out_shape=jax.ShapeDtypeStruct((n, D_MODEL), F32),
        compiler_params=_params(1),
        name="ple",
    )(x, p, pw["g"], pw["wp"], pw["wpg"])


BAND_TQ = 256
BAND_SUB = 128
BAND_KBLK = 1 + -(-BAND // BAND_TQ)


def _prep_weights(norm_g, ffn_w13, ffn_w2, w_in, w_gate, b_gate, w_branch, w_out, w_ple, w_ple_gate):
    bf = lambda a: a.astype(BF16)
    return {"g": norm_g, "w13": bf(ffn_w13), "w2": bf(ffn_w2), "w_in": _pack_w_in(w_in), "wg": bf(w_gate),
            "bg": b_gate.reshape(DEPTH, 1, N_BRANCH * D_MODEL), "wbr": bf(w_branch), "wo": bf(w_out),
            "wp": bf(w_ple), "wpg": bf(w_ple_gate)}


def _trunk_layer(x, p, w, pw, layer, cache, batch, t, shared):
    x = _ffn(x, pw, layer, 0, 0, 1)
    pr, shared = _inproj(x, pw, layer, batch, t, shared)
    if cache is None:
        ya, h_new = _mamba(pr["z"], pr["xbc"], pr["dtwi"], None, None, w, batch, t)
        yb = _dsa(pr["qb"], pr["qi"], pr["dtwi"], pr["ki"], pr["kb"], pr["vb"], layer, batch, t, t, t, 0)
        yc = _band(pr["qc"], pr["kc_bf"], pr["vc_bf"], w["rel_bias"], batch, t, BAND_TQ, BAND_SUB, BAND_TQ,
                   BAND_KBLK, 0, 0, True)
        conv_src = pr["xbc"].reshape(batch, t, CONV_CH)
    else:
        past = cache["dsa_k"].shape[1]
        conv0 = jnp.pad(cache["conv"], ((0, 0), (SUBLANES - (CONV_W - 1), 0), (0, 0)))
        h0 = cache["ssm"].reshape(batch, D_INNER, D_STATE)
        ya, h_new = _mamba(pr["z"], pr["xbc"], pr["dtwi"], conv0, h0, w, batch, t)

        n_valid = past + t
        s_keys = -(-n_valid // DSA_KB) * DSA_KB
        tq_pad = -(-t // DSA_TQ) * DSA_TQ

        def with_cache(c, new):
            width = new.shape[-1]
            a = jnp.concatenate([c.reshape(batch, past, width), new[layer].reshape(batch, t, width)], axis=1)
            return jnp.pad(a, ((0, 0), (0, s_keys - n_valid), (0, 0))).reshape(1, batch * s_keys, width)

        def pad_q(a):
            a = jnp.pad(a.reshape(batch, t, a.shape[-1]), ((0, 0), (0, tq_pad - t), (0, 0)))
            return a.reshape(batch * tq_pad, a.shape[-1])

        yb = _dsa(pad_q(pr["qb"]), pad_q(pr["qi"]), pad_q(pr["dtwi"]), with_cache(cache["idx_k"], pr["ki"]),
                  with_cache(cache["dsa_k"], pr["kb"]), with_cache(cache["dsa_v"], pr["vb"]),
                  0, batch, tq_pad, s_keys, n_valid, past)
        yb = yb.reshape(batch, tq_pad, D_MODEL)[:, :t].reshape(batch * t, D_MODEL)

        nrows = cache["band_k"].shape[1]
        k_rows = -(-(nrows + t) // LANES) * LANES
        lead = k_rows - nrows - t

        def with_band(c, new):
            a = jnp.concatenate([c.reshape(batch, nrows, D_MODEL).astype(BF16), new.reshape(batch, t, D_MODEL)],
                                axis=1)
            return jnp.pad(a, ((0, 0), (lead, 0), (0, 0))).reshape(batch * k_rows, D_MODEL)

        yc = _band(pr["qc"], with_band(cache["band_k"], pr["kc_bf"]), with_band(cache["band_v"], pr["vc_bf"]),
                   w["rel_bias"], batch, t, t, t, k_rows, 1, past, past - nrows, False)
        conv_src = jnp.concatenate([cache["conv"], pr["xbc"].reshape(batch, t, CONV_CH)], axis=1)

    x = _merge(x, ya, yb, yc, pw, layer)
    x = _ffn(x, pw, layer, 1, 4, 5)
    x = _ple(x, p.reshape(batch * t, PLE_DIM), pw, layer)

    state = (h_new.reshape(batch, SSM_HEADS, SSM_HEAD_DIM, D_STATE), conv_src[:, -(CONV_W - 1):])
    return x, state, shared


def _cache_outputs(shared, states, batch, t):
    band_rows = min(BAND, t)
    return (shared["kb"].reshape(DEPTH, batch, t, KV_HEADS, DSA_HEAD_DIM),
            shared["vb"].reshape(DEPTH, batch, t, KV_HEADS, DSA_HEAD_DIM),
            shared["ki"].reshape(DEPTH, batch, t, IDX_DIM),
            shared["kc"].reshape(DEPTH, batch, band_rows, BAND_HEADS, BAND_HEAD_DIM),
            shared["vc"].reshape(DEPTH, batch, band_rows, BAND_HEADS, BAND_HEAD_DIM),
            jnp.stack([s[0] for s in states]), jnp.stack([s[1] for s in states]))


def kernel(x_prompt, x_sample, p_prompt, p_sample, cache_dsa_k, cache_dsa_v, cache_idx_k, cache_band_k,
           cache_band_v, state_ssm, state_conv, norm_g, ffn_w13, ffn_w2, w_in, conv_w, conv_b, dt_bias,
           a_log, d_skip, ssm_norm_g, rel_bias, w_gate, b_gate, w_branch, w_out, w_ple, w_ple_gate):
    bp, tp, _ = x_prompt.shape
    bs, ts, _ = x_sample.shape
    yp = x_prompt.reshape(bp * tp, D_MODEL)
    ys = x_sample.reshape(bs * ts, D_MODEL)
    st_p, st_s, shared_p, shared_s = [], [], None, None
    pw = _prep_weights(norm_g, ffn_w13, ffn_w2, w_in, w_gate, b_gate, w_branch, w_out, w_ple, w_ple_gate)
    for i in range(DEPTH):
        w = {"conv_w": conv_w[i], "conv_b": conv_b[i], "dt_bias": dt_bias[i], "a_log": a_log[i],
             "d_skip": d_skip[i], "ssm_norm_g": ssm_norm_g[i], "rel_bias": rel_bias[i]}
        yp, sp, shared_p = _trunk_layer(yp, p_prompt[i], w, pw, i, None, bp, tp, shared_p)
        st_p.append(sp)
        cache = {"dsa_k": cache_dsa_k[i], "dsa_v": cache_dsa_v[i], "idx_k": cache_idx_k[i],
                 "band_k": cache_band_k[i], "band_v": cache_band_v[i], "ssm": state_ssm[i], "conv": state_conv[i]}
        ys, ss, shared_s = _trunk_layer(ys, p_sample[i], w, pw, i, cache, bs, ts, shared_s)
        st_s.append(ss)
    return (yp.reshape(bp, tp, D_MODEL), ys.reshape(bs, ts, D_MODEL),
            *_cache_outputs(shared_p, st_p, bp, tp), *_cache_outputs(shared_s, st_s, bs, ts))
```

```python
import functools
import math

import numpy as np
import jax
import jax.numpy as jnp
from jax import lax
from jax.experimental import pallas as pl
from jax.experimental.pallas import tpu as pltpu

F32 = jnp.float32
BF16 = jnp.bfloat16

D_MODEL = 1024
DEPTH = 2
CHUNK = 64
EPS = 1e-6
HALF = 0.5
D_FF = 2816
PLE_DIM = 256
SSM_HEAD_DIM = 64
D_INNER = D_MODEL
SSM_HEADS = D_INNER // SSM_HEAD_DIM
N_GROUPS = 4
HEADS_PER_GROUP = SSM_HEADS // N_GROUPS
D_STATE = 128
CONV_W = 4
CONV_CH = D_INNER + 2 * N_GROUPS * D_STATE
DSA_HEAD_DIM = 128
DSA_HEADS = D_MODEL // DSA_HEAD_DIM
KV_HEADS = 2
KV_REP = DSA_HEADS // KV_HEADS
IDX_HEADS = 4
IDX_DIM = 64
TOPK_MAX = 256
BAND_HEAD_DIM = 64
BAND_HEADS = D_MODEL // BAND_HEAD_DIM
LEFT_CHUNKS = 8
BAND = LEFT_CHUNKS * CHUNK
REL_CLIP = 256
N_BRANCH = 3
IN_WIDTHS = (D_INNER, CONV_CH, SSM_HEADS,
             DSA_HEADS * DSA_HEAD_DIM, KV_HEADS * DSA_HEAD_DIM, KV_HEADS * DSA_HEAD_DIM,
             IDX_HEADS * IDX_DIM, IDX_DIM, IDX_HEADS,
             BAND_HEADS * BAND_HEAD_DIM, BAND_HEADS * BAND_HEAD_DIM, BAND_HEADS * BAND_HEAD_DIM)
IN_SPLITS = tuple(int(s) for s in np.cumsum(IN_WIDTHS)[:-1])

LANES = 128
SUBLANES = 8
KV_DIM = KV_HEADS * DSA_HEAD_DIM
GROUP_CH = D_INNER // N_GROUPS
NEG_BIG = -1e30
INT_MIN = -2 ** 31
LOG2E = math.log2(math.e)
VMEM_LIMIT = 56 * 1024 * 1024


def _mm(a, b):
    return jnp.dot(a, b, preferred_element_type=F32)


def _mm_nt(a, b):
    return lax.dot_general(a, b, (((1,), (1,)), ((), ())), preferred_element_type=F32)


def _mm_tn(a, b):
    return lax.dot_general(a, b, (((0,), (0,)), ((), ())), preferred_element_type=F32)


def _rms(x, g):
    return x * lax.rsqrt(jnp.mean(x * x, axis=-1, keepdims=True) + EPS) * g


def _sigmoid(x):
    return 1.0 / (1.0 + jnp.exp(-x))


def _silu(x):
    return x * _sigmoid(x)


def _resident(shape):
    return pl.BlockSpec(shape, lambda *_: (0,) * len(shape), pipeline_mode=pl.Buffered(1))


def _resident_slice(arr, lead, block=None, at=None):
    tail = tuple(arr.shape[len(lead):]) if block is None else tuple(block)
    idx = tuple(lead) + ((0,) * len(tail) if at is None else tuple(at))
    return pl.BlockSpec((None,) * len(lead) + tail, lambda *_: idx, pipeline_mode=pl.Buffered(1))


def _params(n_grid_dims):
    return pltpu.CompilerParams(dimension_semantics=("arbitrary",) * n_grid_dims,
                                vmem_limit_bytes=VMEM_LIMIT)


def _row_tile(n_rows, want):
    t = min(want, n_rows)
    assert n_rows % t == 0
    return t


def _fold_rows(x, op):
    parts = [x[i * SUBLANES:(i + 1) * SUBLANES] for i in range(x.shape[0] // SUBLANES)]
    while len(parts) > 1:
        parts = [op(parts[i], parts[i + 1]) for i in range(0, len(parts) - 1, 2)] + parts[len(parts) & ~1:]
    return parts[0]


FF_CHUNK = 256


def _ffn_body(x_ref, g_ref, wa_ref, wb_ref, w2_ref, o_ref, *, g_pre, g_post):
    x = x_ref[...]
    u = _rms(x, g_ref[g_pre:g_pre + 1, :]).astype(BF16)
    acc = jnp.zeros(x.shape, F32)
    for c in range(D_FF // FF_CHUNK):
        sl = slice(c * FF_CHUNK, (c + 1) * FF_CHUNK)
        a = _mm(u, wa_ref[:, sl])
        b = _mm(u, wb_ref[:, sl])
        acc = acc + _mm((_silu(a) * b).astype(BF16), w2_ref[sl, :])
    o_ref[...] = x + HALF * _rms(acc, g_ref[g_post:g_post + 1, :])


def _ffn(x, pw, layer, j, g_pre, g_post):
    n = x.shape[0]
    tm = _row_tile(n, 1024)
    row = lambda i: (i, 0)
    half = (D_MODEL, D_FF)
    return pl.pallas_call(
        functools.partial(_ffn_body, g_pre=g_pre, g_post=g_post),
        grid=(n // tm,),
        in_specs=[pl.BlockSpec((tm, D_MODEL), row), _resident_slice(pw["g"], (layer,)),
                  _resident_slice(pw["w13"], (layer, j), half, (0, 0)),
                  _resident_slice(pw["w13"], (layer, j), half, (0, 1)),
                  _resident_slice(pw["w2"], (layer, j))],
        out_specs=pl.BlockSpec((tm, D_MODEL), row),
        out_shape=jax.ShapeDtypeStruct((n, D_MODEL), F32),
        compiler_params=_params(1),
        name="ffn",
    )(x, pw["g"], pw["w13"], pw["w13"], pw["w2"])


_INPROJ_GROUPS = (D_INNER, CONV_CH, D_MODEL, KV_DIM, KV_DIM, IDX_HEADS * IDX_DIM, D_MODEL, D_MODEL, D_MODEL,
                  LANES, LANES)
_INPROJ_OUT = (
    ("z", 0, D_INNER, F32, None, "layer"),
    ("xbc", 1, CONV_CH, F32, None, "layer"),
    ("qb", 2, D_MODEL, BF16, DSA_HEAD_DIM ** -0.5 * LOG2E, "layer"),
    ("kb", 3, KV_DIM, F32, None, "stack"),
    ("vb", 4, KV_DIM, F32, None, "stack"),
    ("qi", 5, IDX_HEADS * IDX_DIM, F32, None, "layer"),
    ("qc", 6, D_MODEL, BF16, BAND_HEAD_DIM ** -0.5 * LOG2E, "layer"),
    ("kc", 7, D_MODEL, F32, None, "tail"),
    ("kc_bf", 7, D_MODEL, BF16, None, "layer"),
    ("vc", 8, D_MODEL, F32, None, "tail"),
    ("vc_bf", 8, D_MODEL, BF16, None, "layer"),
    ("ki", 9, IDX_DIM, F32, None, "stack"),
    ("dtwi", 10, LANES, F32, None, "layer"),
)
WI_LANE = SSM_HEADS


def _pack_w_in(w_in):
    z, xbc, dt, qb, kb, vb, qi, ki, wi, qc, kc, vc = jnp.split(w_in, IN_SPLITS, axis=-1)
    pad = lambda w: jnp.pad(w, ((0, 0), (0, 0), (0, LANES - w.shape[-1])))
    cols = [z, xbc, qb, kb, vb, qi, qc, kc, vc, pad(ki), pad(jnp.concatenate([dt, wi], axis=-1))]
    return jnp.concatenate(cols, axis=-1).astype(BF16)


def _inproj_body(x_ref, g_ref, w_ref, *refs):
    out_refs = refs[len(refs) - len(_INPROJ_OUT):]
    u = _rms(x_ref[...], g_ref[2:3, :]).astype(BF16)
    starts = np.concatenate([[0], np.cumsum(_INPROJ_GROUPS)])
    for grp, width in enumerate(_INPROJ_GROUPS):
        r = _mm(u, w_ref[:, int(starts[grp]):int(starts[grp]) + width])
        for (_, og, stored, dtype, scale, _), o_ref in zip(_INPROJ_OUT, out_refs):
            if og == grp:
                v = r if scale is None else r * scale
                o_ref[...] = (v if stored == width else v[:, :stored]).astype(dtype)


def _inproj(x, pw, layer, batch, t, shared):
    n = x.shape[0]
    tm = _row_tile(n, 256)
    tail = min(BAND, t)
    row = lambda i: (i, 0)
    stack_row = lambda i: (layer, i, 0)
    if tail == t:
        tail_row = stack_row
    else:
        assert t % tm == 0 and tail % tm == 0
        per_seq, per_tail = t // tm, tail // tm
        tail_row = lambda i: (layer, (i // per_seq) * per_tail + jnp.maximum(i % per_seq - (per_seq - per_tail), 0), 0)
    out_specs, out_shape, stacked = [], [], []
    for k, (name, _, width, dtype, _, kind) in enumerate(_INPROJ_OUT):
        if kind == "layer":
            out_specs.append(pl.BlockSpec((tm, width), row))
            out_shape.append(jax.ShapeDtypeStruct((n, width), dtype))
        else:
            rows = n if kind == "stack" else batch * tail
            out_specs.append(pl.BlockSpec((None, tm, width), stack_row if kind == "stack" else tail_row))
            out_shape.append(jax.ShapeDtypeStruct((DEPTH, rows, width), dtype))
            stacked.append((name, k))
    carried = [] if shared is None else [shared[name] for name, _ in stacked]
    aliases = {} if shared is None else {3 + a: k for a, (_, k) in enumerate(stacked)}
    outs = pl.pallas_call(
        _inproj_body,
        grid=(n // tm,),
        in_specs=[pl.BlockSpec((tm, D_MODEL), row), _resident_slice(pw["g"], (layer,)),
                  _resident_slice(pw["w_in"], (layer,))] + [pl.BlockSpec(memory_space=pl.ANY)] * len(carried),
        out_specs=out_specs,
        out_shape=out_shape,
        input_output_aliases=aliases,
        compiler_params=_params(1),
        name="inproj",
    )(x, pw["g"], pw["w_in"], *carried)
    pr = {o[0]: a for o, a in zip(_INPROJ_OUT, outs)}
    return pr, {name: pr[name] for name, _ in stacked}


CONV_PAD = 16
CONV_ROWS = CONV_PAD + CHUNK
MAMBA_PAIR = 4


def _split3(x):
    hi = x.astype(BF16)
    r = x - hi.astype(F32)
    mid = r.astype(BF16)
    lo = (r - mid.astype(F32)).astype(BF16)
    return hi, mid, lo


def _expand_heads(x, e):
    hi, mid, lo = _split3(x)
    return _mm(hi, e) + _mm(mid, e) + _mm(lo, e)


def _cumsum_rows(x):
    n = x.shape[0]
    row = lax.broadcasted_iota(jnp.int32, x.shape, 0)
    d = 1
    while d < n:
        x = x + jnp.where(row >= d, pltpu.roll(x, d, 0), 0.0)
        d *= 2
    return x


def _conv_shift_matrix():
    s = np.zeros(((CONV_W - 1) * CHUNK, 3 * CONV_ROWS), np.float32)
    for k in range(CONV_W - 1):
        for l in range(CHUNK):
            for part in range(3):
                s[k * CHUNK + l, part * CONV_ROWS + CONV_PAD - (CONV_W - 1) + k + l] = 1.0
    return jnp.asarray(s, BF16)


def _mamba_body(*refs, has_state, n_chunks):
    if has_state:
        (z_ref, xbc_ref, dtwi_ref, conv0_ref, h0_ref, cw_ref, cb_ref, dtb_ref, alog_ref, dskip_ref,
         ng_ref, e_ref, shift_ref, y_ref, hout_ref, tail_scr, ht_scr) = refs
    else:
        (z_ref, xbc_ref, dtwi_ref, cw_ref, cb_ref, dtb_ref, alog_ref, dskip_ref,
         ng_ref, e_ref, shift_ref, y_ref, hout_ref, tail_scr, ht_scr) = refs
    c = pl.program_id(1)
    q = CHUNK
    tail_rows = slice(CONV_PAD - SUBLANES, CONV_PAD)

    @pl.when(c == 0)
    def _():
        tail_scr[...] = jnp.zeros(tail_scr.shape, F32)
        if has_state:
            for b in range(MAMBA_PAIR):
                tail_scr[b, tail_rows, :] = conv0_ref[b]
                ht_scr[b] = h0_ref[b].T
        else:
            ht_scr[...] = jnp.zeros(ht_scr.shape, F32)

    lane = lax.broadcasted_iota(jnp.int32, (q, LANES), 1)
    li = lax.broadcasted_iota(jnp.int32, (q, D_INNER), 0)
    si = lax.broadcasted_iota(jnp.int32, (q, D_INNER), 1) & (q - 1)
    diag = li == si
    causal = (li >= si)[:, :GROUP_CH]
    bdr = lax.broadcasted_iota(jnp.int32, (GROUP_CH, GROUP_CH), 0) // SSM_HEAD_DIM
    bdc = lax.broadcasted_iota(jnp.int32, (GROUP_CH, GROUP_CH), 1) // SSM_HEAD_DIM
    block_diag = bdr == bdc
    neg_a = -jnp.exp(alog_ref[...])

    def sequence(b):
        x = xbc_ref[b]
        hi, mid, lo = _split3(jnp.concatenate([tail_scr[b], x], axis=0))
        delayed = _mm(shift_ref[...], jnp.concatenate([hi, mid, lo], axis=0))
        tail_scr[b, tail_rows, :] = x[q - SUBLANES:q, :]
        pre = dtwi_ref[b] + dtb_ref[...]
        dt = jnp.maximum(pre, 0.0) + jnp.log1p(jnp.exp(-jnp.abs(pre)))
        dt = jnp.where(lane < SSM_HEADS, dt, 0.0)
        cum = _cumsum_rows(dt * neg_a)
        e = e_ref[...]
        ecol = _expand_heads(cum, e)
        dtx = _expand_heads(dt, e)
        yield

        acc = cb_ref[...] + x * cw_ref[CONV_W - 1:CONV_W, :]
        for k in range(CONV_W - 1):
            acc = acc + delayed[k * q:(k + 1) * q, :] * cw_ref[k:k + 1, :]
        xc = _silu(acc)
        xs = xc[:, :D_INNER]
        bm = xc[:, D_INNER:D_INNER + N_GROUPS * D_STATE].astype(BF16)
        cm = xc[:, D_INNER + N_GROUPS * D_STATE:].astype(BF16)
        cbs = [_mm_nt(cm[:, g * D_STATE:(g + 1) * D_STATE], bm[:, g * D_STATE:(g + 1) * D_STATE])
               for g in range(N_GROUPS)]
        yield

        erow = jnp.sum(jnp.where(diag, ecol, 0.0), axis=0, keepdims=True)
        elast = ecol[q - 1:q, :]
        xdt = xs * dtx
        xdec = (xdt * jnp.exp(elast - ecol)).astype(BF16)
        exp_e = jnp.exp(ecol)
        chunk_decay = jnp.exp(elast)
        yield

        ys = []
        for g in range(N_GROUPS):
            sl = slice(g * GROUP_CH, (g + 1) * GROUP_CH)
            nl = slice(g * D_STATE, (g + 1) * D_STATE)
            cbt = jnp.concatenate([cbs[g]] * HEADS_PER_GROUP, axis=1)
            decay = jnp.exp(jnp.where(causal, ecol[:, sl] - erow[:, sl], NEG_BIG))
            m = (cbt * decay).astype(BF16)
            xg = xdt[:, sl]
            bd = jnp.where(block_diag, jnp.concatenate([xg] * HEADS_PER_GROUP, axis=0), 0.0).astype(BF16)
            y_diag = _mm(m, bd)
            ht_g = ht_scr[b, :, sl]
            y_off = _mm(cm[:, nl], ht_g.astype(BF16)) * exp_e[:, sl]
            ys.append(y_diag + y_off)
            ht_scr[b, :, sl] = ht_g * chunk_decay[:, sl] + _mm_tn(bm[:, nl], xdec[:, sl])
            yield

        y = jnp.concatenate(ys, axis=1) + dskip_ref[...] * xs
        y = y * _silu(z_ref[b])
        outs = []
        for g in range(N_GROUPS):
            yg = y[:, g * GROUP_CH:(g + 1) * GROUP_CH]
            outs.append(yg * lax.rsqrt(jnp.mean(yg * yg, axis=-1, keepdims=True) + EPS))
        y_ref[b] = (jnp.concatenate(outs, axis=1) * ng_ref[...]).astype(y_ref.dtype)
        yield

    for _ in zip(*[sequence(b) for b in range(MAMBA_PAIR)]):
        pass

    @pl.when(c == n_chunks - 1)
    def _():
        for b in range(MAMBA_PAIR):
            hout_ref[b] = ht_scr[b].T


def _head_expand_matrix():
    e = np.zeros((LANES, D_INNER), np.float32)
    for h in range(SSM_HEADS):
        e[h, h * SSM_HEAD_DIM:(h + 1) * SSM_HEAD_DIM] = 1.0
    return jnp.asarray(e, BF16)


def _mamba(z, xbc, dtwi, conv0, h0, lw, batch, t):
    nc = t // CHUNK
    assert batch % MAMBA_PAIR == 0
    has_state = h0 is not None
    chunk = lambda b, c: (b, c, 0)
    per_b = lambda b, c: (b, 0, 0)
    pad16 = lambda v: jnp.pad(v.reshape(1, SSM_HEADS), ((0, 0), (0, LANES - SSM_HEADS)))
    small = [lw["conv_w"], lw["conv_b"].reshape(1, CONV_CH), pad16(lw["dt_bias"]), pad16(lw["a_log"]),
             jnp.repeat(lw["d_skip"], SSM_HEAD_DIM).reshape(1, D_INNER),
             lw["ssm_norm_g"].reshape(1, D_INNER), _head_expand_matrix(), _conv_shift_matrix()]
    ins = [a.reshape(batch, t, a.shape[-1]) for a in (z, xbc, dtwi)]
    in_specs = [pl.BlockSpec((MAMBA_PAIR, CHUNK, a.shape[-1]), chunk) for a in ins]
    if has_state:
        ins += [conv0, h0]
        in_specs += [pl.BlockSpec((MAMBA_PAIR, SUBLANES, CONV_CH), per_b),
                     pl.BlockSpec((MAMBA_PAIR, D_INNER, D_STATE), per_b)]
    ins += small
    in_specs += [_resident(a.shape) for a in small]
    y, h_out = pl.pallas_call(
        functools.partial(_mamba_body, has_state=has_state, n_chunks=nc),
        grid=(batch // MAMBA_PAIR, nc),
        in_specs=in_specs,
        out_specs=[pl.BlockSpec((MAMBA_PAIR, CHUNK, D_INNER), chunk),
                   pl.BlockSpec((MAMBA_PAIR, D_INNER, D_STATE), per_b)],
        out_shape=[jax.ShapeDtypeStruct((batch, t, D_INNER), BF16),
                   jax.ShapeDtypeStruct((batch, D_INNER, D_STATE), F32)],
        scratch_shapes=[pltpu.VMEM((MAMBA_PAIR, CONV_PAD, CONV_CH), F32),
                        pltpu.VMEM((MAMBA_PAIR, D_STATE, D_INNER), F32)],
        compiler_params=_params(2),
        name="mamba",
    )(*ins)
    return y.reshape(batch * t, D_INNER), h_out


DSA_TQ = LANES
DSA_KB = 256
DSA_PAIR = 2
IDX_CAT = 4 * IDX_DIM
DSA_RANK_ROWS = 128


def _hi_lo(x):
    hi = x.astype(BF16).astype(F32)
    return hi, (x - hi).astype(BF16).astype(F32)


def _dsa_body(qb_ref, qi_ref, dtwi_ref, ki_ref, k_ref, v_ref, o_ref,
              kcat_scr, kbf_scr, vt_scr, tri_scr, key_scr, mask_scr, s_scr,
              *, n_kb, n_valid, q_pos0, topk, visible):
    tq, kb_rows = DSA_TQ, DSA_KB
    j = pl.program_id(1)

    @pl.when(j == 0)
    def _():
        hi, lo = _hi_lo(ki_ref[...])
        kcat_scr[...] = jnp.concatenate([hi, lo, hi, jnp.zeros_like(hi)], axis=1).astype(BF16)
        kbf_scr[...] = k_ref[...].astype(BF16)
        for kb in range(n_kb):
            vt = v_ref[kb * kb_rows:(kb + 1) * kb_rows, :].T
            for g in range(KV_HEADS):
                vt_scr[g, kb] = vt[g * DSA_HEAD_DIM:(g + 1) * DSA_HEAD_DIM, :].astype(BF16)
        tr = lax.broadcasted_iota(jnp.int32, (DSA_RANK_ROWS, DSA_RANK_ROWS), 0)
        tc = lax.broadcasted_iota(jnp.int32, (DSA_RANK_ROWS, DSA_RANK_ROWS), 1)
        tri_scr[...] = jnp.where(tc < tr, 1.0, 0.0).astype(BF16)

    n_q = qb_ref.shape[0] // tq
    last_end = jnp.minimum((((q_pos0 + (j + 1) * n_q * tq - 1) >> 6) + 1) << 6, n_valid)
    nkb = (last_end + (kb_rows - 1)) // kb_rows
    krow = lax.broadcasted_iota(jnp.int32, (kb_rows, 1), 0)

    def prepare(qn):
        rows = slice(qn * tq, (qn + 1) * tq)
        qpos = q_pos0 + (j * n_q + qn) * tq + lax.broadcasted_iota(jnp.int32, (1, tq), 1)
        q_end = jnp.minimum(((qpos >> 6) + 1) << 6, n_valid)
        wit = (dtwi_ref[rows, :] * (IDX_DIM ** -0.5 * IDX_HEADS ** -0.5)).T
        qi = qi_ref[rows, :]
        qparts = []
        for h in range(IDX_HEADS):
            hi, lo = _hi_lo(qi[:, h * IDX_DIM:(h + 1) * IDX_DIM])
            qparts.append(jnp.concatenate([hi, hi, lo, jnp.zeros_like(hi)], axis=1))
        qb = qb_ref[rows, :]
        q4 = [jnp.concatenate([qb[:, (g * KV_REP + r) * DSA_HEAD_DIM:(g * KV_REP + r + 1) * DSA_HEAD_DIM]
                               for r in range(KV_REP)], axis=0) for g in range(KV_HEADS)]
        return dict(rows=rows, q_end=q_end, k_eff=jnp.minimum(q_end, topk).astype(F32), wit=wit,
                    qcat=jnp.concatenate(qparts, axis=0).astype(BF16), q4=q4)

    qs = [prepare(qn) for qn in range(n_q)]

    def run(n_vis):
        for qn, q in enumerate(qs):
            for kb in range(n_vis):
                logit = _mm_nt(kcat_scr[kb * kb_rows:(kb + 1) * kb_rows, :], q["qcat"])
                sc = jnp.zeros((kb_rows, tq), F32)
                for h in range(IDX_HEADS):
                    sc = sc + (jnp.maximum(logit[:, h * tq:(h + 1) * tq], 0.0)
                               * q["wit"][WI_LANE + h:WI_LANE + h + 1, :])
                sc = jnp.where(kb * kb_rows + krow < q["q_end"], sc, -jnp.inf)
                bits = lax.bitcast_convert_type(sc, jnp.int32)
                key_scr[qn, kb] = jnp.where(bits < 0, bits ^ jnp.int32(0x7FFFFFFF), bits)

        def count(qn, pred):
            acc = _fold_rows(jnp.where(pred(key_scr[qn, 0]), 1.0, 0.0), jnp.add)
            for kb in range(1, n_vis):
                acc = acc + _fold_rows(jnp.where(pred(key_scr[qn, kb]), 1.0, 0.0), jnp.add)
            return jnp.sum(acc, axis=0, keepdims=True)

        def radix_step(i, tus):
            bit = lax.shift_left(jnp.int32(1), 31 - i)
            new = []
            for qn, q in enumerate(qs):
                cand = tus[qn] | bit
                thr_c = cand ^ jnp.int32(INT_MIN)
                new.append(jnp.where(count(qn, lambda k: k >= thr_c) >= q["k_eff"], cand, tus[qn]))
            return tuple(new)

        tus = lax.fori_loop(0, 32, radix_step, tuple(jnp.zeros((1, tq), jnp.int32) for _ in qs))

        def finish(qn, q, thr):
            need = q["k_eff"] - count(qn, lambda k: k > thr)
            before = jnp.zeros((1, tq), F32)
            for kb in range(n_vis):
                for r0 in range(0, kb_rows, DSA_RANK_ROWS):
                    key = key_scr[qn, kb, r0:r0 + DSA_RANK_ROWS, :]
                    eq = jnp.where(key == thr, 1.0, 0.0)
                    rank = _mm(tri_scr[...], eq.astype(BF16)) + before
                    take = jnp.where(key > thr, 1.0, jnp.where(rank < need, eq, 0.0))
                    mask_scr[qn, kb, r0:r0 + DSA_RANK_ROWS, :] = jnp.where(take > 0.0, 0.0, NEG_BIG)
                    before = before + jnp.sum(_fold_rows(eq, jnp.add), axis=0, keepdims=True)
            yield

            m8 = [jnp.full((SUBLANES, KV_REP * tq), NEG_BIG, F32) for _ in range(KV_HEADS)]
            for kb in range(n_vis):
                mask4 = jnp.concatenate([mask_scr[qn, kb]] * KV_REP, axis=1)
                for g in range(KV_HEADS):
                    kg = kbf_scr[kb * kb_rows:(kb + 1) * kb_rows, g * DSA_HEAD_DIM:(g + 1) * DSA_HEAD_DIM]
                    st = _mm_nt(kg, q["q4"][g]) + mask4
                    s_scr[qn, kb, g] = st
                    m8[g] = jnp.maximum(m8[g], _fold_rows(st, jnp.maximum))
            m = [jnp.max(m8[g], axis=0, keepdims=True) for g in range(KV_HEADS)]
            yield

            acc, den = [None] * KV_HEADS, [None] * KV_HEADS
            for kb in range(n_vis):
                for g in range(KV_HEADS):
                    p = jnp.exp2(s_scr[qn, kb, g] - m[g])
                    pv = _mm(vt_scr[g, kb], p.astype(BF16))
                    psum = _fold_rows(p, jnp.add)
                    acc[g] = pv if acc[g] is None else acc[g] + pv
                    den[g] = psum if den[g] is None else den[g] + psum
            outs = []
            for g in range(KV_HEADS):
                ot = acc[g] / jnp.sum(den[g], axis=0, keepdims=True)
                outs += [ot[:, r * tq:(r + 1) * tq].T for r in range(KV_REP)]
            o_ref[q["rows"], :] = jnp.concatenate(outs, axis=1).astype(o_ref.dtype)
            yield

        for _ in zip(*[finish(qn, q, tus[qn] ^ jnp.int32(INT_MIN)) for qn, q in enumerate(qs)]):
            pass

    if len(visible) == 1:
        run(visible[0])
    else:
        for n_vis in visible:
            pl.when(nkb == n_vis)(functools.partial(run, n_vis))


def _dsa(qb, qi, dtwi, ki_all, k_all, v_all, key_layer, batch, t, s_keys, n_valid, q_pos0):
    tq, kb_rows = DSA_TQ, DSA_KB
    assert t % tq == 0 and s_keys % kb_rows == 0
    n_q = DSA_PAIR if (t // tq) % DSA_PAIR == 0 else 1
    nq = t // (n_q * tq)
    n_kb = s_keys // kb_rows
    topk = min(TOPK_MAX, n_valid // 4)
    qrow = lambda b, j: (b * nq + j, 0)
    krow = lambda b, j: (key_layer, b, 0)

    def visible_blocks(j):
        last_end = min(((((q_pos0 + (j + 1) * n_q * tq - 1) >> 6) + 1) << 6), n_valid)
        return (last_end + kb_rows - 1) // kb_rows

    visible = tuple(sorted({visible_blocks(j) for j in range(nq)}))
    return pl.pallas_call(
        functools.partial(_dsa_body, n_kb=n_kb, n_valid=n_valid, q_pos0=q_pos0, topk=topk, visible=visible),
        grid=(batch, nq),
        in_specs=[pl.BlockSpec((n_q * tq, D_MODEL), qrow), pl.BlockSpec((n_q * tq, IDX_HEADS * IDX_DIM), qrow),
                  pl.BlockSpec((n_q * tq, LANES), qrow), pl.BlockSpec((None, s_keys, IDX_DIM), krow),
                  pl.BlockSpec((None, s_keys, KV_DIM), krow), pl.BlockSpec((None, s_keys, KV_DIM), krow)],
        out_specs=pl.BlockSpec((n_q * tq, D_MODEL), qrow),
        out_shape=jax.ShapeDtypeStruct((batch * t, D_MODEL), BF16),
        scratch_shapes=[pltpu.VMEM((s_keys, IDX_CAT), BF16), pltpu.VMEM((s_keys, KV_DIM), BF16),
                        pltpu.VMEM((KV_HEADS, n_kb, DSA_HEAD_DIM, kb_rows), BF16),
                        pltpu.VMEM((DSA_RANK_ROWS, DSA_RANK_ROWS), BF16),
                        pltpu.VMEM((n_q, n_kb, kb_rows, tq), jnp.int32), pltpu.VMEM((n_q, n_kb, kb_rows, tq), F32),
                        pltpu.VMEM((n_q, n_kb, KV_HEADS, kb_rows, KV_REP * tq), F32)],
        compiler_params=_params(2),
        name="dsa",
    )(qb, qi, dtwi, ki_all, k_all, v_all)


def _band_body(*refs, tq, sub, n_kblk, q_pos0, k_min, clamped):
    q_ref = refs[0]
    k_refs = refs[1:1 + n_kblk]
    v_refs = refs[1 + n_kblk:1 + 2 * n_kblk]
    vec_ref, o_ref, bias_scr = refs[1 + 2 * n_kblk:]
    w = sum(r.shape[0] for r in k_refs)
    wsub = w - tq + sub
    i = pl.program_id(1)

    @pl.when((pl.program_id(0) == 0) & (i == 0))
    def _():
        r = lax.broadcasted_iota(jnp.int32, (sub, wsub), 0)
        c = lax.broadcasted_iota(jnp.int32, (sub, wsub), 1)
        dchunk = (r >> 6) + (wsub - sub) // CHUNK - (c >> 6)
        band_mask = jnp.where((dchunk >= 0) & (dchunk <= LEFT_CHUNKS), 0.0, NEG_BIG)
        for h in range(BAND_HEADS):
            rows = jnp.broadcast_to(vec_ref[h:h + 1, :], (sub, vec_ref.shape[1]))
            toeplitz = pltpu.roll(rows, 0, 1, stride=1, stride_axis=0)[:, :wsub]
            bias_scr[h // 2, (h % 2) * sub:(h % 2 + 1) * sub, :] = toeplitz * LOG2E + band_mask

    lane = lax.broadcasted_iota(jnp.int32, (sub, LANES), 1)
    first_head = lane < BAND_HEAD_DIM
    keep_a = jnp.where(first_head, 1.0, 0.0).astype(BF16)
    keep_b = jnp.where(first_head, 0.0, 1.0).astype(BF16)
    ones = jnp.ones((w, LANES), BF16)
    n_sub = tq // sub

    def all_rows(blocks, lanes):
        parts = [blk[:, lanes] for blk in blocks]
        return parts[0] if len(parts) == 1 else jnp.concatenate(parts, axis=0)

    def scores(hp):
        lanes = slice(hp * LANES, (hp + 1) * LANES)
        q2 = []
        for c2 in range(n_sub):
            qp = q_ref[c2 * sub:(c2 + 1) * sub, lanes]
            q2 += [qp * keep_a, qp * keep_b]
        return _mm_nt(jnp.concatenate(q2, axis=0), all_rows(k_refs, lanes))

    def attend(hp, s_full, mask_missing_keys):
        lanes = slice(hp * LANES, (hp + 1) * LANES)
        ps = []
        for c2 in range(n_sub):
            s = s_full[c2 * 2 * sub:(c2 + 1) * 2 * sub, c2 * sub:c2 * sub + wsub] + bias_scr[hp]
            if mask_missing_keys:
                kpos = q_pos0 + i * tq + c2 * sub + (sub - wsub) + lax.broadcasted_iota(jnp.int32, (1, wsub), 1)
                s = s + jnp.where(kpos >= k_min, 0.0, NEG_BIG)
            p = jnp.exp2(s - jnp.max(s, axis=1, keepdims=True)).astype(BF16)
            pad = [jnp.zeros((2 * sub, c2 * sub), BF16)] if c2 else []
            pad_r = [jnp.zeros((2 * sub, w - wsub - c2 * sub), BF16)] if w - wsub - c2 * sub else []
            ps.append(jnp.concatenate(pad + [p] + pad_r, axis=1) if pad or pad_r else p)
        p_full = ps[0] if n_sub == 1 else jnp.concatenate(ps, axis=0)
        o = _mm(p_full, jnp.concatenate([all_rows(v_refs, lanes), ones], axis=1))
        for c2 in range(n_sub):
            ra, rb = c2 * 2 * sub, c2 * 2 * sub + sub
            oa = o[ra:ra + sub, :LANES] / o[ra:ra + sub, LANES:LANES + 1]
            ob = o[rb:rb + sub, :LANES] / o[rb:rb + sub, LANES:LANES + 1]
            o_ref[c2 * sub:(c2 + 1) * sub, lanes] = jnp.where(first_head, oa, ob).astype(o_ref.dtype)

    def heads(mask_missing_keys):
        n_pairs, ahead, pending = BAND_HEADS // 2, 2, {}
        for n in range(n_pairs + ahead):
            if n < n_pairs:
                pending[n] = scores(n)
            if n >= ahead:
                attend(n - ahead, pending.pop(n - ahead), mask_missing_keys)

    if clamped:
        first_full = -(-(w - tq) // tq)
        pl.when(i < first_full)(functools.partial(heads, True))
        pl.when(i >= first_full)(functools.partial(heads, False))
    else:
        heads(False)


def _band_bias_vec(rel_bias, sub, wsub):
    l = -(-(wsub + sub) // LANES) * LANES
    m = np.arange(l)
    d = np.where(m < wsub, m, m - l)
    rel = np.clip(wsub - sub - d, -REL_CLIP, REL_CLIP) + REL_CLIP
    return rel_bias[jnp.asarray(rel)].T


def _band(q, k, v, rel_bias, batch, t, tq, sub, k_block_rows, n_kblk, q_pos0, k_min, clamped):
    nq = t // tq
    w = n_kblk * k_block_rows
    wsub = w - tq + sub
    assert wsub % LANES == 0 and (wsub - sub) % CHUNK == 0
    qrow = lambda b, i: (b * nq + i, 0)

    def krow(off):
        if clamped:
            return lambda b, i: (b * nq + jnp.maximum(i - (n_kblk - 1) + off, 0), 0)
        return lambda b, i: (b * n_kblk + off, 0)

    vec = _band_bias_vec(rel_bias, sub, wsub)
    kspecs = [pl.BlockSpec((k_block_rows, D_MODEL), krow(o)) for o in range(n_kblk)]
    return pl.pallas_call(
        functools.partial(_band_body, tq=tq, sub=sub, n_kblk=n_kblk, q_pos0=q_pos0, k_min=k_min, clamped=clamped),
        grid=(batch, nq),
        in_specs=[pl.BlockSpec((tq, D_MODEL), qrow)] + kspecs + kspecs + [_resident(vec.shape)],
        out_specs=pl.BlockSpec((tq, D_MODEL), qrow),
        out_shape=jax.ShapeDtypeStruct((batch * t, D_MODEL), BF16),
        scratch_shapes=[pltpu.VMEM((BAND_HEADS // 2, 2 * sub, wsub), F32)],
        compiler_params=_params(2),
        name="band",
    )(q, *([k] * n_kblk), *([v] * n_kblk), vec)


def _merge_body(x_ref, ya_ref, yb_ref, yc_ref, g_ref, wg_ref, bg_ref, wbr_ref, wo_ref, o_ref):
    x = x_ref[...]
    u = _rms(x, g_ref[2:3, :]).astype(BF16)
    mix = jnp.zeros(x.shape, F32)
    for k, y_ref in enumerate((ya_ref, yb_ref, yc_ref)):
        sl = slice(k * D_MODEL, (k + 1) * D_MODEL)
        gate = _sigmoid(_mm(u, wg_ref[:, sl]) + bg_ref[:, sl])
        mix = mix + gate * _mm(y_ref[...], wbr_ref[k])
    o_ref[...] = x + _rms(_mm(mix.astype(BF16), wo_ref[...]), g_ref[3:4, :])


def _merge(x, ya, yb, yc, pw, layer):
    n = x.shape[0]
    tm = _row_tile(n, 512)
    row = lambda i: (i, 0)
    tile = pl.BlockSpec((tm, D_MODEL), row)
    return pl.pallas_call(
        _merge_body,
        grid=(n // tm,),
        in_specs=[tile, tile, tile, tile]
        + [_resident_slice(pw[k], (layer,)) for k in ("g", "wg", "bg", "wbr", "wo")],
        out_specs=tile,
        out_shape=jax.ShapeDtypeStruct((n, D_MODEL), F32),
        compiler_params=_params(1),
        name="merge",
    )(x, ya, yb, yc, pw["g"], pw["wg"], pw["bg"], pw["wbr"], pw["wo"])


def _ple_body(x_ref, p_ref, g_ref, wp_ref, wpg_ref, o_ref):
    x = x_ref[...]
    e = _mm(p_ref[...].astype(BF16), wp_ref[...])
    pg = _sigmoid(_mm(_rms(x, g_ref[6:7, :]).astype(BF16), wpg_ref[...]))
    o_ref[...] = x + _rms(pg * e, g_ref[7:8, :])


def _ple(x, p, pw, layer):
    n = x.shape[0]
    tm = _row_tile(n, 512)
    row = lambda i: (i, 0)
    return pl.pallas_call(
        _ple_body,
        grid=(n // tm,),
        in_specs=[pl.BlockSpec((tm, D_MODEL), row), pl.BlockSpec((tm, PLE_DIM), row)]
        + [_resident_slice(pw[k], (layer,)) for k in ("g", "wp", "wpg")],
        out_specs=pl.BlockSpec((tm, D_MODEL), row),
        out_shape=jax.ShapeDtypeStruct((n, D_MODEL), F32),
        compiler_params=_params(1),
        name="ple",
    )(x, p, pw["g"], pw["wp"], pw["wpg"])


BAND_TQ = 256
BAND_SUB = 128
BAND_KBLK = 1 + -(-BAND // BAND_TQ)


def _prep_weights(norm_g, ffn_w13, ffn_w2, w_in, w_gate, b_gate, w_branch, w_out, w_ple, w_ple_gate):
    bf = lambda a: a.astype(BF16)
    return {"g": norm_g, "w13": bf(ffn_w13), "w2": bf(ffn_w2), "w_in": _pack_w_in(w_in), "wg": bf(w_gate),
            "bg": b_gate.reshape(DEPTH, 1, N_BRANCH * D_MODEL), "wbr": bf(w_branch), "wo": bf(w_out),
            "wp": bf(w_ple), "wpg": bf(w_ple_gate)}


def _trunk_layer(x, p, w, pw, layer, cache, batch, t, shared):
    x = _ffn(x, pw, layer, 0, 0, 1)
    pr, shared = _inproj(x, pw, layer, batch, t, shared)
    if cache is None:
        ya, h_new = _mamba(pr["z"], pr["xbc"], pr["dtwi"], None, None, w, batch, t)
        yb = _dsa(pr["qb"], pr["qi"], pr["dtwi"], pr["ki"], pr["kb"], pr["vb"], layer, batch, t, t, t, 0)
        yc = _band(pr["qc"], pr["kc_bf"], pr["vc_bf"], w["rel_bias"], batch, t, BAND_TQ, BAND_SUB, BAND_TQ,
                   BAND_KBLK, 0, 0, True)
        conv_src = pr["xbc"].reshape(batch, t, CONV_CH)
    else:
        past = cache["dsa_k"].shape[1]
        conv0 = jnp.pad(cache["conv"], ((0, 0), (SUBLANES - (CONV_W - 1), 0), (0, 0)))
        h0 = cache["ssm"].reshape(batch, D_INNER, D_STATE)
        ya, h_new = _mamba(pr["z"], pr["xbc"], pr["dtwi"], conv0, h0, w, batch, t)

        n_valid = past + t
        s_keys = -(-n_valid // DSA_KB) * DSA_KB
        tq_pad = -(-t // DSA_TQ) * DSA_TQ

        def with_cache(c, new):
            width = new.shape[-1]
            a = jnp.concatenate([c.reshape(batch, past, width), new[layer].reshape(batch, t, width)], axis=1)
            return jnp.pad(a, ((0, 0), (0, s_keys - n_valid), (0, 0))).reshape(1, batch * s_keys, width)

        def pad_q(a):
            a = jnp.pad(a.reshape(batch, t, a.shape[-1]), ((0, 0), (0, tq_pad - t), (0, 0)))
            return a.reshape(batch * tq_pad, a.shape[-1])

        yb = _dsa(pad_q(pr["qb"]), pad_q(pr["qi"]), pad_q(pr["dtwi"]), with_cache(cache["idx_k"], pr["ki"]),
                  with_cache(cache["dsa_k"], pr["kb"]), with_cache(cache["dsa_v"], pr["vb"]),
                  0, batch, tq_pad, s_keys, n_valid, past)
        yb = yb.reshape(batch, tq_pad, D_MODEL)[:, :t].reshape(batch * t, D_MODEL)

        nrows = cache["band_k"].shape[1]
        k_rows = -(-(nrows + t) // LANES) * LANES
        lead = k_rows - nrows - t

        def with_band(c, new):
            a = jnp.concatenate([c.reshape(batch, nrows, D_MODEL).astype(BF16), new.reshape(batch, t, D_MODEL)],
                                axis=1)
            return jnp.pad(a, ((0, 0), (lead, 0), (0, 0))).reshape(batch * k_rows, D_MODEL)

        yc = _band(pr["qc"], with_band(cache["band_k"], pr["kc_bf"]), with_band(cache["band_v"], pr["vc_bf"]),
                   w["rel_bias"], batch, t, t, t, k_rows, 1, past, past - nrows, False)
        conv_src = jnp.concatenate([cache["conv"], pr["xbc"].reshape(batch, t, CONV_CH)], axis=1)

    x = _merge(x, ya, yb, yc, pw, layer)
    x = _ffn(x, pw, layer, 1, 4, 5)
    x = _ple(x, p.reshape(batch * t, PLE_DIM), pw, layer)

    state = (h_new.reshape(batch, SSM_HEADS, SSM_HEAD_DIM, D_STATE), conv_src[:, -(CONV_W - 1):])
    return x, state, shared


def _cache_outputs(shared, states, batch, t):
    band_rows = min(BAND, t)
    return (shared["kb"].reshape(DEPTH, batch, t, KV_HEADS, DSA_HEAD_DIM),
            shared["vb"].reshape(DEPTH, batch, t, KV_HEADS, DSA_HEAD_DIM),
            shared["ki"].reshape(DEPTH, batch, t, IDX_DIM),
            shared["kc"].reshape(DEPTH, batch, band_rows, BAND_HEADS, BAND_HEAD_DIM),
            shared["vc"].reshape(DEPTH, batch, band_rows, BAND_HEADS, BAND_HEAD_DIM),
            jnp.stack([s[0] for s in states]), jnp.stack([s[1] for s in states]))


def kernel(x_prompt, x_sample, p_prompt, p_sample, cache_dsa_k, cache_dsa_v, cache_idx_k, cache_band_k,
           cache_band_v, state_ssm, state_conv, norm_g, ffn_w13, ffn_w2, w_in, conv_w, conv_b, dt_bias,
           a_log, d_skip, ssm_norm_g, rel_bias, w_gate, b_gate, w_branch, w_out, w_ple, w_ple_gate):
    bp, tp, _ = x_prompt.shape
    bs, ts, _ = x_sample.shape
    yp = x_prompt.reshape(bp * tp, D_MODEL)
    ys = x_sample.reshape(bs * ts, D_MODEL)
    st_p, st_s, shared_p, shared_s = [], [], None, None
    pw = _prep_weights(norm_g, ffn_w13, ffn_w2, w_in, w_gate, b_gate, w_branch, w_out, w_ple, w_ple_gate)
    for i in range(DEPTH):
        w = {"conv_w": conv_w[i], "conv_b": conv_b[i], "dt_bias": dt_bias[i], "a_log": a_log[i],
             "d_skip": d_skip[i], "ssm_norm_g": ssm_norm_g[i], "rel_bias": rel_bias[i]}
        yp, sp, shared_p = _trunk_layer(yp, p_prompt[i], w, pw, i, None, bp, tp, shared_p)
        st_p.append(sp)
        cache = {"dsa_k": cache_dsa_k[i], "dsa_v": cache_dsa_v[i], "idx_k": cache_idx_k[i],
                 "band_k": cache_band_k[i], "band_v": cache_band_v[i], "ssm": state_ssm[i], "conv": state_conv[i]}
        ys, ss, shared_s = _trunk_layer(ys, p_sample[i], w, pw, i, cache, bs, ts, shared_s)
        st_s.append(ss)
    return (yp.reshape(bp, tp, D_MODEL), ys.reshape(bs, ts, D_MODEL),
            *_cache_outputs(shared_p, st_p, bp, tp), *_cache_outputs(shared_s, st_s, bs, ts))
```

```python
import functools
import math

import numpy as np
import jax
import jax.numpy as jnp
from jax import lax
from jax.experimental import pallas as pl
from jax.experimental.pallas import tpu as pltpu

F32 = jnp.float32
BF16 = jnp.bfloat16

D_MODEL = 1024
DEPTH = 2
CHUNK = 64
EPS = 1e-6
HALF = 0.5
D_FF = 2816
PLE_DIM = 256
SSM_HEAD_DIM = 64
D_INNER = D_MODEL
SSM_HEADS = D_INNER // SSM_HEAD_DIM
N_GROUPS = 4
HEADS_PER_GROUP = SSM_HEADS // N_GROUPS
D_STATE = 128
CONV_W = 4
CONV_CH = D_INNER + 2 * N_GROUPS * D_STATE
DSA_HEAD_DIM = 128
DSA_HEADS = D_MODEL // DSA_HEAD_DIM
KV_HEADS = 2
KV_REP = DSA_HEADS // KV_HEADS
IDX_HEADS = 4
IDX_DIM = 64
TOPK_MAX = 256
BAND_HEAD_DIM = 64
BAND_HEADS = D_MODEL // BAND_HEAD_DIM
LEFT_CHUNKS = 8
BAND = LEFT_CHUNKS * CHUNK
REL_CLIP = 256
N_BRANCH = 3
IN_WIDTHS = (D_INNER, CONV_CH, SSM_HEADS,
             DSA_HEADS * DSA_HEAD_DIM, KV_HEADS * DSA_HEAD_DIM, KV_HEADS * DSA_HEAD_DIM,
             IDX_HEADS * IDX_DIM, IDX_DIM, IDX_HEADS,
             BAND_HEADS * BAND_HEAD_DIM, BAND_HEADS * BAND_HEAD_DIM, BAND_HEADS * BAND_HEAD_DIM)
IN_SPLITS = tuple(int(s) for s in np.cumsum(IN_WIDTHS)[:-1])

LANES = 128
SUBLANES = 8
KV_DIM = KV_HEADS * DSA_HEAD_DIM
GROUP_CH = D_INNER // N_GROUPS
NEG_BIG = -1e30
INT_MIN = -2 ** 31
LOG2E = math.log2(math.e)
VMEM_LIMIT = 56 * 1024 * 1024


def _mm(a, b):
    return jnp.dot(a, b, preferred_element_type=F32)


def _mm_nt(a, b):
    return lax.dot_general(a, b, (((1,), (1,)), ((), ())), preferred_element_type=F32)


def _mm_tn(a, b):
    return lax.dot_general(a, b, (((0,), (0,)), ((), ())), preferred_element_type=F32)


def _rms(x, g):
    return x * lax.rsqrt(jnp.mean(x * x, axis=-1, keepdims=True) + EPS) * g


def _sigmoid(x):
    return 1.0 / (1.0 + jnp.exp(-x))


def _silu(x):
    return x * _sigmoid(x)


def _resident(shape):
    return pl.BlockSpec(shape, lambda *_: (0,) * len(shape), pipeline_mode=pl.Buffered(1))


def _resident_slice(arr, lead, block=None, at=None):
    tail = tuple(arr.shape[len(lead):]) if block is None else tuple(block)
    idx = tuple(lead) + ((0,) * len(tail) if at is None else tuple(at))
    return pl.BlockSpec((None,) * len(lead) + tail, lambda *_: idx, pipeline_mode=pl.Buffered(1))


def _params(n_grid_dims):
    return pltpu.CompilerParams(dimension_semantics=("arbitrary",) * n_grid_dims,
                                vmem_limit_bytes=VMEM_LIMIT)


def _row_tile(n_rows, want):
    t = min(want, n_rows)
    assert n_rows % t == 0
    return t


def _fold_rows(x, op):
    parts = [x[i * SUBLANES:(i + 1) * SUBLANES] for i in range(x.shape[0] // SUBLANES)]
    while len(parts) > 1:
        parts = [op(parts[i], parts[i + 1]) for i in range(0, len(parts) - 1, 2)] + parts[len(parts) & ~1:]
    return parts[0]


FF_CHUNK = 256


def _ffn_body(x_ref, g_ref, wa_ref, wb_ref, w2_ref, o_ref, *, g_pre, g_post):
    x = x_ref[...]
    u = _rms(x, g_ref[g_pre:g_pre + 1, :]).astype(BF16)
    acc = jnp.zeros(x.shape, F32)
    for c in range(D_FF // FF_CHUNK):
        sl = slice(c * FF_CHUNK, (c + 1) * FF_CHUNK)
        a = _mm(u, wa_ref[:, sl])
        b = _mm(u, wb_ref[:, sl])
        acc = acc + _mm((_silu(a) * b).astype(BF16), w2_ref[sl, :])
    o_ref[...] = x + HALF * _rms(acc, g_ref[g_post:g_post + 1, :])


def _ffn(x, pw, layer, j, g_pre, g_post):
    n = x.shape[0]
    tm = _row_tile(n, 1024)
    row = lambda i: (i, 0)
    half = (D_MODEL, D_FF)
    return pl.pallas_call(
        functools.partial(_ffn_body, g_pre=g_pre, g_post=g_post),
        grid=(n // tm,),
        in_specs=[pl.BlockSpec((tm, D_MODEL), row), _resident_slice(pw["g"], (layer,)),
                  _resident_slice(pw["w13"], (layer, j), half, (0, 0)),
                  _resident_slice(pw["w13"], (layer, j), half, (0, 1)),
                  _resident_slice(pw["w2"], (layer, j))],
        out_specs=pl.BlockSpec((tm, D_MODEL), row),
        out_shape=jax.ShapeDtypeStruct((n, D_MODEL), F32),
        compiler_params=_params(1),
        name="ffn",
    )(x, pw["g"], pw["w13"], pw["w13"], pw["w2"])


_INPROJ_GROUPS = (D_INNER, CONV_CH, D_MODEL, KV_DIM, KV_DIM, IDX_HEADS * IDX_DIM, D_MODEL, D_MODEL, D_MODEL,
                  LANES, LANES)
_INPROJ_OUT = (
    ("z", 0, D_INNER, F32, None, "layer"),
    ("xbc", 1, CONV_CH, F32, None, "layer"),
    ("qb", 2, D_MODEL, BF16, DSA_HEAD_DIM ** -0.5 * LOG2E, "layer"),
    ("kb", 3, KV_DIM, F32, None, "stack"),
    ("vb", 4, KV_DIM, F32, None, "stack"),
    ("qi", 5, IDX_HEADS * IDX_DIM, F32, None, "layer"),
    ("qc", 6, D_MODEL, BF16, BAND_HEAD_DIM ** -0.5 * LOG2E, "layer"),
    ("kc", 7, D_MODEL, F32, None, "tail"),
    ("kc_bf", 7, D_MODEL, BF16, None, "layer"),
    ("vc", 8, D_MODEL, F32, None, "tail"),
    ("vc_bf", 8, D_MODEL, BF16, None, "layer"),
    ("ki", 9, IDX_DIM, F32, None, "stack"),
    ("dtwi", 10, LANES, F32, None, "layer"),
)
WI_LANE = SSM_HEADS


def _pack_w_in(w_in):
    z, xbc, dt, qb, kb, vb, qi, ki, wi, qc, kc, vc = jnp.split(w_in, IN_SPLITS, axis=-1)
    pad = lambda w: jnp.pad(w, ((0, 0), (0, 0), (0, LANES - w.shape[-1])))
    cols = [z, xbc, qb, kb, vb, qi, qc, kc, vc, pad(ki), pad(jnp.concatenate([dt, wi], axis=-1))]
    return jnp.concatenate(cols, axis=-1).astype(BF16)


def _inproj_body(x_ref, g_ref, w_ref, *refs, transpose_tails):
    out_refs = refs[len(refs) - len(_INPROJ_OUT):]
    u = _rms(x_ref[...], g_ref[2:3, :]).astype(BF16)
    starts = np.concatenate([[0], np.cumsum(_INPROJ_GROUPS)])
    for grp, width in enumerate(_INPROJ_GROUPS):
        r = _mm(u, w_ref[:, int(starts[grp]):int(starts[grp]) + width])
        for (_, og, stored, dtype, scale, kind), o_ref in zip(_INPROJ_OUT, out_refs):
            if og == grp:
                v = r if scale is None else r * scale
                v = (v if stored == width else v[:, :stored]).astype(dtype)
                o_ref[...] = v.T if (kind == "tail" and transpose_tails) else v


def _inproj(x, pw, layer, batch, t, shared):
    n = x.shape[0]
    tm = _row_tile(n, 256)
    tail = min(BAND, t)
    row = lambda i: (i, 0)
    stack_row = lambda i: (layer, i, 0)
    transpose_tails = tail != t
    if transpose_tails:
        assert t % tm == 0 and tail % tm == 0
        per_seq, per_tail = t // tm, tail // tm
        tail_blk = lambda i: (layer, i // per_seq, 0, jnp.maximum(i % per_seq - (per_seq - per_tail), 0))
    out_specs, out_shape, stacked = [], [], []
    for k, (name, _, width, dtype, _, kind) in enumerate(_INPROJ_OUT):
        if kind == "layer":
            out_specs.append(pl.BlockSpec((tm, width), row))
            out_shape.append(jax.ShapeDtypeStruct((n, width), dtype))
        elif kind == "tail" and transpose_tails:
            out_specs.append(pl.BlockSpec((None, None, width, tm), tail_blk))
            out_shape.append(jax.ShapeDtypeStruct((DEPTH, batch, width, tail), dtype))
            stacked.append((name, k))
        else:
            out_specs.append(pl.BlockSpec((None, tm, width), stack_row))
            out_shape.append(jax.ShapeDtypeStruct((DEPTH, n, width), dtype))
            stacked.append((name, k))
    carried = [] if shared is None else [shared[name] for name, _ in stacked]
    aliases = {} if shared is None else {3 + a: k for a, (_, k) in enumerate(stacked)}
    outs = pl.pallas_call(
        functools.partial(_inproj_body, transpose_tails=transpose_tails),
        grid=(n // tm,),
        in_specs=[pl.BlockSpec((tm, D_MODEL), row), _resident_slice(pw["g"], (layer,)),
                  _resident_slice(pw["w_in"], (layer,))] + [pl.BlockSpec(memory_space=pl.ANY)] * len(carried),
        out_specs=out_specs,
        out_shape=out_shape,
        input_output_aliases=aliases,
        compiler_params=_params(1),
        name="inproj",
    )(x, pw["g"], pw["w_in"], *carried)
    pr = {o[0]: a for o, a in zip(_INPROJ_OUT, outs)}
    return pr, {name: pr[name] for name, _ in stacked}


CONV_PAD = 16
CONV_ROWS = CONV_PAD + CHUNK
MAMBA_PAIR = 4


def _split3(x):
    hi = x.astype(BF16)
    r = x - hi.astype(F32)
    mid = r.astype(BF16)
    lo = (r - mid.astype(F32)).astype(BF16)
    return hi, mid, lo


def _expand_heads(x, e):
    hi, mid, lo = _split3(x)
    return _mm(hi, e) + _mm(mid, e) + _mm(lo, e)


def _cumsum_rows(x):
    n = x.shape[0]
    row = lax.broadcasted_iota(jnp.int32, x.shape, 0)
    d = 1
    while d < n:
        x = x + jnp.where(row >= d, pltpu.roll(x, d, 0), 0.0)
        d *= 2
    return x


def _conv_shift_matrix():
    s = np.zeros(((CONV_W - 1) * CHUNK, 3 * CONV_ROWS), np.float32)
    for k in range(CONV_W - 1):
        for l in range(CHUNK):
            for part in range(3):
                s[k * CHUNK + l, part * CONV_ROWS + CONV_PAD - (CONV_W - 1) + k + l] = 1.0
    return jnp.asarray(s, BF16)


def _mamba_body(*refs, has_state, n_chunks):
    if has_state:
        (z_ref, xbc_ref, dtwi_ref, conv0_ref, h0_ref, cw_ref, cb_ref, dtb_ref, alog_ref, dskip_ref,
         ng_ref, e_ref, shift_ref, y_ref, hout_ref, tail_scr, ht_scr) = refs
    else:
        (z_ref, xbc_ref, dtwi_ref, cw_ref, cb_ref, dtb_ref, alog_ref, dskip_ref,
         ng_ref, e_ref, shift_ref, y_ref, hout_ref, tail_scr, ht_scr) = refs
    c = pl.program_id(1)
    q = CHUNK
    tail_rows = slice(CONV_PAD - SUBLANES, CONV_PAD)

    @pl.when(c == 0)
    def _():
        tail_scr[...] = jnp.zeros(tail_scr.shape, F32)
        if has_state:
            for b in range(MAMBA_PAIR):
                tail_scr[b, tail_rows, :] = conv0_ref[b]
                ht_scr[b] = h0_ref[b].T
        else:
            ht_scr[...] = jnp.zeros(ht_scr.shape, F32)

    lane = lax.broadcasted_iota(jnp.int32, (q, LANES), 1)
    li = lax.broadcasted_iota(jnp.int32, (q, D_INNER), 0)
    si = lax.broadcasted_iota(jnp.int32, (q, D_INNER), 1) & (q - 1)
    diag = li == si
    causal = (li >= si)[:, :GROUP_CH]
    bdr = lax.broadcasted_iota(jnp.int32, (GROUP_CH, GROUP_CH), 0) // SSM_HEAD_DIM
    bdc = lax.broadcasted_iota(jnp.int32, (GROUP_CH, GROUP_CH), 1) // SSM_HEAD_DIM
    block_diag = bdr == bdc
    neg_a = -jnp.exp(alog_ref[...])

    def sequence(b):
        x = xbc_ref[b]
        hi, mid, lo = _split3(jnp.concatenate([tail_scr[b], x], axis=0))
        delayed = _mm(shift_ref[...], jnp.concatenate([hi, mid, lo], axis=0))
        tail_scr[b, tail_rows, :] = x[q - SUBLANES:q, :]
        pre = dtwi_ref[b] + dtb_ref[...]
        dt = jnp.maximum(pre, 0.0) + jnp.log1p(jnp.exp(-jnp.abs(pre)))
        dt = jnp.where(lane < SSM_HEADS, dt, 0.0)
        cum = _cumsum_rows(dt * neg_a)
        e = e_ref[...]
        ecol = _expand_heads(cum, e)
        dtx = _expand_heads(dt, e)
        yield

        acc = cb_ref[...] + x * cw_ref[CONV_W - 1:CONV_W, :]
        for k in range(CONV_W - 1):
            acc = acc + delayed[k * q:(k + 1) * q, :] * cw_ref[k:k + 1, :]
        xc = _silu(acc)
        xs = xc[:, :D_INNER]
        bm = xc[:, D_INNER:D_INNER + N_GROUPS * D_STATE].astype(BF16)
        cm = xc[:, D_INNER + N_GROUPS * D_STATE:].astype(BF16)
        cbs = [_mm_nt(cm[:, g * D_STATE:(g + 1) * D_STATE], bm[:, g * D_STATE:(g + 1) * D_STATE])
               for g in range(N_GROUPS)]
        yield

        erow = jnp.sum(jnp.where(diag, ecol, 0.0), axis=0, keepdims=True)
        elast = ecol[q - 1:q, :]
        xdt = xs * dtx
        xdec = (xdt * jnp.exp(elast - ecol)).astype(BF16)
        exp_e = jnp.exp(ecol)
        chunk_decay = jnp.exp(elast)
        yield

        ys = []
        for g in range(N_GROUPS):
            sl = slice(g * GROUP_CH, (g + 1) * GROUP_CH)
            nl = slice(g * D_STATE, (g + 1) * D_STATE)
            cbt = jnp.concatenate([cbs[g]] * HEADS_PER_GROUP, axis=1)
            decay = jnp.exp(jnp.where(causal, ecol[:, sl] - erow[:, sl], NEG_BIG))
            m = (cbt * decay).astype(BF16)
            xg = xdt[:, sl]
            bd = jnp.where(block_diag, jnp.concatenate([xg] * HEADS_PER_GROUP, axis=0), 0.0).astype(BF16)
            y_diag = _mm(m, bd)
            ht_g = ht_scr[b, :, sl]
            y_off = _mm(cm[:, nl], ht_g.astype(BF16)) * exp_e[:, sl]
            ys.append(y_diag + y_off)
            ht_scr[b, :, sl] = ht_g * chunk_decay[:, sl] + _mm_tn(bm[:, nl], xdec[:, sl])
            yield

        y = jnp.concatenate(ys, axis=1) + dskip_ref[...] * xs
        y = y * _silu(z_ref[b])
        outs = []
        for g in range(N_GROUPS):
            yg = y[:, g * GROUP_CH:(g + 1) * GROUP_CH]
            outs.append(yg * lax.rsqrt(jnp.mean(yg * yg, axis=-1, keepdims=True) + EPS))
        y_ref[b] = (jnp.concatenate(outs, axis=1) * ng_ref[...]).astype(y_ref.dtype)
        yield

    for _ in zip(*[sequence(b) for b in range(MAMBA_PAIR)]):
        pass

    @pl.when(c == n_chunks - 1)
    def _():
        for b in range(MAMBA_PAIR):
            hout_ref[b] = ht_scr[b].T


def _head_expand_matrix():
    e = np.zeros((LANES, D_INNER), np.float32)
    for h in range(SSM_HEADS):
        e[h, h * SSM_HEAD_DIM:(h + 1) * SSM_HEAD_DIM] = 1.0
    return jnp.asarray(e, BF16)


def _mamba(z, xbc, dtwi, conv0, h0, lw, batch, t):
    nc = t // CHUNK
    assert batch % MAMBA_PAIR == 0
    has_state = h0 is not None
    chunk = lambda b, c: (b, c, 0)
    per_b = lambda b, c: (b, 0, 0)
    pad16 = lambda v: jnp.pad(v.reshape(1, SSM_HEADS), ((0, 0), (0, LANES - SSM_HEADS)))
    small = [lw["conv_w"], lw["conv_b"].reshape(1, CONV_CH), pad16(lw["dt_bias"]), pad16(lw["a_log"]),
             jnp.repeat(lw["d_skip"], SSM_HEAD_DIM).reshape(1, D_INNER),
             lw["ssm_norm_g"].reshape(1, D_INNER), _head_expand_matrix(), _conv_shift_matrix()]
    ins = [a.reshape(batch, t, a.shape[-1]) for a in (z, xbc, dtwi)]
    in_specs = [pl.BlockSpec((MAMBA_PAIR, CHUNK, a.shape[-1]), chunk) for a in ins]
    if has_state:
        ins += [conv0, h0]
        in_specs += [pl.BlockSpec((MAMBA_PAIR, SUBLANES, CONV_CH), per_b),
                     pl.BlockSpec((MAMBA_PAIR, D_INNER, D_STATE), per_b)]
    ins += small
    in_specs += [_resident(a.shape) for a in small]
    y, h_out = pl.pallas_call(
        functools.partial(_mamba_body, has_state=has_state, n_chunks=nc),
        grid=(batch // MAMBA_PAIR, nc),
        in_specs=in_specs,
        out_specs=[pl.BlockSpec((MAMBA_PAIR, CHUNK, D_INNER), chunk),
                   pl.BlockSpec((MAMBA_PAIR, D_INNER, D_STATE), per_b)],
        out_shape=[jax.ShapeDtypeStruct((batch, t, D_INNER), BF16),
                   jax.ShapeDtypeStruct((batch, D_INNER, D_STATE), F32)],
        scratch_shapes=[pltpu.VMEM((MAMBA_PAIR, CONV_PAD, CONV_CH), F32),
                        pltpu.VMEM((MAMBA_PAIR, D_STATE, D_INNER), F32)],
        compiler_params=_params(2),
        name="mamba",
    )(*ins)
    return y.reshape(batch * t, D_INNER), h_out


DSA_TQ = LANES
DSA_KB = 256
DSA_PAIR = 2
IDX_CAT = 4 * IDX_DIM
DSA_RANK_ROWS = 128


def _hi_lo(x):
    hi = x.astype(BF16).astype(F32)
    return hi, (x - hi).astype(BF16).astype(F32)


def _dsa_body(qb_ref, qi_ref, dtwi_ref, ki_ref, k_ref, v_ref, o_ref,
              kcat_scr, kbf_scr, vt_scr, tri_scr, key_scr, mask_scr, s_scr,
              *, n_kb, n_valid, q_pos0, topk, visible):
    tq, kb_rows = DSA_TQ, DSA_KB
    j = pl.program_id(1)

    @pl.when(j == 0)
    def _():
        hi, lo = _hi_lo(ki_ref[...])
        kcat_scr[...] = jnp.concatenate([hi, lo, hi, jnp.zeros_like(hi)], axis=1).astype(BF16)
        kbf_scr[...] = k_ref[...].astype(BF16)
        for kb in range(n_kb):
            vt = v_ref[kb * kb_rows:(kb + 1) * kb_rows, :].T
            for g in range(KV_HEADS):
                vt_scr[g, kb] = vt[g * DSA_HEAD_DIM:(g + 1) * DSA_HEAD_DIM, :].astype(BF16)
        tr = lax.broadcasted_iota(jnp.int32, (DSA_RANK_ROWS, DSA_RANK_ROWS), 0)
        tc = lax.broadcasted_iota(jnp.int32, (DSA_RANK_ROWS, DSA_RANK_ROWS), 1)
        tri_scr[...] = jnp.where(tc < tr, 1.0, 0.0).astype(BF16)

    n_q = qb_ref.shape[0] // tq
    last_end = jnp.minimum((((q_pos0 + (j + 1) * n_q * tq - 1) >> 6) + 1) << 6, n_valid)
    nkb = (last_end + (kb_rows - 1)) // kb_rows
    krow = lax.broadcasted_iota(jnp.int32, (kb_rows, 1), 0)

    def prepare(qn):
        rows = slice(qn * tq, (qn + 1) * tq)
        qpos = q_pos0 + (j * n_q + qn) * tq + lax.broadcasted_iota(jnp.int32, (1, tq), 1)
        q_end = jnp.minimum(((qpos >> 6) + 1) << 6, n_valid)
        wit = (dtwi_ref[rows, :] * (IDX_DIM ** -0.5 * IDX_HEADS ** -0.5)).T
        qi = qi_ref[rows, :]
        qparts = []
        for h in range(IDX_HEADS):
            hi, lo = _hi_lo(qi[:, h * IDX_DIM:(h + 1) * IDX_DIM])
            qparts.append(jnp.concatenate([hi, hi, lo, jnp.zeros_like(hi)], axis=1))
        qb = qb_ref[rows, :]
        q4 = [jnp.concatenate([qb[:, (g * KV_REP + r) * DSA_HEAD_DIM:(g * KV_REP + r + 1) * DSA_HEAD_DIM]
                               for r in range(KV_REP)], axis=0) for g in range(KV_HEADS)]
        return dict(rows=rows, q_end=q_end, k_eff=jnp.minimum(q_end, topk).astype(F32), wit=wit,
                    qcat=jnp.concatenate(qparts, axis=0).astype(BF16), q4=q4)

    qs = [prepare(qn) for qn in range(n_q)]

    def run(n_vis):
        for qn, q in enumerate(qs):
            for kb in range(n_vis):
                logit = _mm_nt(kcat_scr[kb * kb_rows:(kb + 1) * kb_rows, :], q["qcat"])
                sc = jnp.zeros((kb_rows, tq), F32)
                for h in range(IDX_HEADS):
                    sc = sc + (jnp.maximum(logit[:, h * tq:(h + 1) * tq], 0.0)
                               * q["wit"][WI_LANE + h:WI_LANE + h + 1, :])
                sc = jnp.where(kb * kb_rows + krow < q["q_end"], sc, -jnp.inf)
                bits = lax.bitcast_convert_type(sc, jnp.int32)
                key_scr[qn, kb] = jnp.where(bits < 0, bits ^ jnp.int32(0x7FFFFFFF), bits)

        def count(qn, pred):
            acc = _fold_rows(jnp.where(pred(key_scr[qn, 0]), 1.0, 0.0), jnp.add)
            for kb in range(1, n_vis):
                acc = acc + _fold_rows(jnp.where(pred(key_scr[qn, kb]), 1.0, 0.0), jnp.add)
            return jnp.sum(acc, axis=0, keepdims=True)

        def radix_step(i, tus):
            bit = lax.shift_left(jnp.int32(1), 31 - i)
            new = []
            for qn, q in enumerate(qs):
                cand = tus[qn] | bit
                thr_c = cand ^ jnp.int32(INT_MIN)
                new.append(jnp.where(count(qn, lambda k: k >= thr_c) >= q["k_eff"], cand, tus[qn]))
            return tuple(new)

        tus = lax.fori_loop(0, 32, radix_step, tuple(jnp.zeros((1, tq), jnp.int32) for _ in qs))

        def finish(qn, q, thr):
            need = q["k_eff"] - count(qn, lambda k: k > thr)
            before = jnp.zeros((1, tq), F32)
            for kb in range(n_vis):
                for r0 in range(0, kb_rows, DSA_RANK_ROWS):
                    key = key_scr[qn, kb, r0:r0 + DSA_RANK_ROWS, :]
                    eq = jnp.where(key == thr, 1.0, 0.0)
                    rank = _mm(tri_scr[...], eq.astype(BF16)) + before
                    take = jnp.where(key > thr, 1.0, jnp.where(rank < need, eq, 0.0))
                    mask_scr[qn, kb, r0:r0 + DSA_RANK_ROWS, :] = jnp.where(take > 0.0, 0.0, NEG_BIG)
                    before = before + jnp.sum(_fold_rows(eq, jnp.add), axis=0, keepdims=True)
            yield

            m8 = [jnp.full((SUBLANES, KV_REP * tq), NEG_BIG, F32) for _ in range(KV_HEADS)]
            for kb in range(n_vis):
                mask4 = jnp.concatenate([mask_scr[qn, kb]] * KV_REP, axis=1)
                for g in range(KV_HEADS):
                    kg = kbf_scr[kb * kb_rows:(kb + 1) * kb_rows, g * DSA_HEAD_DIM:(g + 1) * DSA_HEAD_DIM]
                    st = _mm_nt(kg, q["q4"][g]) + mask4
                    s_scr[qn, kb, g] = st
                    m8[g] = jnp.maximum(m8[g], _fold_rows(st, jnp.maximum))
            m = [jnp.max(m8[g], axis=0, keepdims=True) for g in range(KV_HEADS)]
            yield

            acc, den = [None] * KV_HEADS, [None] * KV_HEADS
            for kb in range(n_vis):
                for g in range(KV_HEADS):
                    p = jnp.exp2(s_scr[qn, kb, g] - m[g])
                    pv = _mm(vt_scr[g, kb], p.astype(BF16))
                    psum = _fold_rows(p, jnp.add)
                    acc[g] = pv if acc[g] is None else acc[g] + pv
                    den[g] = psum if den[g] is None else den[g] + psum
            outs = []
            for g in range(KV_HEADS):
                ot = acc[g] / jnp.sum(den[g], axis=0, keepdims=True)
                outs += [ot[:, r * tq:(r + 1) * tq].T for r in range(KV_REP)]
            o_ref[q["rows"], :] = jnp.concatenate(outs, axis=1).astype(o_ref.dtype)
            yield

        for _ in zip(*[finish(qn, q, tus[qn] ^ jnp.int32(INT_MIN)) for qn, q in enumerate(qs)]):
            pass

    if len(visible) == 1:
        run(visible[0])
    else:
        for n_vis in visible:
            pl.when(nkb == n_vis)(functools.partial(run, n_vis))


def _dsa(qb, qi, dtwi, ki_all, k_all, v_all, key_layer, batch, t, s_keys, n_valid, q_pos0):
    tq, kb_rows = DSA_TQ, DSA_KB
    assert t % tq == 0 and s_keys % kb_rows == 0
    n_q = DSA_PAIR if (t // tq) % DSA_PAIR == 0 else 1
    nq = t // (n_q * tq)
    n_kb = s_keys // kb_rows
    topk = min(TOPK_MAX, n_valid // 4)
    qrow = lambda b, j: (b * nq + j, 0)
    krow = lambda b, j: (key_layer, b, 0)

    def visible_blocks(j):
        last_end = min(((((q_pos0 + (j + 1) * n_q * tq - 1) >> 6) + 1) << 6), n_valid)
        return (last_end + kb_rows - 1) // kb_rows

    visible = tuple(sorted({visible_blocks(j) for j in range(nq)}))
    return pl.pallas_call(
        functools.partial(_dsa_body, n_kb=n_kb, n_valid=n_valid, q_pos0=q_pos0, topk=topk, visible=visible),
        grid=(batch, nq),
        in_specs=[pl.BlockSpec((n_q * tq, D_MODEL), qrow), pl.BlockSpec((n_q * tq, IDX_HEADS * IDX_DIM), qrow),
                  pl.BlockSpec((n_q * tq, LANES), qrow), pl.BlockSpec((None, s_keys, IDX_DIM), krow),
                  pl.BlockSpec((None, s_keys, KV_DIM), krow), pl.BlockSpec((None, s_keys, KV_DIM), krow)],
        out_specs=pl.BlockSpec((n_q * tq, D_MODEL), qrow),
        out_shape=jax.ShapeDtypeStruct((batch * t, D_MODEL), BF16),
        scratch_shapes=[pltpu.VMEM((s_keys, IDX_CAT), BF16), pltpu.VMEM((s_keys, KV_DIM), BF16),
                        pltpu.VMEM((KV_HEADS, n_kb, DSA_HEAD_DIM, kb_rows), BF16),
                        pltpu.VMEM((DSA_RANK_ROWS, DSA_RANK_ROWS), BF16),
                        pltpu.VMEM((n_q, n_kb, kb_rows, tq), jnp.int32), pltpu.VMEM((n_q, n_kb, kb_rows, tq), F32),
                        pltpu.VMEM((n_q, n_kb, KV_HEADS, kb_rows, KV_REP * tq), F32)],
        compiler_params=_params(2),
        name="dsa",
    )(qb, qi, dtwi, ki_all, k_all, v_all)


def _band_body(*refs, tq, sub, n_kblk, q_pos0, k_min, clamped):
    q_ref = refs[0]
    k_refs = refs[1:1 + n_kblk]
    v_refs = refs[1 + n_kblk:1 + 2 * n_kblk]
    vec_ref, o_ref, bias_scr = refs[1 + 2 * n_kblk:]
    w = sum(r.shape[0] for r in k_refs)
    wsub = w - tq + sub
    i = pl.program_id(1)

    @pl.when((pl.program_id(0) == 0) & (i == 0))
    def _():
        r = lax.broadcasted_iota(jnp.int32, (sub, wsub), 0)
        c = lax.broadcasted_iota(jnp.int32, (sub, wsub), 1)
        dchunk = (r >> 6) + (wsub - sub) // CHUNK - (c >> 6)
        band_mask = jnp.where((dchunk >= 0) & (dchunk <= LEFT_CHUNKS), 0.0, NEG_BIG)
        for h in range(BAND_HEADS):
            rows = jnp.broadcast_to(vec_ref[h:h + 1, :], (sub, vec_ref.shape[1]))
            toeplitz = pltpu.roll(rows, 0, 1, stride=1, stride_axis=0)[:, :wsub]
            bias_scr[h // 2, (h % 2) * sub:(h % 2 + 1) * sub, :] = toeplitz * LOG2E + band_mask

    lane = lax.broadcasted_iota(jnp.int32, (sub, LANES), 1)
    first_head = lane < BAND_HEAD_DIM
    keep_a = jnp.where(first_head, 1.0, 0.0).astype(BF16)
    keep_b = jnp.where(first_head, 0.0, 1.0).astype(BF16)
    ones = jnp.ones((w, LANES), BF16)
    n_sub = tq // sub

    def all_rows(blocks, lanes):
        parts = [blk[:, lanes] for blk in blocks]
        return parts[0] if len(parts) == 1 else jnp.concatenate(parts, axis=0)

    def scores(hp):
        lanes = slice(hp * LANES, (hp + 1) * LANES)
        q2 = []
        for c2 in range(n_sub):
            qp = q_ref[c2 * sub:(c2 + 1) * sub, lanes]
            q2 += [qp * keep_a, qp * keep_b]
        return _mm_nt(jnp.concatenate(q2, axis=0), all_rows(k_refs, lanes))

    def attend(hp, s_full, mask_missing_keys):
        lanes = slice(hp * LANES, (hp + 1) * LANES)
        ps = []
        for c2 in range(n_sub):
            s = s_full[c2 * 2 * sub:(c2 + 1) * 2 * sub, c2 * sub:c2 * sub + wsub] + bias_scr[hp]
            if mask_missing_keys:
                kpos = q_pos0 + i * tq + c2 * sub + (sub - wsub) + lax.broadcasted_iota(jnp.int32, (1, wsub), 1)
                s = s + jnp.where(kpos >= k_min, 0.0, NEG_BIG)
            p = jnp.exp2(s - jnp.max(s, axis=1, keepdims=True)).astype(BF16)
            pad = [jnp.zeros((2 * sub, c2 * sub), BF16)] if c2 else []
            pad_r = [jnp.zeros((2 * sub, w - wsub - c2 * sub), BF16)] if w - wsub - c2 * sub else []
            ps.append(jnp.concatenate(pad + [p] + pad_r, axis=1) if pad or pad_r else p)
        p_full = ps[0] if n_sub == 1 else jnp.concatenate(ps, axis=0)
        o = _mm(p_full, jnp.concatenate([all_rows(v_refs, lanes), ones], axis=1))
        for c2 in range(n_sub):
            ra, rb = c2 * 2 * sub, c2 * 2 * sub + sub
            oa = o[ra:ra + sub, :LANES] / o[ra:ra + sub, LANES:LANES + 1]
            ob = o[rb:rb + sub, :LANES] / o[rb:rb + sub, LANES:LANES + 1]
            o_ref[c2 * sub:(c2 + 1) * sub, lanes] = jnp.where(first_head, oa, ob).astype(o_ref.dtype)

    def heads(mask_missing_keys):
        n_pairs, ahead, pending = BAND_HEADS // 2, 2, {}
        for n in range(n_pairs + ahead):
            if n < n_pairs:
                pending[n] = scores(n)
            if n >= ahead:
                attend(n - ahead, pending.pop(n - ahead), mask_missing_keys)

    if clamped:
        first_full = -(-(w - tq) // tq)
        pl.when(i < first_full)(functools.partial(heads, True))
        pl.when(i >= first_full)(functools.partial(heads, False))
    else:
        heads(False)


def _band_bias_vec(rel_bias, sub, wsub):
    l = -(-(wsub + sub) // LANES) * LANES
    m = np.arange(l)
    d = np.where(m < wsub, m, m - l)
    rel = np.clip(wsub - sub - d, -REL_CLIP, REL_CLIP) + REL_CLIP
    return rel_bias[jnp.asarray(rel)].T


def _band(q, k, v, rel_bias, batch, t, tq, sub, k_block_rows, n_kblk, q_pos0, k_min, clamped):
    nq = t // tq
    w = n_kblk * k_block_rows
    wsub = w - tq + sub
    assert wsub % LANES == 0 and (wsub - sub) % CHUNK == 0
    qrow = lambda b, i: (b * nq + i, 0)

    def krow(off):
        if clamped:
            return lambda b, i: (b * nq + jnp.maximum(i - (n_kblk - 1) + off, 0), 0)
        return lambda b, i: (b * n_kblk + off, 0)

    vec = _band_bias_vec(rel_bias, sub, wsub)
    kspecs = [pl.BlockSpec((k_block_rows, D_MODEL), krow(o)) for o in range(n_kblk)]
    return pl.pallas_call(
        functools.partial(_band_body, tq=tq, sub=sub, n_kblk=n_kblk, q_pos0=q_pos0, k_min=k_min, clamped=clamped),
        grid=(batch, nq),
        in_specs=[pl.BlockSpec((tq, D_MODEL), qrow)] + kspecs + kspecs + [_resident(vec.shape)],
        out_specs=pl.BlockSpec((tq, D_MODEL), qrow),
        out_shape=jax.ShapeDtypeStruct((batch * t, D_MODEL), BF16),
        scratch_shapes=[pltpu.VMEM((BAND_HEADS // 2, 2 * sub, wsub), F32)],
        compiler_params=_params(2),
        name="band",
    )(q, *([k] * n_kblk), *([v] * n_kblk), vec)


def _merge_body(x_ref, ya_ref, yb_ref, yc_ref, g_ref, wg_ref, bg_ref, wbr_ref, wo_ref, o_ref):
    x = x_ref[...]
    u = _rms(x, g_ref[2:3, :]).astype(BF16)
    mix = jnp.zeros(x.shape, F32)
    for k, y_ref in enumerate((ya_ref, yb_ref, yc_ref)):
        sl = slice(k * D_MODEL, (k + 1) * D_MODEL)
        gate = _sigmoid(_mm(u, wg_ref[:, sl]) + bg_ref[:, sl])
        mix = mix + gate * _mm(y_ref[...], wbr_ref[k])
    o_ref[...] = x + _rms(_mm(mix.astype(BF16), wo_ref[...]), g_ref[3:4, :])


def _merge(x, ya, yb, yc, pw, layer):
    n = x.shape[0]
    tm = _row_tile(n, 512)
    row = lambda i: (i, 0)
    tile = pl.BlockSpec((tm, D_MODEL), row)
    return pl.pallas_call(
        _merge_body,
        grid=(n // tm,),
        in_specs=[tile, tile, tile, tile]
        + [_resident_slice(pw[k], (layer,)) for k in ("g", "wg", "bg", "wbr", "wo")],
        out_specs=tile,
        out_shape=jax.ShapeDtypeStruct((n, D_MODEL), F32),
        compiler_params=_params(1),
        name="merge",
    )(x, ya, yb, yc, pw["g"], pw["wg"], pw["bg"], pw["wbr"], pw["wo"])


def _ple_body(x_ref, p_ref, g_ref, wp_ref, wpg_ref, o_ref):
    x = x_ref[...]
    e = _mm(p_ref[...].astype(BF16), wp_ref[...])
    pg = _sigmoid(_mm(_rms(x, g_ref[6:7, :]).astype(BF16), wpg_ref[...]))
    o_ref[...] = x + _rms(pg * e, g_ref[7:8, :])


def _ple(x, p, pw, layer):
    n = x.shape[0]
    tm = _row_tile(n, 512)
    row = lambda i: (i, 0)
    return pl.pallas_call(
        _ple_body,
        grid=(n // tm,),
        in_specs=[pl.BlockSpec((tm, D_MODEL), row), pl.BlockSpec((tm, PLE_DIM), row)]
        + [_resident_slice(pw[k], (layer,)) for k in ("g", "wp", "wpg")],
        out_specs=pl.BlockSpec((tm, D_MODEL), row),
        out_shape=jax.ShapeDtypeStruct((n, D_MODEL), F32),
        compiler_params=_params(1),
        name="ple",
    )(x, p, pw["g"], pw["wp"], pw["wpg"])


BAND_TQ = 256
BAND_SUB = 128
BAND_KBLK = 1 + -(-BAND // BAND_TQ)


def _prep_weights(norm_g, ffn_w13, ffn_w2, w_in, w_gate, b_gate, w_branch, w_out, w_ple, w_ple_gate):
    bf = lambda a: a.astype(BF16)
    return {"g": norm_g, "w13": bf(ffn_w13), "w2": bf(ffn_w2), "w_in": _pack_w_in(w_in), "wg": bf(w_gate),
            "bg": b_gate.reshape(DEPTH, 1, N_BRANCH * D_MODEL), "wbr": bf(w_branch), "wo": bf(w_out),
            "wp": bf(w_ple), "wpg": bf(w_ple_gate)}


def _trunk_layer(x, p, w, pw, layer, cache, batch, t, shared):
    x = _ffn(x, pw, layer, 0, 0, 1)
    pr, shared = _inproj(x, pw, layer, batch, t, shared)
    if cache is None:
        ya, h_new = _mamba(pr["z"], pr["xbc"], pr["dtwi"], None, None, w, batch, t)
        yb = _dsa(pr["qb"], pr["qi"], pr["dtwi"], pr["ki"], pr["kb"], pr["vb"], layer, batch, t, t, t, 0)
        yc = _band(pr["qc"], pr["kc_bf"], pr["vc_bf"], w["rel_bias"], batch, t, BAND_TQ, BAND_SUB, BAND_TQ,
                   BAND_KBLK, 0, 0, True)
        conv_src = pr["xbc"].reshape(batch, t, CONV_CH)
    else:
        past = cache["dsa_k"].shape[1]
        conv0 = jnp.pad(cache["conv"], ((0, 0), (SUBLANES - (CONV_W - 1), 0), (0, 0)))
        h0 = cache["ssm"].reshape(batch, D_INNER, D_STATE)
        ya, h_new = _mamba(pr["z"], pr["xbc"], pr["dtwi"], conv0, h0, w, batch, t)

        n_valid = past + t
        s_keys = -(-n_valid // DSA_KB) * DSA_KB
        tq_pad = -(-t // DSA_TQ) * DSA_TQ

        def with_cache(c, new):
            width = new.shape[-1]
            a = jnp.concatenate([c.reshape(batch, past, width), new[layer].reshape(batch, t, width)], axis=1)
            return jnp.pad(a, ((0, 0), (0, s_keys - n_valid), (0, 0))).reshape(1, batch * s_keys, width)

        def pad_q(a):
            a = jnp.pad(a.reshape(batch, t, a.shape[-1]), ((0, 0), (0, tq_pad - t), (0, 0)))
            return a.reshape(batch * tq_pad, a.shape[-1])

        yb = _dsa(pad_q(pr["qb"]), pad_q(pr["qi"]), pad_q(pr["dtwi"]), with_cache(cache["idx_k"], pr["ki"]),
                  with_cache(cache["dsa_k"], pr["kb"]), with_cache(cache["dsa_v"], pr["vb"]),
                  0, batch, tq_pad, s_keys, n_valid, past)
        yb = yb.reshape(batch, tq_pad, D_MODEL)[:, :t].reshape(batch * t, D_MODEL)

        nrows = cache["band_k"].shape[1]
        k_rows = -(-(nrows + t) // LANES) * LANES
        lead = k_rows - nrows - t

        def with_band(c, new):
            a = jnp.concatenate([c.reshape(batch, nrows, D_MODEL).astype(BF16), new.reshape(batch, t, D_MODEL)],
                                axis=1)
            return jnp.pad(a, ((0, 0), (lead, 0), (0, 0))).reshape(batch * k_rows, D_MODEL)

        yc = _band(pr["qc"], with_band(cache["band_k"], pr["kc_bf"]), with_band(cache["band_v"], pr["vc_bf"]),
                   w["rel_bias"], batch, t, t, t, k_rows, 1, past, past - nrows, False)
        conv_src = jnp.concatenate([cache["conv"], pr["xbc"].reshape(batch, t, CONV_CH)], axis=1)

    x = _merge(x, ya, yb, yc, pw, layer)
    x = _ffn(x, pw, layer, 1, 4, 5)
    x = _ple(x, p.reshape(batch * t, PLE_DIM), pw, layer)

    state = (h_new.reshape(batch, SSM_HEADS, SSM_HEAD_DIM, D_STATE), conv_src[:, -(CONV_W - 1):])
    return x, state, shared


def _cache_outputs(shared, states, batch, t):
    band_rows = min(BAND, t)

    def band_tail(a):
        if a.ndim == 4:
            a = a.reshape(DEPTH, batch, BAND_HEADS, BAND_HEAD_DIM, band_rows)
            return jnp.transpose(a, (0, 1, 4, 2, 3))
        return a.reshape(DEPTH, batch, band_rows, BAND_HEADS, BAND_HEAD_DIM)

    return (shared["kb"].reshape(DEPTH, batch, t, KV_HEADS, DSA_HEAD_DIM),
            shared["vb"].reshape(DEPTH, batch, t, KV_HEADS, DSA_HEAD_DIM),
            shared["ki"].reshape(DEPTH, batch, t, IDX_DIM), band_tail(shared["kc"]), band_tail(shared["vc"]),
            jnp.stack([s[0] for s in states]), jnp.stack([s[1] for s in states]))


def kernel(x_prompt, x_sample, p_prompt, p_sample, cache_dsa_k, cache_dsa_v, cache_idx_k, cache_band_k,
           cache_band_v, state_ssm, state_conv, norm_g, ffn_w13, ffn_w2, w_in, conv_w, conv_b, dt_bias,
           a_log, d_skip, ssm_norm_g, rel_bias, w_gate, b_gate, w_branch, w_out, w_ple, w_ple_gate):
    bp, tp, _ = x_prompt.shape
    bs, ts, _ = x_sample.shape
    yp = x_prompt.reshape(bp * tp, D_MODEL)
    ys = x_sample.reshape(bs * ts, D_MODEL)
    st_p, st_s, shared_p, shared_s = [], [], None, None
    pw = _prep_weights(norm_g, ffn_w13, ffn_w2, w_in, w_gate, b_gate, w_branch, w_out, w_ple, w_ple_gate)
    for i in range(DEPTH):
        w = {"conv_w": conv_w[i], "conv_b": conv_b[i], "dt_bias": dt_bias[i], "a_log": a_log[i],
             "d_skip": d_skip[i], "ssm_norm_g": ssm_norm_g[i], "rel_bias": rel_bias[i]}
        yp, sp, shared_p = _trunk_layer(yp, p_prompt[i], w, pw, i, None, bp, tp, shared_p)
        st_p.append(sp)
        cache = {"dsa_k": cache_dsa_k[i], "dsa_v": cache_dsa_v[i], "idx_k": cache_idx_k[i],
                 "band_k": cache_band_k[i], "band_v": cache_band_v[i], "ssm": state_ssm[i], "conv": state_conv[i]}
        ys, ss, shared_s = _trunk_layer(ys, p_sample[i], w, pw, i, cache, bs, ts, shared_s)
        st_s.append(ss)
    return (yp.reshape(bp, tp, D_MODEL), ys.reshape(bs, ts, D_MODEL),
            *_cache_outputs(shared_p, st_p, bp, tp), *_cache_outputs(shared_s, st_s, bs, ts))
```

```python
import functools
import math

import numpy as np
import jax
import jax.numpy as jnp
from jax import lax
from jax.experimental import pallas as pl
from jax.experimental.pallas import tpu as pltpu

F32 = jnp.float32
BF16 = jnp.bfloat16

D_MODEL = 1024
DEPTH = 2
CHUNK = 64
EPS = 1e-6
HALF = 0.5
D_FF = 2816
PLE_DIM = 256
SSM_HEAD_DIM = 64
D_INNER = D_MODEL
SSM_HEADS = D_INNER // SSM_HEAD_DIM
N_GROUPS = 4
HEADS_PER_GROUP = SSM_HEADS // N_GROUPS
D_STATE = 128
CONV_W = 4
CONV_CH = D_INNER + 2 * N_GROUPS * D_STATE
DSA_HEAD_DIM = 128
DSA_HEADS = D_MODEL // DSA_HEAD_DIM
KV_HEADS = 2
KV_REP = DSA_HEADS // KV_HEADS
IDX_HEADS = 4
IDX_DIM = 64
TOPK_MAX = 256
BAND_HEAD_DIM = 64
BAND_HEADS = D_MODEL // BAND_HEAD_DIM
LEFT_CHUNKS = 8
BAND = LEFT_CHUNKS * CHUNK
REL_CLIP = 256
N_BRANCH = 3
IN_WIDTHS = (D_INNER, CONV_CH, SSM_HEADS,
             DSA_HEADS * DSA_HEAD_DIM, KV_HEADS * DSA_HEAD_DIM, KV_HEADS * DSA_HEAD_DIM,
             IDX_HEADS * IDX_DIM, IDX_DIM, IDX_HEADS,
             BAND_HEADS * BAND_HEAD_DIM, BAND_HEADS * BAND_HEAD_DIM, BAND_HEADS * BAND_HEAD_DIM)
IN_SPLITS = tuple(int(s) for s in np.cumsum(IN_WIDTHS)[:-1])

LANES = 128
SUBLANES = 8
KV_DIM = KV_HEADS * DSA_HEAD_DIM
GROUP_CH = D_INNER // N_GROUPS
NEG_BIG = -1e30
INT_MIN = -2 ** 31
LOG2E = math.log2(math.e)
VMEM_LIMIT = 56 * 1024 * 1024


def _mm(a, b):
    return jnp.dot(a, b, preferred_element_type=F32)


def _mm_nt(a, b):
    return lax.dot_general(a, b, (((1,), (1,)), ((), ())), preferred_element_type=F32)


def _mm_tn(a, b):
    return lax.dot_general(a, b, (((0,), (0,)), ((), ())), preferred_element_type=F32)


def _rms(x, g):
    return x * lax.rsqrt(jnp.mean(x * x, axis=-1, keepdims=True) + EPS) * g


def _sigmoid(x):
    return 1.0 / (1.0 + jnp.exp(-x))


def _silu(x):
    return x * _sigmoid(x)


def _resident(shape):
    return pl.BlockSpec(shape, lambda *_: (0,) * len(shape), pipeline_mode=pl.Buffered(1))


def _resident_slice(arr, lead, block=None, at=None):
    tail = tuple(arr.shape[len(lead):]) if block is None else tuple(block)
    idx = tuple(lead) + ((0,) * len(tail) if at is None else tuple(at))
    return pl.BlockSpec((None,) * len(lead) + tail, lambda *_: idx, pipeline_mode=pl.Buffered(1))


def _params(n_grid_dims):
    return pltpu.CompilerParams(dimension_semantics=("arbitrary",) * n_grid_dims,
                                vmem_limit_bytes=VMEM_LIMIT)


def _row_tile(n_rows, want):
    t = min(want, n_rows)
    assert n_rows % t == 0
    return t


def _fold_rows(x, op):
    parts = [x[i * SUBLANES:(i + 1) * SUBLANES] for i in range(x.shape[0] // SUBLANES)]
    while len(parts) > 1:
        parts = [op(parts[i], parts[i + 1]) for i in range(0, len(parts) - 1, 2)] + parts[len(parts) & ~1:]
    return parts[0]


FF_CHUNK = 256


def _ffn_body(x_ref, g_ref, wa_ref, wb_ref, w2_ref, o_ref, *, g_pre, g_post):
    x = x_ref[...]
    u = _rms(x, g_ref[g_pre:g_pre + 1, :]).astype(BF16)
    acc = jnp.zeros(x.shape, F32)
    for c in range(D_FF // FF_CHUNK):
        sl = slice(c * FF_CHUNK, (c + 1) * FF_CHUNK)
        a = _mm(u, wa_ref[:, sl])
        b = _mm(u, wb_ref[:, sl])
        acc = acc + _mm((_silu(a) * b).astype(BF16), w2_ref[sl, :])
    o_ref[...] = x + HALF * _rms(acc, g_ref[g_post:g_post + 1, :])


def _ffn(x, pw, layer, j, g_pre, g_post):
    n = x.shape[0]
    tm = _row_tile(n, 1024)
    row = lambda i: (i, 0)
    half = (D_MODEL, D_FF)
    return pl.pallas_call(
        functools.partial(_ffn_body, g_pre=g_pre, g_post=g_post),
        grid=(n // tm,),
        in_specs=[pl.BlockSpec((tm, D_MODEL), row), _resident_slice(pw["g"], (layer,)),
                  _resident_slice(pw["w13"], (layer, j), half, (0, 0)),
                  _resident_slice(pw["w13"], (layer, j), half, (0, 1)),
                  _resident_slice(pw["w2"], (layer, j))],
        out_specs=pl.BlockSpec((tm, D_MODEL), row),
        out_shape=jax.ShapeDtypeStruct((n, D_MODEL), F32),
        compiler_params=_params(1),
        name="ffn",
    )(x, pw["g"], pw["w13"], pw["w13"], pw["w2"])


_INPROJ_GROUPS = (D_INNER, CONV_CH, D_MODEL, KV_DIM, KV_DIM, IDX_HEADS * IDX_DIM, D_MODEL, D_MODEL, D_MODEL,
                  LANES, LANES)
_INPROJ_OUT = (
    ("z", 0, D_INNER, F32, None, "layer"),
    ("xbc", 1, CONV_CH, F32, None, "layer"),
    ("qb", 2, D_MODEL, BF16, DSA_HEAD_DIM ** -0.5 * LOG2E, "layer"),
    ("kb", 3, KV_DIM, F32, None, "stack"),
    ("vb", 4, KV_DIM, F32, None, "stack"),
    ("qi", 5, IDX_HEADS * IDX_DIM, F32, None, "layer"),
    ("qc", 6, D_MODEL, BF16, BAND_HEAD_DIM ** -0.5 * LOG2E, "layer"),
    ("kc", 7, D_MODEL, F32, None, "tail"),
    ("kc_bf", 7, D_MODEL, BF16, None, "layer"),
    ("vc", 8, D_MODEL, F32, None, "tail"),
    ("vc_bf", 8, D_MODEL, BF16, None, "layer"),
    ("ki", 9, IDX_DIM, F32, None, "stack"),
    ("dtwi", 10, LANES, F32, None, "layer"),
)
WI_LANE = SSM_HEADS


def _pack_w_in(w_in):
    z, xbc, dt, qb, kb, vb, qi, ki, wi, qc, kc, vc = jnp.split(w_in, IN_SPLITS, axis=-1)
    pad = lambda w: jnp.pad(w, ((0, 0), (0, 0), (0, LANES - w.shape[-1])))
    cols = [z, xbc, qb, kb, vb, qi, qc, kc, vc, pad(ki), pad(jnp.concatenate([dt, wi], axis=-1))]
    return jnp.concatenate(cols, axis=-1).astype(BF16)


def _inproj_body(x_ref, g_ref, w_ref, *refs, transpose_tails):
    out_refs = refs[len(refs) - len(_INPROJ_OUT):]
    u = _rms(x_ref[...], g_ref[2:3, :]).astype(BF16)
    starts = np.concatenate([[0], np.cumsum(_INPROJ_GROUPS)])
    for grp, width in enumerate(_INPROJ_GROUPS):
        r = _mm(u, w_ref[:, int(starts[grp]):int(starts[grp]) + width])
        for (_, og, stored, dtype, scale, kind), o_ref in zip(_INPROJ_OUT, out_refs):
            if og == grp:
                v = r if scale is None else r * scale
                v = (v if stored == width else v[:, :stored]).astype(dtype)
                o_ref[...] = v.T if (kind == "tail" and transpose_tails) else v


def _inproj(x, pw, layer, batch, t, shared):
    n = x.shape[0]
    tm = _row_tile(n, 256)
    tail = min(BAND, t)
    row = lambda i: (i, 0)
    stack_row = lambda i: (layer, i, 0)
    transpose_tails = tail != t
    if transpose_tails:
        assert t % tm == 0 and tail % tm == 0
        per_seq, per_tail = t // tm, tail // tm
        tail_blk = lambda i: (layer, i // per_seq, 0, jnp.maximum(i % per_seq - (per_seq - per_tail), 0))
    out_specs, out_shape, stacked = [], [], []
    for k, (name, _, width, dtype, _, kind) in enumerate(_INPROJ_OUT):
        if kind == "layer":
            out_specs.append(pl.BlockSpec((tm, width), row))
            out_shape.append(jax.ShapeDtypeStruct((n, width), dtype))
        elif kind == "tail" and transpose_tails:
            out_specs.append(pl.BlockSpec((None, None, width, tm), tail_blk))
            out_shape.append(jax.ShapeDtypeStruct((DEPTH, batch, width, tail), dtype))
            stacked.append((name, k))
        else:
            out_specs.append(pl.BlockSpec((None, tm, width), stack_row))
            out_shape.append(jax.ShapeDtypeStruct((DEPTH, n, width), dtype))
            stacked.append((name, k))
    carried = [] if shared is None else [shared[name] for name, _ in stacked]
    aliases = {} if shared is None else {3 + a: k for a, (_, k) in enumerate(stacked)}
    outs = pl.pallas_call(
        functools.partial(_inproj_body, transpose_tails=transpose_tails),
        grid=(n // tm,),
        in_specs=[pl.BlockSpec((tm, D_MODEL), row), _resident_slice(pw["g"], (layer,)),
                  _resident_slice(pw["w_in"], (layer,))] + [pl.BlockSpec(memory_space=pl.ANY)] * len(carried),
        out_specs=out_specs,
        out_shape=out_shape,
        input_output_aliases=aliases,
        compiler_params=_params(1),
        name="inproj",
    )(x, pw["g"], pw["w_in"], *carried)
    pr = {o[0]: a for o, a in zip(_INPROJ_OUT, outs)}
    return pr, {name: pr[name] for name, _ in stacked}


CONV_PAD = 16
CONV_ROWS = CONV_PAD + CHUNK
MAMBA_PAIR = 4


def _split3(x):
    hi = x.astype(BF16)
    r = x - hi.astype(F32)
    mid = r.astype(BF16)
    lo = (r - mid.astype(F32)).astype(BF16)
    return hi, mid, lo


def _expand_heads(x, e):
    hi, mid, lo = _split3(x)
    return _mm(hi, e) + _mm(mid, e) + _mm(lo, e)


def _cumsum_rows(x):
    n = x.shape[0]
    row = lax.broadcasted_iota(jnp.int32, x.shape, 0)
    d = 1
    while d < n:
        x = x + jnp.where(row >= d, pltpu.roll(x, d, 0), 0.0)
        d *= 2
    return x


def _conv_shift_matrix():
    s = np.zeros(((CONV_W - 1) * CHUNK, 3 * CONV_ROWS), np.float32)
    for k in range(CONV_W - 1):
        for l in range(CHUNK):
            for part in range(3):
                s[k * CHUNK + l, part * CONV_ROWS + CONV_PAD - (CONV_W - 1) + k + l] = 1.0
    return jnp.asarray(s, BF16)


def _mamba_body(*refs, has_state, n_chunks):
    if has_state:
        (z_ref, xbc_ref, dtwi_ref, conv0_ref, h0_ref, cw_ref, cb_ref, dtb_ref, alog_ref, dskip_ref,
         ng_ref, e_ref, shift_ref, y_ref, hout_ref, tail_scr, ht_scr) = refs
    else:
        (z_ref, xbc_ref, dtwi_ref, cw_ref, cb_ref, dtb_ref, alog_ref, dskip_ref,
         ng_ref, e_ref, shift_ref, y_ref, hout_ref, tail_scr, ht_scr) = refs
    c = pl.program_id(1)
    q = CHUNK
    tail_rows = slice(CONV_PAD - SUBLANES, CONV_PAD)

    @pl.when(c == 0)
    def _():
        tail_scr[...] = jnp.zeros(tail_scr.shape, F32)
        if has_state:
            for b in range(MAMBA_PAIR):
                tail_scr[b, tail_rows, :] = conv0_ref[b]
                ht_scr[b] = h0_ref[b].T
        else:
            ht_scr[...] = jnp.zeros(ht_scr.shape, F32)

    lane = lax.broadcasted_iota(jnp.int32, (q, LANES), 1)
    li = lax.broadcasted_iota(jnp.int32, (q, D_INNER), 0)
    si = lax.broadcasted_iota(jnp.int32, (q, D_INNER), 1) & (q - 1)
    diag = li == si
    causal = (li >= si)[:, :GROUP_CH]
    bdr = lax.broadcasted_iota(jnp.int32, (GROUP_CH, GROUP_CH), 0) // SSM_HEAD_DIM
    bdc = lax.broadcasted_iota(jnp.int32, (GROUP_CH, GROUP_CH), 1) // SSM_HEAD_DIM
    block_diag = bdr == bdc
    neg_a = -jnp.exp(alog_ref[...])

    def sequence(b):
        x = xbc_ref[b]
        hi, mid, lo = _split3(jnp.concatenate([tail_scr[b], x], axis=0))
        delayed = _mm(shift_ref[...], jnp.concatenate([hi, mid, lo], axis=0))
        tail_scr[b, tail_rows, :] = x[q - SUBLANES:q, :]
        pre = dtwi_ref[b] + dtb_ref[...]
        dt = jnp.maximum(pre, 0.0) + jnp.log1p(jnp.exp(-jnp.abs(pre)))
        dt = jnp.where(lane < SSM_HEADS, dt, 0.0)
        cum = _cumsum_rows(dt * neg_a)
        e = e_ref[...]
        ecol = _expand_heads(cum, e)
        dtx = _expand_heads(dt, e)
        yield

        acc = cb_ref[...] + x * cw_ref[CONV_W - 1:CONV_W, :]
        for k in range(CONV_W - 1):
            acc = acc + delayed[k * q:(k + 1) * q, :] * cw_ref[k:k + 1, :]
        xc = _silu(acc)
        xs = xc[:, :D_INNER]
        bm = xc[:, D_INNER:D_INNER + N_GROUPS * D_STATE].astype(BF16)
        cm = xc[:, D_INNER + N_GROUPS * D_STATE:].astype(BF16)
        cbs = [_mm_nt(cm[:, g * D_STATE:(g + 1) * D_STATE], bm[:, g * D_STATE:(g + 1) * D_STATE])
               for g in range(N_GROUPS)]
        yield

        erow = jnp.sum(jnp.where(diag, ecol, 0.0), axis=0, keepdims=True)
        elast = ecol[q - 1:q, :]
        xdt = xs * dtx
        xdec = (xdt * jnp.exp(elast - ecol)).astype(BF16)
        exp_e = jnp.exp(ecol)
        chunk_decay = jnp.exp(elast)
        yield

        ys = []
        for g in range(N_GROUPS):
            sl = slice(g * GROUP_CH, (g + 1) * GROUP_CH)
            nl = slice(g * D_STATE, (g + 1) * D_STATE)
            cbt = jnp.concatenate([cbs[g]] * HEADS_PER_GROUP, axis=1)
            decay = jnp.exp(jnp.where(causal, ecol[:, sl] - erow[:, sl], NEG_BIG))
            m = (cbt * decay).astype(BF16)
            xg = xdt[:, sl]
            bd = jnp.where(block_diag, jnp.concatenate([xg] * HEADS_PER_GROUP, axis=0), 0.0).astype(BF16)
            y_diag = _mm(m, bd)
            ht_g = ht_scr[b, :, sl]
            y_off = _mm(cm[:, nl], ht_g.astype(BF16)) * exp_e[:, sl]
            ys.append(y_diag + y_off)
            ht_scr[b, :, sl] = ht_g * chunk_decay[:, sl] + _mm_tn(bm[:, nl], xdec[:, sl])
            yield

        y = jnp.concatenate(ys, axis=1) + dskip_ref[...] * xs
        y = y * _silu(z_ref[b])
        outs = []
        for g in range(N_GROUPS):
            yg = y[:, g * GROUP_CH:(g + 1) * GROUP_CH]
            outs.append(yg * lax.rsqrt(jnp.mean(yg * yg, axis=-1, keepdims=True) + EPS))
        y_ref[b] = (jnp.concatenate(outs, axis=1) * ng_ref[...]).astype(y_ref.dtype)
        yield

    for _ in zip(*[sequence(b) for b in range(MAMBA_PAIR)]):
        pass

    @pl.when(c == n_chunks - 1)
    def _():
        for b in range(MAMBA_PAIR):
            hout_ref[b] = ht_scr[b].T


def _head_expand_matrix():
    e = np.zeros((LANES, D_INNER), np.float32)
    for h in range(SSM_HEADS):
        e[h, h * SSM_HEAD_DIM:(h + 1) * SSM_HEAD_DIM] = 1.0
    return jnp.asarray(e, BF16)


def _mamba(z, xbc, dtwi, conv0, h0, lw, batch, t):
    nc = t // CHUNK
    assert batch % MAMBA_PAIR == 0
    has_state = h0 is not None
    chunk = lambda b, c: (b, c, 0)
    per_b = lambda b, c: (b, 0, 0)
    pad16 = lambda v: jnp.pad(v.reshape(1, SSM_HEADS), ((0, 0), (0, LANES - SSM_HEADS)))
    small = [lw["conv_w"], lw["conv_b"].reshape(1, CONV_CH), pad16(lw["dt_bias"]), pad16(lw["a_log"]),
             jnp.repeat(lw["d_skip"], SSM_HEAD_DIM).reshape(1, D_INNER),
             lw["ssm_norm_g"].reshape(1, D_INNER), _head_expand_matrix(), _conv_shift_matrix()]
    ins = [a.reshape(batch, t, a.shape[-1]) for a in (z, xbc, dtwi)]
    in_specs = [pl.BlockSpec((MAMBA_PAIR, CHUNK, a.shape[-1]), chunk) for a in ins]
    if has_state:
        ins += [conv0, h0]
        in_specs += [pl.BlockSpec((MAMBA_PAIR, SUBLANES, CONV_CH), per_b),
                     pl.BlockSpec((MAMBA_PAIR, D_INNER, D_STATE), per_b)]
    ins += small
    in_specs += [_resident(a.shape) for a in small]
    y, h_out = pl.pallas_call(
        functools.partial(_mamba_body, has_state=has_state, n_chunks=nc),
        grid=(batch // MAMBA_PAIR, nc),
        in_specs=in_specs,
        out_specs=[pl.BlockSpec((MAMBA_PAIR, CHUNK, D_INNER), chunk),
                   pl.BlockSpec((MAMBA_PAIR, D_INNER, D_STATE), per_b)],
        out_shape=[jax.ShapeDtypeStruct((batch, t, D_INNER), BF16),
                   jax.ShapeDtypeStruct((batch, D_INNER, D_STATE), F32)],
        scratch_shapes=[pltpu.VMEM((MAMBA_PAIR, CONV_PAD, CONV_CH), F32),
                        pltpu.VMEM((MAMBA_PAIR, D_STATE, D_INNER), F32)],
        compiler_params=_params(2),
        name="mamba",
    )(*ins)
    return y.reshape(batch * t, D_INNER), h_out


DSA_TQ = LANES
DSA_KB = 256
DSA_PAIR = 2
IDX_CAT = 4 * IDX_DIM
DSA_RANK_ROWS = 128


def _hi_lo(x):
    hi = x.astype(BF16).astype(F32)
    return hi, (x - hi).astype(BF16).astype(F32)


def _dsa_body(qb_ref, qi_ref, dtwi_ref, ki_ref, k_ref, v_ref, o_ref,
              kcat_scr, kbf_scr, vt_scr, tri_scr, key_scr, mask_scr, s_scr,
              *, n_kb, n_valid, q_pos0, topk, visible):
    tq, kb_rows = DSA_TQ, DSA_KB
    j = pl.program_id(1)

    @pl.when(j == 0)
    def _():
        hi, lo = _hi_lo(ki_ref[...])
        kcat_scr[...] = jnp.concatenate([hi, lo, hi, jnp.zeros_like(hi)], axis=1).astype(BF16)
        kbf_scr[...] = k_ref[...].astype(BF16)
        for kb in range(n_kb):
            vt = v_ref[kb * kb_rows:(kb + 1) * kb_rows, :].T
            for g in range(KV_HEADS):
                vt_scr[g, kb] = vt[g * DSA_HEAD_DIM:(g + 1) * DSA_HEAD_DIM, :].astype(BF16)
        tr = lax.broadcasted_iota(jnp.int32, (DSA_RANK_ROWS, DSA_RANK_ROWS), 0)
        tc = lax.broadcasted_iota(jnp.int32, (DSA_RANK_ROWS, DSA_RANK_ROWS), 1)
        tri_scr[...] = jnp.where(tc < tr, 1.0, 0.0).astype(BF16)

    n_q = qb_ref.shape[0] // tq
    last_end = jnp.minimum((((q_pos0 + (j + 1) * n_q * tq - 1) >> 6) + 1) << 6, n_valid)
    nkb = (last_end + (kb_rows - 1)) // kb_rows
    krow = lax.broadcasted_iota(jnp.int32, (kb_rows, 1), 0)

    def prepare(qn):
        rows = slice(qn * tq, (qn + 1) * tq)
        qpos = q_pos0 + (j * n_q + qn) * tq + lax.broadcasted_iota(jnp.int32, (1, tq), 1)
        q_end = jnp.minimum(((qpos >> 6) + 1) << 6, n_valid)
        wit = (dtwi_ref[rows, :] * (IDX_DIM ** -0.5 * IDX_HEADS ** -0.5)).T
        qi = qi_ref[rows, :]
        qparts = []
        for h in range(IDX_HEADS):
            hi, lo = _hi_lo(qi[:, h * IDX_DIM:(h + 1) * IDX_DIM])
            qparts.append(jnp.concatenate([hi, hi, lo, jnp.zeros_like(hi)], axis=1))
        qb = qb_ref[rows, :]
        q4 = [jnp.concatenate([qb[:, (g * KV_REP + r) * DSA_HEAD_DIM:(g * KV_REP + r + 1) * DSA_HEAD_DIM]
                               for r in range(KV_REP)], axis=0) for g in range(KV_HEADS)]
        return dict(rows=rows, q_end=q_end, k_eff=jnp.minimum(q_end, topk).astype(F32), wit=wit,
                    qcat=jnp.concatenate(qparts, axis=0).astype(BF16), q4=q4)

    qs = [prepare(qn) for qn in range(n_q)]

    def run(n_vis):
        for qn, q in enumerate(qs):
            for kb in range(n_vis):
                logit = _mm_nt(kcat_scr[kb * kb_rows:(kb + 1) * kb_rows, :], q["qcat"])
                sc = jnp.zeros((kb_rows, tq), F32)
                for h in range(IDX_HEADS):
                    sc = sc + (jnp.maximum(logit[:, h * tq:(h + 1) * tq], 0.0)
                               * q["wit"][WI_LANE + h:WI_LANE + h + 1, :])
                sc = jnp.where(kb * kb_rows + krow < q["q_end"], sc, -jnp.inf)
                bits = lax.bitcast_convert_type(sc, jnp.int32)
                key_scr[qn, kb] = jnp.where(bits < 0, bits ^ jnp.int32(0x7FFFFFFF), bits)

        def count(qn, pred):
            acc = _fold_rows(jnp.where(pred(key_scr[qn, 0]), 1.0, 0.0), jnp.add)
            for kb in range(1, n_vis):
                acc = acc + _fold_rows(jnp.where(pred(key_scr[qn, kb]), 1.0, 0.0), jnp.add)
            return jnp.sum(acc, axis=0, keepdims=True)

        def radix_step(i, tus):
            bit = lax.shift_left(jnp.int32(1), 31 - i)
            new = []
            for qn, q in enumerate(qs):
                cand = tus[qn] | bit
                thr_c = cand ^ jnp.int32(INT_MIN)
                new.append(jnp.where(count(qn, lambda k: k >= thr_c) >= q["k_eff"], cand, tus[qn]))
            return tuple(new)

        tus = lax.fori_loop(0, 32, radix_step, tuple(jnp.zeros((1, tq), jnp.int32) for _ in qs))

        def finish(qn, q, thr):
            need = q["k_eff"] - count(qn, lambda k: k > thr)
            before = jnp.zeros((1, tq), F32)
            for kb in range(n_vis):
                for r0 in range(0, kb_rows, DSA_RANK_ROWS):
                    key = key_scr[qn, kb, r0:r0 + DSA_RANK_ROWS, :]
                    eq = jnp.where(key == thr, 1.0, 0.0)
                    rank = _mm(tri_scr[...], eq.astype(BF16)) + before
                    take = jnp.where(key > thr, 1.0, jnp.where(rank < need, eq, 0.0))
                    mask_scr[qn, kb, r0:r0 + DSA_RANK_ROWS, :] = jnp.where(take > 0.0, 0.0, NEG_BIG)
                    before = before + jnp.sum(_fold_rows(eq, jnp.add), axis=0, keepdims=True)
            yield

            m8 = [jnp.full((SUBLANES, KV_REP * tq), NEG_BIG, F32) for _ in range(KV_HEADS)]
            for kb in range(n_vis):
                mask4 = jnp.concatenate([mask_scr[qn, kb]] * KV_REP, axis=1)
                for g in range(KV_HEADS):
                    kg = kbf_scr[kb * kb_rows:(kb + 1) * kb_rows, g * DSA_HEAD_DIM:(g + 1) * DSA_HEAD_DIM]
                    st = _mm_nt(kg, q["q4"][g]) + mask4
                    s_scr[qn, kb, g] = st
                    m8[g] = jnp.maximum(m8[g], _fold_rows(st, jnp.maximum))
            m = [jnp.max(m8[g], axis=0, keepdims=True) for g in range(KV_HEADS)]
            yield

            acc, den = [None] * KV_HEADS, [None] * KV_HEADS
            for kb in range(n_vis):
                for g in range(KV_HEADS):
                    p = jnp.exp2(s_scr[qn, kb, g] - m[g])
                    pv = _mm(vt_scr[g, kb], p.astype(BF16))
                    psum = _fold_rows(p, jnp.add)
                    acc[g] = pv if acc[g] is None else acc[g] + pv
                    den[g] = psum if den[g] is None else den[g] + psum
            outs = []
            for g in range(KV_HEADS):
                ot = acc[g] / jnp.sum(den[g], axis=0, keepdims=True)
                outs += [ot[:, r * tq:(r + 1) * tq].T for r in range(KV_REP)]
            o_ref[q["rows"], :] = jnp.concatenate(outs, axis=1).astype(o_ref.dtype)
            yield

        for _ in zip(*[finish(qn, q, tus[qn] ^ jnp.int32(INT_MIN)) for qn, q in enumerate(qs)]):
            pass

    if len(visible) == 1:
        run(visible[0])
    else:
        for n_vis in visible:
            pl.when(nkb == n_vis)(functools.partial(run, n_vis))


def _dsa(qb, qi, dtwi, ki_all, k_all, v_all, key_layer, batch, t, s_keys, n_valid, q_pos0):
    tq, kb_rows = DSA_TQ, DSA_KB
    assert t % tq == 0 and s_keys % kb_rows == 0
    n_q = DSA_PAIR if (t // tq) % DSA_PAIR == 0 else 1
    nq = t // (n_q * tq)
    n_kb = s_keys // kb_rows
    topk = min(TOPK_MAX, n_valid // 4)
    qrow = lambda b, j: (b * nq + j, 0)
    krow = lambda b, j: (key_layer, b, 0)

    def visible_blocks(j):
        last_end = min(((((q_pos0 + (j + 1) * n_q * tq - 1) >> 6) + 1) << 6), n_valid)
        return (last_end + kb_rows - 1) // kb_rows

    visible = tuple(sorted({visible_blocks(j) for j in range(nq)}))
    return pl.pallas_call(
        functools.partial(_dsa_body, n_kb=n_kb, n_valid=n_valid, q_pos0=q_pos0, topk=topk, visible=visible),
        grid=(batch, nq),
        in_specs=[pl.BlockSpec((n_q * tq, D_MODEL), qrow), pl.BlockSpec((n_q * tq, IDX_HEADS * IDX_DIM), qrow),
                  pl.BlockSpec((n_q * tq, LANES), qrow), pl.BlockSpec((None, s_keys, IDX_DIM), krow),
                  pl.BlockSpec((None, s_keys, KV_DIM), krow), pl.BlockSpec((None, s_keys, KV_DIM), krow)],
        out_specs=pl.BlockSpec((n_q * tq, D_MODEL), qrow),
        out_shape=jax.ShapeDtypeStruct((batch * t, D_MODEL), BF16),
        scratch_shapes=[pltpu.VMEM((s_keys, IDX_CAT), BF16), pltpu.VMEM((s_keys, KV_DIM), BF16),
                        pltpu.VMEM((KV_HEADS, n_kb, DSA_HEAD_DIM, kb_rows), BF16),
                        pltpu.VMEM((DSA_RANK_ROWS, DSA_RANK_ROWS), BF16),
                        pltpu.VMEM((n_q, n_kb, kb_rows, tq), jnp.int32), pltpu.VMEM((n_q, n_kb, kb_rows, tq), F32),
                        pltpu.VMEM((n_q, n_kb, KV_HEADS, kb_rows, KV_REP * tq), F32)],
        compiler_params=_params(2),
        name="dsa",
    )(qb, qi, dtwi, ki_all, k_all, v_all)


def _band_body(*refs, tq, sub, n_kblk, q_pos0, k_min, clamped):
    q_ref = refs[0]
    k_refs = refs[1:1 + n_kblk]
    v_refs = refs[1 + n_kblk:1 + 2 * n_kblk]
    vec_ref, o_ref, bias_scr = refs[1 + 2 * n_kblk:]
    w = sum(r.shape[0] for r in k_refs)
    wsub = w - tq + sub
    i = pl.program_id(1)

    @pl.when((pl.program_id(0) == 0) & (i == 0))
    def _():
        r = lax.broadcasted_iota(jnp.int32, (sub, wsub), 0)
        c = lax.broadcasted_iota(jnp.int32, (sub, wsub), 1)
        dchunk = (r >> 6) + (wsub - sub) // CHUNK - (c >> 6)
        band_mask = jnp.where((dchunk >= 0) & (dchunk <= LEFT_CHUNKS), 0.0, NEG_BIG)
        for h in range(BAND_HEADS):
            rows = jnp.broadcast_to(vec_ref[h:h + 1, :], (sub, vec_ref.shape[1]))
            toeplitz = pltpu.roll(rows, 0, 1, stride=1, stride_axis=0)[:, :wsub]
            bias_scr[h // 2, (h % 2) * sub:(h % 2 + 1) * sub, :] = toeplitz * LOG2E + band_mask

    lane = lax.broadcasted_iota(jnp.int32, (sub, LANES), 1)
    first_head = lane < BAND_HEAD_DIM
    keep_a = jnp.where(first_head, 1.0, 0.0).astype(BF16)
    keep_b = jnp.where(first_head, 0.0, 1.0).astype(BF16)
    ones = jnp.ones((w, LANES), BF16)
    n_sub = tq // sub

    def all_rows(blocks, lanes):
        parts = [blk[:, lanes].astype(BF16) for blk in blocks]
        return parts[0] if len(parts) == 1 else jnp.concatenate(parts, axis=0)

    def scores(hp):
        lanes = slice(hp * LANES, (hp + 1) * LANES)
        q2 = []
        for c2 in range(n_sub):
            qp = q_ref[c2 * sub:(c2 + 1) * sub, lanes]
            q2 += [qp * keep_a, qp * keep_b]
        return _mm_nt(jnp.concatenate(q2, axis=0), all_rows(k_refs, lanes))

    def attend(hp, s_full, mask_missing_keys):
        lanes = slice(hp * LANES, (hp + 1) * LANES)
        ps = []
        for c2 in range(n_sub):
            s = s_full[c2 * 2 * sub:(c2 + 1) * 2 * sub, c2 * sub:c2 * sub + wsub] + bias_scr[hp]
            if mask_missing_keys:
                kpos = q_pos0 + i * tq + c2 * sub + (sub - wsub) + lax.broadcasted_iota(jnp.int32, (1, wsub), 1)
                s = s + jnp.where(kpos >= k_min, 0.0, NEG_BIG)
            p = jnp.exp2(s - jnp.max(s, axis=1, keepdims=True)).astype(BF16)
            pad = [jnp.zeros((2 * sub, c2 * sub), BF16)] if c2 else []
            pad_r = [jnp.zeros((2 * sub, w - wsub - c2 * sub), BF16)] if w - wsub - c2 * sub else []
            ps.append(jnp.concatenate(pad + [p] + pad_r, axis=1) if pad or pad_r else p)
        p_full = ps[0] if n_sub == 1 else jnp.concatenate(ps, axis=0)
        o = _mm(p_full, jnp.concatenate([all_rows(v_refs, lanes), ones], axis=1))
        for c2 in range(n_sub):
            ra, rb = c2 * 2 * sub, c2 * 2 * sub + sub
            oa = o[ra:ra + sub, :LANES] / o[ra:ra + sub, LANES:LANES + 1]
            ob = o[rb:rb + sub, :LANES] / o[rb:rb + sub, LANES:LANES + 1]
            o_ref[c2 * sub:(c2 + 1) * sub, lanes] = jnp.where(first_head, oa, ob).astype(o_ref.dtype)

    def heads(mask_missing_keys):
        n_pairs, ahead, pending = BAND_HEADS // 2, 2, {}
        for n in range(n_pairs + ahead):
            if n < n_pairs:
                pending[n] = scores(n)
            if n >= ahead:
                attend(n - ahead, pending.pop(n - ahead), mask_missing_keys)

    if clamped:
        first_full = -(-(w - tq) // tq)
        pl.when(i < first_full)(functools.partial(heads, True))
        pl.when(i >= first_full)(functools.partial(heads, False))
    else:
        heads(False)


def _band_bias_vec(rel_bias, sub, wsub):
    l = -(-(wsub + sub) // LANES) * LANES
    m = np.arange(l)
    d = np.where(m < wsub, m, m - l)
    rel = np.clip(wsub - sub - d, -REL_CLIP, REL_CLIP) + REL_CLIP
    return rel_bias[jnp.asarray(rel)].T


def _band(q, k, v, rel_bias, batch, t, tq, sub, k_block_rows, n_kblk, q_pos0, k_min, clamped):
    nq = t // tq
    w = n_kblk * k_block_rows
    wsub = w - tq + sub
    assert wsub % LANES == 0 and (wsub - sub) % CHUNK == 0
    qrow = lambda b, i: (b * nq + i, 0)

    def krow(off):
        if clamped:
            return lambda b, i: (b * nq + jnp.maximum(i - (n_kblk - 1) + off, 0), 0)
        return lambda b, i: (b * n_kblk + off, 0)

    vec = _band_bias_vec(rel_bias, sub, wsub)
    kspecs = [pl.BlockSpec((k_block_rows, D_MODEL), krow(o)) for o in range(n_kblk)]
    return pl.pallas_call(
        functools.partial(_band_body, tq=tq, sub=sub, n_kblk=n_kblk, q_pos0=q_pos0, k_min=k_min, clamped=clamped),
        grid=(batch, nq),
        in_specs=[pl.BlockSpec((tq, D_MODEL), qrow)] + kspecs + kspecs + [_resident(vec.shape)],
        out_specs=pl.BlockSpec((tq, D_MODEL), qrow),
        out_shape=jax.ShapeDtypeStruct((batch * t, D_MODEL), BF16),
        scratch_shapes=[pltpu.VMEM((BAND_HEADS // 2, 2 * sub, wsub), F32)],
        compiler_params=_params(2),
        name="band",
    )(q, *([k] * n_kblk), *([v] * n_kblk), vec)


def _merge_body(x_ref, ya_ref, yb_ref, yc_ref, g_ref, wg_ref, bg_ref, wbr_ref, wo_ref, o_ref):
    x = x_ref[...]
    u = _rms(x, g_ref[2:3, :]).astype(BF16)
    mix = jnp.zeros(x.shape, F32)
    for k, y_ref in enumerate((ya_ref, yb_ref, yc_ref)):
        sl = slice(k * D_MODEL, (k + 1) * D_MODEL)
        gate = _sigmoid(_mm(u, wg_ref[:, sl]) + bg_ref[:, sl])
        mix = mix + gate * _mm(y_ref[...], wbr_ref[k])
    o_ref[...] = x + _rms(_mm(mix.astype(BF16), wo_ref[...]), g_ref[3:4, :])


def _merge(x, ya, yb, yc, pw, layer):
    n = x.shape[0]
    tm = _row_tile(n, 512)
    row = lambda i: (i, 0)
    tile = pl.BlockSpec((tm, D_MODEL), row)
    return pl.pallas_call(
        _merge_body,
        grid=(n // tm,),
        in_specs=[tile, tile, tile, tile]
        + [_resident_slice(pw[k], (layer,)) for k in ("g", "wg", "bg", "wbr", "wo")],
        out_specs=tile,
        out_shape=jax.ShapeDtypeStruct((n, D_MODEL), F32),
        compiler_params=_params(1),
        name="merge",
    )(x, ya, yb, yc, pw["g"], pw["wg"], pw["bg"], pw["wbr"], pw["wo"])


def _ple_body(x_ref, p_ref, g_ref, wp_ref, wpg_ref, o_ref):
    x = x_ref[...]
    e = _mm(p_ref[...].astype(BF16), wp_ref[...])
    pg = _sigmoid(_mm(_rms(x, g_ref[6:7, :]).astype(BF16), wpg_ref[...]))
    o_ref[...] = x + _rms(pg * e, g_ref[7:8, :])


def _ple(x, p, pw, layer):
    n = x.shape[0]
    tm = _row_tile(n, 512)
    row = lambda i: (i, 0)
    return pl.pallas_call(
        _ple_body,
        grid=(n // tm,),
        in_specs=[pl.BlockSpec((tm, D_MODEL), row), pl.BlockSpec((tm, PLE_DIM), row)]
        + [_resident_slice(pw[k], (layer,)) for k in ("g", "wp", "wpg")],
        out_specs=pl.BlockSpec((tm, D_MODEL), row),
        out_shape=jax.ShapeDtypeStruct((n, D_MODEL), F32),
        compiler_params=_params(1),
        name="ple",
    )(x, p, pw["g"], pw["wp"], pw["wpg"])


BAND_TQ = 256
BAND_SUB = 128
BAND_KBLK = 1 + -(-BAND // BAND_TQ)


def _prep_weights(norm_g, ffn_w13, ffn_w2, w_in, w_gate, b_gate, w_branch, w_out, w_ple, w_ple_gate):
    bf = lambda a: a.astype(BF16)
    return {"g": norm_g, "w13": bf(ffn_w13), "w2": bf(ffn_w2), "w_in": _pack_w_in(w_in), "wg": bf(w_gate),
            "bg": b_gate.reshape(DEPTH, 1, N_BRANCH * D_MODEL), "wbr": bf(w_branch), "wo": bf(w_out),
            "wp": bf(w_ple), "wpg": bf(w_ple_gate)}


def _trunk_layer(x, p, w, pw, layer, cache, batch, t, shared):
    x = _ffn(x, pw, layer, 0, 0, 1)
    pr, shared = _inproj(x, pw, layer, batch, t, shared)
    if cache is None:
        ya, h_new = _mamba(pr["z"], pr["xbc"], pr["dtwi"], None, None, w, batch, t)
        yb = _dsa(pr["qb"], pr["qi"], pr["dtwi"], pr["ki"], pr["kb"], pr["vb"], layer, batch, t, t, t, 0)
        yc = _band(pr["qc"], pr["kc_bf"], pr["vc_bf"], w["rel_bias"], batch, t, BAND_TQ, BAND_SUB, BAND_TQ,
                   BAND_KBLK, 0, 0, True)
        conv_src = pr["xbc"].reshape(batch, t, CONV_CH)
    else:
        past = cache["dsa_k"].shape[1]
        conv0 = jnp.pad(cache["conv"], ((0, 0), (SUBLANES - (CONV_W - 1), 0), (0, 0)))
        h0 = cache["ssm"].reshape(batch, D_INNER, D_STATE)
        ya, h_new = _mamba(pr["z"], pr["xbc"], pr["dtwi"], conv0, h0, w, batch, t)

        n_valid = past + t
        s_keys = -(-n_valid // DSA_KB) * DSA_KB
        tq_pad = -(-t // DSA_TQ) * DSA_TQ

        def with_cache(c, new):
            width = new.shape[-1]
            a = jnp.concatenate([c.reshape(batch, past, width), new[layer].reshape(batch, t, width)], axis=1)
            return jnp.pad(a, ((0, 0), (0, s_keys - n_valid), (0, 0))).reshape(1, batch * s_keys, width)

        def pad_q(a):
            a = jnp.pad(a.reshape(batch, t, a.shape[-1]), ((0, 0), (0, tq_pad - t), (0, 0)))
            return a.reshape(batch * tq_pad, a.shape[-1])

        yb = _dsa(pad_q(pr["qb"]), pad_q(pr["qi"]), pad_q(pr["dtwi"]), with_cache(cache["idx_k"], pr["ki"]),
                  with_cache(cache["dsa_k"], pr["kb"]), with_cache(cache["dsa_v"], pr["vb"]),
                  0, batch, tq_pad, s_keys, n_valid, past)
        yb = yb.reshape(batch, tq_pad, D_MODEL)[:, :t].reshape(batch * t, D_MODEL)

        nrows = cache["band_k"].shape[1]
        k_rows = -(-(nrows + t) // LANES) * LANES
        lead = k_rows - nrows - t

        def with_band(c, new):
            a = jnp.concatenate([c.reshape(batch, nrows, D_MODEL), new[layer].reshape(batch, t, D_MODEL)], axis=1)
            return jnp.pad(a, ((0, 0), (lead, 0), (0, 0))).reshape(batch * k_rows, D_MODEL)

        yc = _band(pr["qc"], with_band(cache["band_k"], pr["kc"]), with_band(cache["band_v"], pr["vc"]),
                   w["rel_bias"], batch, t, t, t, k_rows, 1, past, past - nrows, False)
        conv_src = jnp.concatenate([cache["conv"], pr["xbc"].reshape(batch, t, CONV_CH)], axis=1)

    x = _merge(x, ya, yb, yc, pw, layer)
    x = _ffn(x, pw, layer, 1, 4, 5)
    x = _ple(x, p.reshape(batch * t, PLE_DIM), pw, layer)

    state = (h_new.reshape(batch, SSM_HEADS, SSM_HEAD_DIM, D_STATE), conv_src[:, -(CONV_W - 1):])
    return x, state, shared


def _cache_outputs(shared, states, batch, t):
    band_rows = min(BAND, t)

    def band_tail(a):
        if a.ndim == 4:
            a = a.reshape(DEPTH, batch, BAND_HEADS, BAND_HEAD_DIM, band_rows)
            return jnp.transpose(a, (0, 1, 4, 2, 3))
        return a.reshape(DEPTH, batch, band_rows, BAND_HEADS, BAND_HEAD_DIM)

    return (shared["kb"].reshape(DEPTH, batch, t, KV_HEADS, DSA_HEAD_DIM),
            shared["vb"].reshape(DEPTH, batch, t, KV_HEADS, DSA_HEAD_DIM),
            shared["ki"].reshape(DEPTH, batch, t, IDX_DIM), band_tail(shared["kc"]), band_tail(shared["vc"]),
            jnp.stack([s[0] for s in states]), jnp.stack([s[1] for s in states]))


def kernel(x_prompt, x_sample, p_prompt, p_sample, cache_dsa_k, cache_dsa_v, cache_idx_k, cache_band_k,
           cache_band_v, state_ssm, state_conv, norm_g, ffn_w13, ffn_w2, w_in, conv_w, conv_b, dt_bias,
           a_log, d_skip, ssm_norm_g, rel_bias, w_gate, b_gate, w_branch, w_out, w_ple, w_ple_gate):
    bp, tp, _ = x_prompt.shape
    bs, ts, _ = x_sample.shape
    yp = x_prompt.reshape(bp * tp, D_MODEL)
    ys = x_sample.reshape(bs * ts, D_MODEL)
    st_p, st_s, shared_p, shared_s = [], [], None, None
    pw = _prep_weights(norm_g, ffn_w13, ffn_w2, w_in, w_gate, b_gate, w_branch, w_out, w_ple, w_ple_gate)
    for i in range(DEPTH):
        w = {"conv_w": conv_w[i], "conv_b": conv_b[i], "dt_bias": dt_bias[i], "a_log": a_log[i],
             "d_skip": d_skip[i], "ssm_norm_g": ssm_norm_g[i], "rel_bias": rel_bias[i]}
        yp, sp, shared_p = _trunk_layer(yp, p_prompt[i], w, pw, i, None, bp, tp, shared_p)
        st_p.append(sp)
        cache = {"dsa_k": cache_dsa_k[i], "dsa_v": cache_dsa_v[i], "idx_k": cache_idx_k[i],
                 "band_k": cache_band_k[i], "band_v": cache_band_v[i], "ssm": state_ssm[i], "conv": state_conv[i]}
        ys, ss, shared_s = _trunk_layer(ys, p_sample[i], w, pw, i, cache, bs, ts, shared_s)
        st_s.append(ss)
    return (yp.reshape(bp, tp, D_MODEL), ys.reshape(bs, ts, D_MODEL),
            *_cache_outputs(shared_p, st_p, bp, tp), *_cache_outputs(shared_s, st_s, bs, ts))
```
